```python
import math
import jax
import jax.numpy as jnp
from jax import lax
import numpy as np

D_MODEL = 1024
BATCH = 16
SEQ = 4096
DEPTH = 1
DEC_BATCH = 32
DEC_SEQ = 2048
PAST_LEN = 128

N_META = 16
RMS_EPS = 1e-6
HY_WIDTH = D_MODEL // 2
HY_SHORT = 3
FILT_BANDS = 16
FILT_EMB = 1 + 2 * FILT_BANDS
FILT_HIDDEN = 64
DECAY_TARGET = 1e-2
FAST_DECAY_PCT = 0.3
SLOW_DECAY_PCT = 1.5
ATT_HEADS = 4
ATT_HEAD_DIM = D_MODEL // 16
ATT_V_DIM = 2 * ATT_HEAD_DIM
ATT_WIDTH = ATT_HEADS * ATT_V_DIM
QBLOCK = 128
N_BUCKETS = 32
MAX_DISTANCE = 128
N_EXPERTS = 32
TOP_K = 4
D_FF = D_MODEL
SWIGLU_LIMIT = 7.0
SWIGLU_ALPHA = 1.702
MOE_BLOCK = 256
N_GATES = 2
COL_HY = 3 * HY_WIDTH
COL_Q = ATT_HEADS * 2 * ATT_HEAD_DIM
COL_K = ATT_HEADS * 2 * ATT_HEAD_DIM
COL_V = ATT_WIDTH
COL_G = N_GATES * D_MODEL
IN_COLS = COL_HY + COL_Q + COL_K + COL_V + COL_G

kernel_name = 'hyena_diffattn_moe_meta_encoder'


def _rmsnorm(x, g):
    xf = x.astype(jnp.float32)
    y = xf * lax.rsqrt(jnp.mean(xf * xf, axis=-1, keepdims=True) + RMS_EPS)
    return y.astype(x.dtype) * g.astype(x.dtype)


def _short_conv(u, w, b):
    n = u.shape[1]
    pad = HY_SHORT // 2
    up = jnp.pad(u, ((0, 0), (pad, pad), (0, 0)))
    y = up[:, 0:n] * w[0]
    for j in range(1, HY_SHORT):
        y = y + up[:, j:j + n] * w[j]
    return y + b


def _hyena_filter(L, w1, b1, fr1, w2, b2, fr2, w3):
    f32 = jnp.float32
    pos = jnp.arange(L, dtype=f32)[:, None]
    t = pos / max(L - 1, 1)
    w = 2.0 * math.pi * pos / L
    f = jnp.linspace(1e-4, FILT_BANDS - 1, FILT_BANDS, dtype=f32)[None]
    z = jnp.concatenate([t, jnp.cos(f * w), -jnp.sin(f * w)], axis=-1)
    h = jnp.sin(fr1.astype(f32) * (z @ w1.astype(f32) + b1.astype(f32)))
    h = jnp.sin(fr2.astype(f32) * (h @ w2.astype(f32) + b2.astype(f32)))
    h = h @ w3.astype(f32)
    max_decay = math.log(DECAY_TARGET) / FAST_DECAY_PCT
    min_decay = math.log(DECAY_TARGET) / SLOW_DECAY_PCT
    deltas = jnp.abs(jnp.linspace(min_decay, max_decay, HY_WIDTH, dtype=f32))
    decay = jnp.exp(-t * deltas[None])
    k_fwd = h[:, :HY_WIDTH] * decay
    k_bwd = h[:, HY_WIDTH:] * decay
    k_circ = jnp.concatenate([k_fwd, jnp.zeros((1, HY_WIDTH), f32), k_bwd[1:][::-1]], axis=0)
    return k_circ * lax.rsqrt(jnp.sum(k_circ * k_circ, axis=0, keepdims=True) + RMS_EPS)


def _fft_long_conv(u, k_circ, skip):
    L = u.shape[1]
    n = 2 * L
    u_f = jnp.fft.rfft(u.astype(jnp.float32), n=n, axis=1)
    k_f = jnp.fft.rfft(k_circ, n=n, axis=0)
    y = jnp.fft.irfft(u_f * k_f[None], n=n, axis=1)[:, :L]
    return y.astype(u.dtype) + u * skip.astype(u.dtype)


def _t5_bucket(rel):
    nb = N_BUCKETS // 2
    max_exact = nb // 2
    ret = jnp.where(rel > 0, nb, 0)
    n = jnp.abs(rel)
    nf = jnp.maximum(n, 1).astype(jnp.float32)
    large = max_exact + (jnp.log(nf / max_exact) / math.log(MAX_DISTANCE / max_exact) * (nb - max_exact)).astype(jnp.int32)
    large = jnp.minimum(large, nb - 1)
    return ret + jnp.where(n < max_exact, n, large)


def _diff_attention(q, k, v, rel_bias, q_norm_g, k_norm_g, lq1, lk1, lq2, lk2, subln_g, lam_init):
    B, N = q.shape[0], q.shape[1]
    q = _rmsnorm(q.reshape(B, N, ATT_HEADS, 2, ATT_HEAD_DIM), q_norm_g)
    k = _rmsnorm(k.reshape(B, N, ATT_HEADS, 2, ATT_HEAD_DIM), k_norm_g)
    v = v.reshape(B, N, ATT_HEADS, ATT_V_DIM)
    lam = (jnp.exp(jnp.sum(lq1.astype(jnp.float32) * lk1.astype(jnp.float32)))
           - jnp.exp(jnp.sum(lq2.astype(jnp.float32) * lk2.astype(jnp.float32))) + lam_init)
    kpos = jnp.arange(N, dtype=jnp.int32)
    scale = ATT_HEAD_DIM ** -0.5
    table = rel_bias.astype(jnp.float32)

    def attend(qb, qpos):
        s = jnp.einsum('bqhcd,bkhcd->bhcqk', qb, k).astype(jnp.float32) * scale
        bias = table[_t5_bucket(kpos[None, :] - qpos[:, None])]
        s = s + jnp.transpose(bias, (2, 0, 1))[None, :, None]
        p = jax.nn.softmax(s, axis=-1)
        a = (p[:, :, 0] - lam * p[:, :, 1]).astype(v.dtype)
        return jnp.einsum('bhqk,bkhe->bqhe', a, v)

    o_meta = attend(q[:, :N_META], jnp.arange(N_META, dtype=jnp.int32))
    S = N - N_META
    nb = S // QBLOCK
    qr = q[:, N_META:].reshape(B, nb, QBLOCK, ATT_HEADS, 2, ATT_HEAD_DIM).transpose(1, 0, 2, 3, 4, 5)

    def body(args):
        qb, i = args
        return attend(qb, N_META + i * QBLOCK + jnp.arange(QBLOCK, dtype=jnp.int32))

    o_real = lax.map(body, (qr, jnp.arange(nb, dtype=jnp.int32)))
    o_real = o_real.transpose(1, 0, 2, 3, 4).reshape(B, S, ATT_HEADS, ATT_V_DIM)
    o = jnp.concatenate([o_meta, o_real], axis=1)
    o = _rmsnorm(o, subln_g) * (1.0 - lam_init)
    return o.reshape(B, N, ATT_WIDTH)


def _moe(h, w_router, b_router, w1, b1, w2, b2):
    T, D = h.shape
    logits = (h @ w_router).astype(jnp.float32) + b_router.astype(jnp.float32)
    top_v, top_i = lax.top_k(logits, TOP_K)
    gates = jax.nn.softmax(top_v, axis=-1).astype(h.dtype)
    A = T * TOP_K
    flat_e = top_i.reshape(A)
    order = jnp.argsort(flat_e)
    sorted_e = flat_e[order]
    sorted_tok = (order // TOP_K).astype(jnp.int32)
    sorted_gate = gates.reshape(A)[order]
    counts = jnp.bincount(flat_e, length=N_EXPERTS)
    padded = ((counts + MOE_BLOCK - 1) // MOE_BLOCK) * MOE_BLOCK
    pad_end = jnp.cumsum(padded)
    pad_start = pad_end - padded
    start = jnp.cumsum(counts) - counts
    dest = pad_start[sorted_e] + (jnp.arange(A, dtype=jnp.int32) - start[sorted_e])
    n_blocks = (A + N_EXPERTS * (MOE_BLOCK - 1) + MOE_BLOCK - 1) // MOE_BLOCK
    P = n_blocks * MOE_BLOCK
    slot_tok = jnp.full((P,), T, jnp.int32).at[dest].set(sorted_tok)
    block_e = jnp.minimum(jnp.searchsorted(pad_end, jnp.arange(n_blocks, dtype=jnp.int32) * MOE_BLOCK, side='right'), N_EXPERTS - 1)
    h_pad = jnp.concatenate([h, jnp.zeros((1, D), h.dtype)], axis=0)
    xs = h_pad[slot_tok].reshape(n_blocks, MOE_BLOCK, D)

    def expert_block(args):
        xb, e = args
        u = xb @ w1[e] + b1[e]
        glu = jnp.minimum(u[:, :D_FF], SWIGLU_LIMIT)
        lin = jnp.clip(u[:, D_FF:], -SWIGLU_LIMIT, SWIGLU_LIMIT)
        a = glu * jax.nn.sigmoid(SWIGLU_ALPHA * glu) * (lin + 1.0)
        return a @ w2[e] + b2[e]

    ys = lax.map(expert_block, (xs, block_e)).reshape(P, D)
    y_assign = ys[dest] * sorted_gate[:, None]
    return jax.ops.segment_sum(y_assign, sorted_tok, num_segments=T)


def _layer(x, rel_bias, p, l):
    B, N, D = x.shape
    lam_init = 0.8 - 0.6 * math.exp(-0.3 * l)
    h = _rmsnorm(x, p['norm1_g'][l])
    proj = h @ p['w_in'][l]
    c0 = COL_HY
    c1 = c0 + COL_Q
    c2 = c1 + COL_K
    c3 = c2 + COL_V
    hy, q, k, v, g = proj[..., :c0], proj[..., c0:c1], proj[..., c1:c2], proj[..., c2:c3], proj[..., c3:]
    hy = _short_conv(hy, p['short_conv_w'][l], p['short_conv_b'][l])
    x0, x1, vh = hy[..., :HY_WIDTH], hy[..., HY_WIDTH:2 * HY_WIDTH], hy[..., 2 * HY_WIDTH:]
    k_circ = _hyena_filter(N, p['filt_w1'][l], p['filt_b1'][l], p['filt_freq1'][l],
                           p['filt_w2'][l], p['filt_b2'][l], p['filt_freq2'][l], p['filt_w3'][l])
    y_a = x0 * _fft_long_conv(x1 * vh, k_circ, p['hy_skip'][l])
    y_b = _diff_attention(q, k, v, rel_bias, p['q_norm_g'][l], p['k_norm_g'][l],
                          p['lambda_q1'][l], p['lambda_k1'][l], p['lambda_q2'][l], p['lambda_k2'][l],
                          p['subln_g'][l], lam_init)
    g_a, g_b = g[..., :D_MODEL], g[..., D_MODEL:]
    m = jax.nn.sigmoid(g_a) * (y_a @ p['proj_a'][l]) + jax.nn.sigmoid(g_b) * (y_b @ p['proj_b'][l])
    x = x + m @ p['w_out'][l]
    h2 = _rmsnorm(x, p['norm2_g'][l]).reshape(B * N, D)
    y = _moe(h2, p['w_router'][l], p['b_router'][l], p['w_mlp1'][l], p['b_mlp1'][l], p['w_mlp2'][l], p['b_mlp2'][l])
    return x + y.reshape(B, N, D)


def _trunk(x, meta_tokens, rel_bias, p):
    B = x.shape[0]
    meta = jnp.broadcast_to(meta_tokens.astype(x.dtype)[None], (B, N_META, D_MODEL))
    h = jnp.concatenate([meta, x], axis=1)
    for l in range(DEPTH):
        h = _layer(h, rel_bias, p, l)
    return h[:, N_META:]


def setup_inputs(seed: int = 0) -> dict:
    key = jax.random.key(seed)
    ks = jax.random.split(key, 40)

    def nrm(i, shape, s):
        return jax.random.normal(ks[i], shape, jnp.float32) * s

    return {
        'x_prompt': nrm(0, (BATCH, SEQ, D_MODEL), 1.0),
        'x_sample': nrm(1, (DEC_BATCH, DEC_SEQ, D_MODEL), 1.0),
        'meta_tokens': nrm(2, (N_META, D_MODEL), 1.0),
        'rel_bias': nrm(3, (N_BUCKETS, ATT_HEADS), 0.2),
        'norm1_g': 1.0 + nrm(4, (DEPTH, D_MODEL), 0.02),
        'w_in': nrm(5, (DEPTH, D_MODEL, IN_COLS), D_MODEL ** -0.5),
        'short_conv_w': nrm(6, (DEPTH, HY_SHORT, COL_HY), HY_SHORT ** -0.5),
        'short_conv_b': nrm(7, (DEPTH, COL_HY), 0.02),
        'filt_w1': nrm(8, (DEPTH, FILT_EMB, FILT_HIDDEN), FILT_EMB ** -0.5),
        'filt_b1': nrm(9, (DEPTH, FILT_HIDDEN), 0.02),
        'filt_freq1': 1.0 + nrm(10, (DEPTH, FILT_HIDDEN), 0.01),
        'filt_w2': nrm(11, (DEPTH, FILT_HIDDEN, FILT_HIDDEN), FILT_HIDDEN ** -0.5),
        'filt_b2': nrm(12, (DEPTH, FILT_HIDDEN), 0.02),
        'filt_freq2': 1.0 + nrm(13, (DEPTH, FILT_HIDDEN), 0.01),
        'filt_w3': nrm(14, (DEPTH, FILT_HIDDEN, 2 * HY_WIDTH), FILT_HIDDEN ** -0.5),
        'hy_skip': nrm(15, (DEPTH, HY_WIDTH), 1.0),
        'proj_a': nrm(16, (DEPTH, HY_WIDTH, D_MODEL), HY_WIDTH ** -0.5),
        'q_norm_g': 1.0 + nrm(17, (DEPTH, ATT_HEAD_DIM), 0.02),
        'k_norm_g': 1.0 + nrm(18, (DEPTH, ATT_HEAD_DIM), 0.02),
        'lambda_q1': nrm(19, (DEPTH, ATT_HEAD_DIM), 0.1),
        'lambda_k1': nrm(20, (DEPTH, ATT_HEAD_DIM), 0.1),
        'lambda_q2': nrm(21, (DEPTH, ATT_HEAD_DIM), 0.1),
        'lambda_k2': nrm(22, (DEPTH, ATT_HEAD_DIM), 0.1),
        'subln_g': 1.0 + nrm(23, (DEPTH, ATT_V_DIM), 0.02),
        'proj_b': nrm(24, (DEPTH, ATT_WIDTH, D_MODEL), ATT_WIDTH ** -0.5),
        'w_out': nrm(25, (DEPTH, D_MODEL, D_MODEL), D_MODEL ** -0.5),
        'norm2_g': 1.0 + nrm(26, (DEPTH, D_MODEL), 0.02),
        'w_router': nrm(27, (DEPTH, D_MODEL, N_EXPERTS), D_MODEL ** -0.5),
        'b_router': nrm(28, (DEPTH, N_EXPERTS), 0.01),
        'w_mlp1': nrm(29, (DEPTH, N_EXPERTS, D_MODEL, 2 * D_FF), D_MODEL ** -0.5),
        'b_mlp1': nrm(30, (DEPTH, N_EXPERTS, 2 * D_FF), 0.02),
        'w_mlp2': nrm(31, (DEPTH, N_EXPERTS, D_FF, D_MODEL), D_FF ** -0.5),
        'b_mlp2': nrm(32, (DEPTH, N_EXPERTS, D_MODEL), 0.02),
    }


def reference(x_prompt, x_sample, meta_tokens, rel_bias, norm1_g, w_in, short_conv_w, short_conv_b,
              filt_w1, filt_b1, filt_freq1, filt_w2, filt_b2, filt_freq2, filt_w3, hy_skip, proj_a,
              q_norm_g, k_norm_g, lambda_q1, lambda_k1, lambda_q2, lambda_k2, subln_g, proj_b, w_out,
              norm2_g, w_router, b_router, w_mlp1, b_mlp1, w_mlp2, b_mlp2):
    p = {
        'norm1_g': norm1_g, 'w_in': w_in, 'short_conv_w': short_conv_w, 'short_conv_b': short_conv_b,
        'filt_w1': filt_w1, 'filt_b1': filt_b1, 'filt_freq1': filt_freq1,
        'filt_w2': filt_w2, 'filt_b2': filt_b2, 'filt_freq2': filt_freq2, 'filt_w3': filt_w3,
        'hy_skip': hy_skip, 'proj_a': proj_a, 'q_norm_g': q_norm_g, 'k_norm_g': k_norm_g,
        'lambda_q1': lambda_q1, 'lambda_k1': lambda_k1, 'lambda_q2': lambda_q2, 'lambda_k2': lambda_k2,
        'subln_g': subln_g, 'proj_b': proj_b, 'w_out': w_out, 'norm2_g': norm2_g,
        'w_router': w_router, 'b_router': b_router, 'w_mlp1': w_mlp1, 'b_mlp1': b_mlp1,
        'w_mlp2': w_mlp2, 'b_mlp2': b_mlp2,
    }
    y_prompt = _trunk(x_prompt, meta_tokens, rel_bias, p)
    y_sample = _trunk(x_sample, meta_tokens, rel_bias, p)
    return (y_prompt, y_sample)
```

```python
import functools
import math

import jax
import jax.numpy as jnp
import numpy as np
from jax import lax
from jax.experimental import pallas as pl
from jax.experimental.pallas import tpu as pltpu

F32 = jnp.float32
BF16 = jnp.bfloat16

D_MODEL = 1024
N_META = 16
RMS_EPS = 1e-6
HY_WIDTH = 512
HY_SHORT = 3
FILT_BANDS = 16
FILT_EMB = 1 + 2 * FILT_BANDS
FILT_HIDDEN = 64
DECAY_TARGET = 1e-2
FAST_DECAY_PCT = 0.3
SLOW_DECAY_PCT = 1.5
ATT_HEADS = 4
ATT_HEAD_DIM = 64
ATT_V_DIM = 128
ATT_WIDTH = 512
N_BUCKETS = 32
MAX_DISTANCE = 128
N_EXPERTS = 32
TOP_K = 4
D_FF = 1024
SWIGLU_LIMIT = 7.0
SWIGLU_ALPHA = 1.702
COL_HY = 3 * HY_WIDTH
COL_Q = 512
COL_K = 512
COL_V = 512
COL_G = 2 * D_MODEL
IN_COLS = COL_HY + COL_Q + COL_K + COL_V + COL_G
LAM_INIT = 0.8 - 0.6 * math.exp(-0.3 * 0)

V7X_VMEM_BYTES = 64 * 1024 * 1024
LANES = 128
CONV_BLK = 256
HY_PAD = 512


def _cparams(sem, vmem_mb):
    return pltpu.CompilerParams(dimension_semantics=sem, vmem_limit_bytes=vmem_mb * 1024 * 1024)


def _const_spec(shape):
    nd = len(shape)
    return pl.BlockSpec(shape, lambda *_: (0,) * nd, pipeline_mode=pl.Buffered(1))


def _prep_kernel(x_ref, g1_ref, w_ref, bd_ref, qg_ref, kg_ref, hy_ref, q_ref, k_ref, v_ref, sg_ref):
    x = x_ref[...]
    ms = jnp.mean(x * x, axis=-1, keepdims=True)
    h = (x * lax.rsqrt(ms + RMS_EPS) * g1_ref[...]).astype(BF16)

    def proj(lo, hi):
        return jnp.dot(h, w_ref[:, lo:hi], preferred_element_type=F32)

    def headnorm(t, g):
        msq = jnp.dot((t * t).astype(BF16), bd_ref[...], preferred_element_type=F32)
        return t * lax.rsqrt(msq + RMS_EPS) * g

    c0, c1, c2, c3 = COL_HY, COL_HY + COL_Q, COL_HY + COL_Q + COL_K, COL_HY + COL_Q + COL_K + COL_V
    hy_ref[...] = proj(0, c0).astype(BF16)
    q_ref[...] = headnorm(proj(c0, c1), qg_ref[...]).astype(BF16)
    k_ref[...] = headnorm(proj(c1, c2), kg_ref[...]).astype(BF16)
    v_ref[...] = proj(c2, c3).astype(BF16)
    sg_ref[...] = jax.nn.sigmoid(proj(c3, IN_COLS)).astype(BF16)


def _prep(x, g1, w_in_bf, bd, qg, kg, *, tm, hy_pad_blocks):
    B, S, D = x.shape
    nt = S // tm
    tok = lambda w: pl.BlockSpec((None, tm, w), lambda b, i: (b, i, 0))
    return pl.pallas_call(
        _prep_kernel,
        grid=(B, nt),
        in_specs=[tok(D), _const_spec((1, D)), _const_spec((D, IN_COLS)), _const_spec((COL_Q, COL_Q)),
                  _const_spec((1, COL_Q)), _const_spec((1, COL_K))],
        out_specs=[pl.BlockSpec((None, tm, COL_HY), lambda b, i: (b, i + hy_pad_blocks, 0)),
                   tok(COL_Q), tok(COL_K), tok(COL_V), tok(COL_G)],
        out_shape=[jax.ShapeDtypeStruct((B, S + hy_pad_blocks * tm, COL_HY), BF16),
                   jax.ShapeDtypeStruct((B, S, COL_Q), BF16), jax.ShapeDtypeStruct((B, S, COL_K), BF16),
                   jax.ShapeDtypeStruct((B, S, COL_V), BF16), jax.ShapeDtypeStruct((B, S, COL_G), BF16)],
        compiler_params=_cparams(("parallel", "parallel"), 48),
        name="prep",
    )(x, g1, w_in_bf, bd, qg, kg)


def _filt_kernel(w1t_ref, w1c_ref, w1s_ref, b1_ref, fr1_ref, w2t_ref, b2_ref, fr2_ref, w3f_ref, w3b_ref,
                 bands_ref, deltas_ref, kt_ref, hid_ref, *, n_seq, s_real):
    lk = kt_ref.shape[1]
    hi = lax.Precision.HIGHEST
    u = lax.broadcasted_iota(jnp.int32, (1, lk), 1)
    d = u - s_real
    pos = jnp.abs(d).astype(F32)
    t = pos / float(max(n_seq - 1, 1))

    @pl.when(pl.program_id(0) == 0)
    def _():
        w = 2.0 * math.pi * pos / float(n_seq)
        ang = bands_ref[...] * w
        pre = (w1t_ref[...] * t
               + jnp.dot(w1c_ref[...], jnp.cos(ang), precision=hi, preferred_element_type=F32)
               - jnp.dot(w1s_ref[...], jnp.sin(ang), precision=hi, preferred_element_type=F32)
               + b1_ref[...])
        h1 = jnp.sin(fr1_ref[...] * pre)
        h2 = jnp.sin(fr2_ref[...] * (jnp.dot(w2t_ref[...], h1, precision=hi, preferred_element_type=F32)
                                     + b2_ref[...]))
        hid_ref[...] = h2

    h2 = hid_ref[...]
    hf = jnp.dot(w3f_ref[...], h2, precision=hi, preferred_element_type=F32)
    hb = jnp.dot(w3b_ref[...], h2, precision=hi, preferred_element_type=F32)
    decay = jnp.exp(-t * deltas_ref[...])
    valid = pos <= float(n_seq - 1)
    kun = jnp.where(valid, jnp.where(d >= 0, hf, hb) * decay, 0.0)
    nrm = lax.rsqrt(jnp.sum(kun * kun, axis=1, keepdims=True) + RMS_EPS)
    kt_ref[...] = kun * nrm


def _hyena_filter(fw1, fb1, ffr1, fw2, fb2, ffr2, fw3, *, s_real):
    n_seq = s_real + N_META
    lk = 2 * s_real + CONV_BLK
    cf = 64
    col = lambda a: a.reshape(-1, 1)
    w1t = fw1.T
    bands = jnp.linspace(1e-4, FILT_BANDS - 1, FILT_BANDS, dtype=F32).reshape(-1, 1)
    max_decay = math.log(DECAY_TARGET) / FAST_DECAY_PCT
    min_decay = math.log(DECAY_TARGET) / SLOW_DECAY_PCT
    deltas = jnp.abs(jnp.linspace(min_decay, max_decay, HY_WIDTH, dtype=F32)).reshape(-1, 1)
    w3t = fw3.T
    h = FILT_HIDDEN
    return pl.pallas_call(
        functools.partial(_filt_kernel, n_seq=n_seq, s_real=s_real),
        grid=(HY_WIDTH // cf,),
        in_specs=[_const_spec((h, 1)), _const_spec((h, FILT_BANDS)), _const_spec((h, FILT_BANDS)),
                  _const_spec((h, 1)), _const_spec((h, 1)), _const_spec((h, h)), _const_spec((h, 1)),
                  _const_spec((h, 1)),
                  pl.BlockSpec((cf, h), lambda i: (i, 0)), pl.BlockSpec((cf, h), lambda i: (i, 0)),
                  _const_spec((FILT_BANDS, 1)), pl.BlockSpec((cf, 1), lambda i: (i, 0))],
        out_specs=pl.BlockSpec((cf, lk), lambda i: (i, 0)),
        out_shape=jax.ShapeDtypeStruct((HY_WIDTH, lk), F32),
        scratch_shapes=[pltpu.VMEM((h, lk), F32)],
        compiler_params=_cparams(("arbitrary",), 48),
        name="hyena_filter",
    )(w1t[:, 0:1], w1t[:, 1:1 + FILT_BANDS], w1t[:, 1 + FILT_BANDS:], col(fb1), col(ffr1), fw2.T, col(fb2),
      col(ffr2), w3t[:HY_WIDTH], w3t[HY_WIDTH:], bands, deltas)


def _hy_pre_kernel(x0_ref, x1_ref, v_ref, p0_ref, p1_ref, pv_ref, n0_ref, n1_ref, nv_ref,
                   scw_ref, scb_ref, skip_ref, zt_ref, a_ref, bt_ref, zs_ref):
    jt = pl.program_id(1)
    last = pl.num_programs(1) - 1
    nb_batch = x0_ref.shape[0]
    row = lax.broadcasted_iota(jnp.int32, (CONV_BLK, LANES), 0)
    keep_next = (jt < last).astype(F32)
    z_keep = jnp.logical_or(jt > 0, row >= CONV_BLK - N_META)

    def short_conv(cur_ref, prev_ref, next_ref, comp, b):
        cur = cur_ref[b].astype(F32)
        prev = prev_ref[b, 15:16, :].astype(F32)
        nxt = next_ref[b, 0:1, :].astype(F32) * keep_next
        up = jnp.where(row == 0, prev, pltpu.roll(cur, 1, 0))
        dn = jnp.where(row == CONV_BLK - 1, nxt, pltpu.roll(cur, CONV_BLK - 1, 0))
        w = scw_ref[comp]
        return up * w[0:1] + cur * w[1:2] + dn * w[2:3] + scb_ref[comp]

    def body(b, carry):
        x0c = short_conv(x0_ref, p0_ref, n0_ref, 0, b)
        x1c = short_conv(x1_ref, p1_ref, n1_ref, 1, b)
        vc = short_conv(v_ref, pv_ref, nv_ref, 2, b)
        z = jnp.where(z_keep, x1c * vc, 0.0)
        zs_ref[b] = z.T
        a_ref[b] = x0c.astype(BF16)
        bt_ref[b] = (x0c * z * skip_ref[...]).astype(BF16)
        return carry

    lax.fori_loop(0, nb_batch, body, 0)
    zt_ref[...] = pltpu.einshape("bcl->cbl", zs_ref[...]).astype(BF16)


def _hy_pre(hy_ext, scw, scb, skip, *, s_real):
    B = hy_ext.shape[0]
    nb = s_real // CONV_BLK
    ncb = HY_WIDTH // LANES
    sub = 16
    r16 = CONV_BLK // sub
    last16 = (s_real + HY_PAD) // sub - 1
    cur = lambda comp: pl.BlockSpec((B, CONV_BLK, LANES), lambda cb, jt: (0, jt + 1, comp * ncb + cb))
    prv = lambda comp: pl.BlockSpec((B, sub, LANES), lambda cb, jt: (0, r16 * (jt + 1) - 1, comp * ncb + cb))
    nxt = lambda comp: pl.BlockSpec(
        (B, sub, LANES), lambda cb, jt: (0, jnp.minimum(r16 * (jt + 2), last16), comp * ncb + cb))
    nat = pl.BlockSpec((B, CONV_BLK, LANES), lambda cb, jt: (0, jnp.maximum(jt - 1, 0), cb))
    scw_r = scw.reshape(HY_SHORT, 3, ncb, LANES).transpose(2, 1, 0, 3)
    scb_r = scb.reshape(3, ncb, 1, LANES).transpose(1, 0, 2, 3)
    skip_r = skip.reshape(ncb, 1, LANES)
    return pl.pallas_call(
        _hy_pre_kernel,
        grid=(ncb, nb + 1),
        in_specs=[cur(0), cur(1), cur(2), prv(0), prv(1), prv(2), nxt(0), nxt(1), nxt(2),
                  pl.BlockSpec((None, 3, HY_SHORT, LANES), lambda cb, jt: (cb, 0, 0, 0)),
                  pl.BlockSpec((None, 3, 1, LANES), lambda cb, jt: (cb, 0, 0, 0)),
                  pl.BlockSpec((None, 1, LANES), lambda cb, jt: (cb, 0, 0))],
        out_specs=[pl.BlockSpec((LANES, None, B, CONV_BLK), lambda cb, jt: (cb, jt, 0, 0)), nat, nat],
        out_shape=[jax.ShapeDtypeStruct((HY_WIDTH, nb + 1, B, CONV_BLK), BF16),
                   jax.ShapeDtypeStruct((B, s_real, HY_WIDTH), BF16),
                   jax.ShapeDtypeStruct((B, s_real, HY_WIDTH), BF16)],
        scratch_shapes=[pltpu.VMEM((B, LANES, CONV_BLK), F32)],
        compiler_params=_cparams(("parallel", "arbitrary"), 48),
        name="hyena_pre",
    )(hy_ext, hy_ext, hy_ext, hy_ext, hy_ext, hy_ext, hy_ext, hy_ext, hy_ext, scw_r, scb_r, skip_r)


def _hy_conv_kernel(kt_ref, z_ref, y_ref, g_ref, *, nb):
    cbk = z_ref.shape[0]
    B = z_ref.shape[2]
    s_real = nb * CONV_BLK
    nq = 2 * s_real // LANES + 1

    def channel(c, carry):
        def build(q, carry2):
            off = pl.multiple_of(2 * s_real - LANES * q, LANES)
            w = kt_ref[pl.ds(c, 1), pl.ds(off, 2 * LANES)]
            rolled = pltpu.roll(jnp.broadcast_to(w, (LANES, 2 * LANES)), 0, 1, stride=1, stride_axis=0)
            g_ref[pl.ds(pl.multiple_of(LANES * q, LANES), LANES), :] = rolled[:, LANES:].astype(BF16)
            return carry2

        lax.fori_loop(0, nq, build, 0)
        y_ref[c] = jnp.zeros(y_ref.shape[1:], F32)
        for delta in range(-(nb - 1), nb + 1):
            base = CONV_BLK * (nb - delta)
            tile = jnp.concatenate([g_ref[base + LANES:base + LANES + CONV_BLK, :],
                                    g_ref[base:base + CONV_BLK, :]], axis=1)
            jb_lo = max(0, 1 - delta)
            n = min(nb, nb - delta) - jb_lo + 1
            o_lo = jb_lo + delta - 1
            lhs = z_ref[c, jb_lo:jb_lo + n].reshape(n * B, CONV_BLK)
            y_ref[c, o_lo:o_lo + n] += jnp.dot(lhs, tile, preferred_element_type=F32).reshape(n, B, CONV_BLK)
        return carry

    lax.fori_loop(0, cbk, channel, 0)


def _hy_conv(kt, zt, *, cbk=8):
    C, nb1, B, _ = zt.shape
    nb = nb1 - 1
    lk = kt.shape[1]
    return pl.pallas_call(
        functools.partial(_hy_conv_kernel, nb=nb),
        grid=(C // cbk,),
        in_specs=[pl.BlockSpec((cbk, lk), lambda i: (i, 0)),
                  pl.BlockSpec((cbk, nb1, B, CONV_BLK), lambda i: (i, 0, 0, 0))],
        out_specs=pl.BlockSpec((cbk, nb, B, CONV_BLK), lambda i: (i, 0, 0, 0)),
        out_shape=jax.ShapeDtypeStruct((C, nb, B, CONV_BLK), F32),
        scratch_shapes=[pltpu.VMEM((2 * nb * CONV_BLK + LANES, LANES), BF16)],
        compiler_params=_cparams(("parallel",), 48),
        name="hyena_conv",
    )(kt, zt)


def _hy_post_kernel(yt_ref, a_ref, bt_ref, o_ref, ys_ref):
    ys_ref[...] = pltpu.einshape("cbl->bcl", yt_ref[...])

    def body(b, carry):
        y = ys_ref[b].T
        o_ref[b] = (a_ref[b].astype(F32) * y + bt_ref[b].astype(F32)).astype(BF16)
        return carry

    lax.fori_loop(0, a_ref.shape[0], body, 0)


def _hy_post(yt, a, bt):
    C, nb, B, _ = yt.shape
    nat = pl.BlockSpec((B, CONV_BLK, LANES), lambda cb, ib: (0, ib, cb))
    return pl.pallas_call(
        _hy_post_kernel,
        grid=(C // LANES, nb),
        in_specs=[pl.BlockSpec((LANES, None, B, CONV_BLK), lambda cb, ib: (cb, ib, 0, 0)), nat, nat],
        out_specs=nat,
        out_shape=jax.ShapeDtypeStruct(a.shape, BF16),
        scratch_shapes=[pltpu.VMEM((B, LANES, CONV_BLK), F32)],
        compiler_params=_cparams(("parallel", "parallel"), 48),
        name="hyena_post",
    )(yt, a, bt)


def _attn_kernel(far_ref, q_ref, k_ref, v_ref, km_ref, vm_ref, near_ref, bmeta_ref, lam_ref, sg_ref,
                 o_ref, m_ref, l_ref, acc_ref, *, blk):
    h = pl.program_id(1)
    i = pl.program_id(2)
    nk = k_ref.shape[0] // blk
    nt = (((1,), (1,)), ((), ()))
    q = q_ref[...]
    qs = (q[:, :ATT_HEAD_DIM], q[:, ATT_HEAD_DIM:])

    def step(kblk, vblk, bias, first=False):
        for c in range(2):
            kc = kblk[:, c * ATT_HEAD_DIM:(c + 1) * ATT_HEAD_DIM]
            s = lax.dot_general(qs[c], kc, nt, preferred_element_type=F32) + bias
            smax = jnp.max(s, axis=1, keepdims=True)
            if first:
                m_new = smax
                p = jnp.exp(s - m_new)
                l_ref[c] = jnp.sum(p, axis=1, keepdims=True)
                acc_ref[c] = jnp.dot(p.astype(BF16), vblk, preferred_element_type=F32)
            else:
                m_old = m_ref[c]
                m_new = jnp.maximum(m_old, smax)
                alpha = jnp.exp(m_old - m_new)
                p = jnp.exp(s - m_new)
                l_ref[c] = alpha * l_ref[c] + jnp.sum(p, axis=1, keepdims=True)
                acc_ref[c] = alpha * acc_ref[c] + jnp.dot(p.astype(BF16), vblk, preferred_element_type=F32)
            m_ref[c] = m_new

    def kv(j):
        r0 = pl.multiple_of(j * blk, blk)
        return k_ref[pl.ds(r0, blk), :], v_ref[pl.ds(r0, blk), :]

    step(km_ref[...], vm_ref[...], bmeta_ref[jnp.minimum(i, 1)], first=True)

    def far_loop(lo, hi, c_far):
        def body(j, carry):
            kb, vb = kv(j)
            step(kb, vb, c_far)
            return carry
        lax.fori_loop(lo, hi, body, 0)

    far_loop(0, jnp.maximum(i - 1, 0), far_ref[2 * h])
    for dj in (-1, 0, 1):
        @pl.when(jnp.logical_and(i + dj >= 0, i + dj < nk))
        def _():
            kb, vb = kv(i + dj)
            step(kb, vb, near_ref[dj + 1])
    far_loop(jnp.minimum(i + 2, nk), nk, far_ref[2 * h + 1])

    lq = lam_ref[...]
    lam = (jnp.exp(jnp.sum(lq[0:1] * lq[1:2], axis=1, keepdims=True))
           - jnp.exp(jnp.sum(lq[2:3] * lq[3:4], axis=1, keepdims=True)) + LAM_INIT)
    o = acc_ref[0] / l_ref[0] - lam * (acc_ref[1] / l_ref[1])
    o = o * lax.rsqrt(jnp.mean(o * o, axis=1, keepdims=True) + RMS_EPS) * sg_ref[...] * (1.0 - LAM_INIT)
    o_ref[...] = o.astype(BF16)


def _t5_bucket(rel):
    nb = N_BUCKETS // 2
    max_exact = nb // 2
    ret = jnp.where(rel > 0, nb, 0)
    n = jnp.abs(rel)
    nf = jnp.maximum(n, 1).astype(F32)
    large = max_exact + (jnp.log(nf / max_exact) / math.log(MAX_DISTANCE / max_exact)
                         * (nb - max_exact)).astype(jnp.int32)
    large = jnp.minimum(large, nb - 1)
    return ret + jnp.where(n < max_exact, n, large)


def _attn_bias_tables(rel_bias, blk):
    table = rel_bias.astype(F32)
    r = jnp.arange(blk, dtype=jnp.int32)
    rel = jnp.stack([dj * blk + r[None, :] - r[:, None] for dj in (-1, 0, 1)])
    near = jnp.transpose(table[_t5_bucket(rel)], (3, 0, 1, 2))
    m = jnp.arange(N_META, dtype=jnp.int32)
    rel_m0 = m[None, :] - (N_META + r[:, None])
    rel_m1 = rel_m0 - blk
    bmeta = jnp.transpose(table[_t5_bucket(jnp.stack([rel_m0, rel_m1]))], (3, 0, 1, 2))
    far = table[_t5_bucket(jnp.array([-(blk + 1), blk + 1], jnp.int32))]
    return near, bmeta, far.T.reshape(-1)


def _attention(q, k, v, kmeta, vmeta, near, bmeta, far, lam, subln_g, *, blk):
    B, S, _ = q.shape
    tile = pl.BlockSpec((None, blk, ATT_V_DIM), lambda b, h, i, *_: (b, i, h))
    full = pl.BlockSpec((None, S, ATT_V_DIM), lambda b, h, i, *_: (b, 0, h))
    meta = pl.BlockSpec((N_META, ATT_V_DIM), lambda b, h, i, *_: (0, h))
    grid_spec = pltpu.PrefetchScalarGridSpec(
        num_scalar_prefetch=1,
        grid=(B, ATT_HEADS, S // blk),
        in_specs=[tile, full, full, meta, meta,
                  pl.BlockSpec((None, 3, blk, blk), lambda b, h, i, *_: (h, 0, 0, 0)),
                  pl.BlockSpec((None, 2, blk, N_META), lambda b, h, i, *_: (h, 0, 0, 0)),
                  pl.BlockSpec((4, ATT_HEAD_DIM), lambda b, h, i, *_: (0, 0)),
                  pl.BlockSpec((1, ATT_V_DIM), lambda b, h, i, *_: (0, 0))],
        out_specs=tile,
        scratch_shapes=[pltpu.VMEM((2, blk, 1), F32), pltpu.VMEM((2, blk, 1), F32),
                        pltpu.VMEM((2, blk, ATT_V_DIM), F32)],
    )
    return pl.pallas_call(
        functools.partial(_attn_kernel, blk=blk),
        grid_spec=grid_spec,
        out_shape=jax.ShapeDtypeStruct((B, S, ATT_WIDTH), BF16),
        compiler_params=_cparams(("parallel", "parallel", "arbitrary"), 48),
        name="diff_attention",
    )(far, q, k, v, kmeta, vmeta, near, bmeta, lam, subln_g)


def _merge_kernel(x_ref, ya_ref, yb_ref, sg_ref, pa_ref, pb_ref, wo_ref, g2_ref, wr_ref, br_ref,
                  x1_ref, h2_ref, ti_ref, tg_ref):
    ma = jnp.dot(ya_ref[...], pa_ref[...], preferred_element_type=F32)
    mb = jnp.dot(yb_ref[...], pb_ref[...], preferred_element_type=F32)
    m = sg_ref[:, :D_MODEL].astype(F32) * ma + sg_ref[:, D_MODEL:].astype(F32) * mb
    x1 = x_ref[...] + jnp.dot(m.astype(BF16), wo_ref[...], preferred_element_type=F32)
    x1_ref[...] = x1
    h2 = x1 * lax.rsqrt(jnp.mean(x1 * x1, axis=-1, keepdims=True) + RMS_EPS) * g2_ref[...]
    h2_ref[...] = h2.astype(BF16)
    logits = jnp.dot(h2, wr_ref[...], precision=lax.Precision.HIGHEST, preferred_element_type=F32) + br_ref[...]
    lane = lax.broadcasted_iota(jnp.int32, logits.shape, 1)
    slot = lax.broadcasted_iota(jnp.int32, ti_ref.shape, 1)
    top_i = jnp.zeros(ti_ref.shape, jnp.int32)
    top_v = jnp.zeros(tg_ref.shape, F32)
    work = logits
    for kk in range(TOP_K):
        mx = jnp.max(work, axis=1, keepdims=True)
        idx = jnp.min(jnp.where(work == mx, lane, N_EXPERTS), axis=1, keepdims=True)
        top_i = jnp.where(slot == kk, idx, top_i)
        top_v = jnp.where(slot == kk, mx, top_v)
        work = jnp.where(lane == idx, -jnp.inf, work)
    ex = jnp.exp(top_v - jnp.max(top_v, axis=1, keepdims=True))
    ti_ref[...] = top_i
    tg_ref[...] = ex / jnp.sum(ex, axis=1, keepdims=True)


def _merge(x, ya, yb, sg, pa, pb, wo, g2, wr, br, *, tm):
    T, D = x.shape
    tok = lambda w: pl.BlockSpec((tm, w), lambda i: (i, 0))
    return pl.pallas_call(
        _merge_kernel,
        grid=(T // tm,),
        in_specs=[tok(D), tok(HY_WIDTH), tok(ATT_WIDTH), tok(COL_G), _const_spec((HY_WIDTH, D)),
                  _const_spec((ATT_WIDTH, D)), _const_spec((D, D)), _const_spec((1, D)),
                  _const_spec((D, N_EXPERTS)), _const_spec((1, N_EXPERTS))],
        out_specs=[tok(D), tok(D), tok(TOP_K), tok(TOP_K)],
        out_shape=[jax.ShapeDtypeStruct((T, D), F32), jax.ShapeDtypeStruct((T, D), BF16),
                   jax.ShapeDtypeStruct((T, TOP_K), jnp.int32), jax.ShapeDtypeStruct((T, TOP_K), F32)],
        compiler_params=_cparams(("parallel",), 48),
        name="merge_router",
    )(x, ya, yb, sg, pa, pb, wo, g2, wr, br)


def _moe_kernel(be_ref, nu_ref, xs_ref, w1_ref, b1_ref, w2_ref, b2_ref, ys_ref):
    @pl.when(pl.program_id(0) < nu_ref[0])
    def _():
        u = jnp.dot(xs_ref[...], w1_ref[...], preferred_element_type=F32) + b1_ref[...]
        glu = jnp.minimum(u[:, :D_FF], SWIGLU_LIMIT)
        lin = jnp.clip(u[:, D_FF:], -SWIGLU_LIMIT, SWIGLU_LIMIT)
        a = glu * jax.nn.sigmoid(SWIGLU_ALPHA * glu) * (lin + 1.0)
        y = jnp.dot(a.astype(BF16), w2_ref[...], preferred_element_type=F32) + b2_ref[...]
        ys_ref[...] = y.astype(BF16)

    @pl.when(pl.program_id(0) >= nu_ref[0])
    def _():
        ys_ref[...] = jnp.zeros_like(ys_ref)


def _moe_experts(block_e, n_used, xs, w1, b1, w2, b2, *, tm):
    P, D = xs.shape
    grid_spec = pltpu.PrefetchScalarGridSpec(
        num_scalar_prefetch=2,
        grid=(P // tm,),
        in_specs=[pl.BlockSpec((tm, D), lambda i, be, nu: (i, 0)),
                  pl.BlockSpec((None, D, 2 * D_FF), lambda i, be, nu: (be[i], 0, 0)),
                  pl.BlockSpec((None, 1, 2 * D_FF), lambda i, be, nu: (be[i], 0, 0)),
                  pl.BlockSpec((None, D_FF, D), lambda i, be, nu: (be[i], 0, 0)),
                  pl.BlockSpec((None, 1, D), lambda i, be, nu: (be[i], 0, 0))],
        out_specs=pl.BlockSpec((tm, D), lambda i, be, nu: (i, 0)),
    )
    return pl.pallas_call(
        _moe_kernel,
        grid_spec=grid_spec,
        out_shape=jax.ShapeDtypeStruct((P, D), BF16),
        compiler_params=_cparams(("arbitrary",), 48),
        name="moe_experts",
    )(block_e, n_used, xs, w1, b1, w2, b2)


def _combine_kernel(x1_ref, yg_ref, tg_ref, o_ref):
    acc = x1_ref[...]
    g = tg_ref[...]
    for kk in range(TOP_K):
        acc = acc + g[:, kk:kk + 1] * yg_ref[kk].astype(F32)
    o_ref[...] = acc


def _combine(x1, yg, tg, *, tm):
    T, D = x1.shape
    return pl.pallas_call(
        _combine_kernel,
        grid=(T // tm,),
        in_specs=[pl.BlockSpec((tm, D), lambda i: (i, 0)), pl.BlockSpec((TOP_K, tm, D), lambda i: (0, i, 0)),
                  pl.BlockSpec((tm, TOP_K), lambda i: (i, 0))],
        out_specs=pl.BlockSpec((tm, D), lambda i: (i, 0)),
        out_shape=jax.ShapeDtypeStruct((T, D), F32),
        compiler_params=_cparams(("parallel",), 48),
        name="moe_combine",
    )(x1, yg, tg)


def _route(top_i, *, tm):
    T = top_i.shape[0]
    A = T * TOP_K
    flat_e = top_i.reshape(A)
    order = jnp.argsort(flat_e)
    sorted_e = flat_e[order]
    counts = jnp.sum(flat_e[:, None] == jnp.arange(N_EXPERTS, dtype=jnp.int32)[None, :], axis=0, dtype=jnp.int32)
    padded = ((counts + tm - 1) // tm) * tm
    pad_end = jnp.cumsum(padded)
    pad_start = pad_end - padded
    start = jnp.cumsum(counts) - counts
    dest = pad_start[sorted_e] + (jnp.arange(A, dtype=jnp.int32) - start[sorted_e])
    n_blocks = A // tm + N_EXPERTS
    slot_tok = jnp.zeros((n_blocks * tm,), jnp.int32).at[dest].set((order // TOP_K).astype(jnp.int32))
    slot_of = jnp.zeros((A,), jnp.int32).at[order].set(dest)
    block_e = jnp.minimum(
        jnp.searchsorted(pad_end, jnp.arange(n_blocks, dtype=jnp.int32) * tm, side='right'), N_EXPERTS - 1
    ).astype(jnp.int32)
    n_used = (pad_end[-1] // tm).astype(jnp.int32).reshape(1)
    return slot_tok, slot_of.reshape(T, TOP_K), block_e, n_used


def _trunk(x, meta_proj, w, tabs, *, tm_prep, tm_tok, tm_moe, attn_blk):
    B, S, D = x.shape
    T = B * S
    hy_m, q_m, k_m, v_m, _ = meta_proj
    hy_ext, q, k, v, sg = _prep(x, w['g1'], w['w_in'], w['bd'], w['qg'], w['kg'], tm=tm_prep,
                                hy_pad_blocks=HY_PAD // tm_prep)
    front = jnp.concatenate([jnp.zeros((HY_PAD - N_META, COL_HY), BF16), hy_m[0]], axis=0)
    hy_ext = lax.dynamic_update_slice(hy_ext, jnp.broadcast_to(front[None], (B, HY_PAD, COL_HY)), (0, 0, 0))
    kt = _hyena_filter(w['fw1'], w['fb1'], w['ffr1'], w['fw2'], w['fb2'], w['ffr2'], w['fw3'], s_real=S)
    zt, a, bt = _hy_pre(hy_ext, w['scw'], w['scb'], w['skip'], s_real=S)
    ya = _hy_post(_hy_conv(kt, zt), a, bt)
    near, bmeta, far = tabs
    yb = _attention(q, k, v, k_m[0], v_m[0], near, bmeta, far, w['lam'], w['subln'], blk=attn_blk)
    x1, h2, top_i, top_g = _merge(x.reshape(T, D), ya.reshape(T, HY_WIDTH), yb.reshape(T, ATT_WIDTH),
                                  sg.reshape(T, COL_G), w['pa'], w['pb'], w['wo'], w['g2'], w['wr'], w['br'],
                                  tm=tm_tok)
    slot_tok, slot_of, block_e, n_used = _route(top_i, tm=tm_moe)
    xs = jnp.take(h2, slot_tok, axis=0)
    ys = _moe_experts(block_e, n_used, xs, w['w1'], w['b1'], w['w2'], w['b2'], tm=tm_moe)
    yg = jnp.take(ys, slot_of.T, axis=0)
    return _combine(x1, yg, top_g, tm=tm_tok).reshape(B, S, D)


def kernel(x_prompt, x_sample, meta_tokens, rel_bias, norm1_g, w_in, short_conv_w, short_conv_b, filt_w1, filt_b1,
           filt_freq1, filt_w2, filt_b2, filt_freq2, filt_w3, hy_skip, proj_a, q_norm_g, k_norm_g, lambda_q1,
           lambda_k1, lambda_q2, lambda_k2, subln_g, proj_b, w_out, norm2_g, w_router, b_router, w_mlp1, b_mlp1,
           w_mlp2, b_mlp2):
    l = 0
    n_hm = COL_Q // ATT_HEAD_DIM
    w = {
        'g1': norm1_g[l][None], 'w_in': w_in[l].astype(BF16),
        'bd': jnp.kron(jnp.eye(n_hm, dtype=F32), jnp.full((ATT_HEAD_DIM, ATT_HEAD_DIM), 1.0 / ATT_HEAD_DIM, F32)
                       ).astype(BF16),
        'qg': (jnp.tile(q_norm_g[l], n_hm) * ATT_HEAD_DIM ** -0.5)[None], 'kg': jnp.tile(k_norm_g[l], n_hm)[None],
        'fw1': filt_w1[l], 'fb1': filt_b1[l], 'ffr1': filt_freq1[l], 'fw2': filt_w2[l], 'fb2': filt_b2[l],
        'ffr2': filt_freq2[l], 'fw3': filt_w3[l],
        'scw': short_conv_w[l], 'scb': short_conv_b[l], 'skip': hy_skip[l],
        'lam': jnp.stack([lambda_q1[l], lambda_k1[l], lambda_q2[l], lambda_k2[l]]), 'subln': subln_g[l][None],
        'pa': proj_a[l].astype(BF16), 'pb': proj_b[l].astype(BF16), 'wo': w_out[l].astype(BF16),
        'g2': norm2_g[l][None], 'wr': w_router[l], 'br': b_router[l][None],
        'w1': w_mlp1[l].astype(BF16), 'b1': b_mlp1[l][:, None, :], 'w2': w_mlp2[l].astype(BF16),
        'b2': b_mlp2[l][:, None, :],
    }
    attn_blk = 512
    meta_proj = _prep(meta_tokens[None], w['g1'], w['w_in'], w['bd'], w['qg'], w['kg'], tm=N_META, hy_pad_blocks=0)
    tabs = _attn_bias_tables(rel_bias, attn_blk)
    cfg = dict(tm_prep=512, tm_tok=512, tm_moe=512, attn_blk=attn_blk)
    y_prompt = _trunk(x_prompt, meta_proj, w, tabs, **cfg)
    y_sample = _trunk(x_sample, meta_proj, w, tabs, **cfg)
    return (y_prompt, y_sample)
```

```python
import functools
import math

import jax
import jax.numpy as jnp
import numpy as np
from jax import lax
from jax.experimental import pallas as pl
from jax.experimental.pallas import tpu as pltpu

F32 = jnp.float32
BF16 = jnp.bfloat16

D_MODEL = 1024
N_META = 16
RMS_EPS = 1e-6
HY_WIDTH = 512
HY_SHORT = 3
FILT_BANDS = 16
FILT_EMB = 1 + 2 * FILT_BANDS
FILT_HIDDEN = 64
DECAY_TARGET = 1e-2
FAST_DECAY_PCT = 0.3
SLOW_DECAY_PCT = 1.5
ATT_HEADS = 4
ATT_HEAD_DIM = 64
ATT_V_DIM = 128
ATT_WIDTH = 512
N_BUCKETS = 32
MAX_DISTANCE = 128
N_EXPERTS = 32
TOP_K = 4
D_FF = 1024
SWIGLU_LIMIT = 7.0
SWIGLU_ALPHA = 1.702
COL_HY = 3 * HY_WIDTH
COL_Q = 512
COL_K = 512
COL_V = 512
COL_G = 2 * D_MODEL
IN_COLS = COL_HY + COL_Q + COL_K + COL_V + COL_G
LAM_INIT = 0.8 - 0.6 * math.exp(-0.3 * 0)
LOG2E = 1.4426950408889634

V7X_VMEM_BYTES = 64 * 1024 * 1024
LANES = 128
CONV_BLK = 256
HY_PAD = 512
TILES = dict(tm_prep=512, tm_tok=512, tm_moe=512, attn_blk=512)


def _cparams(sem, vmem_mb):
    return pltpu.CompilerParams(dimension_semantics=sem, vmem_limit_bytes=vmem_mb * 1024 * 1024)


def _const_spec(shape):
    nd = len(shape)
    return pl.BlockSpec(shape, lambda *_: (0,) * nd, pipeline_mode=pl.Buffered(1))


def _prep_kernel(x_ref, g1_ref, w_ref, wqt_ref, wvt_ref, bd_ref, qg_ref, kg_ref,
                 hy_ref, qt_ref, k_ref, vt_ref, sg_ref):
    x = x_ref[...]
    ms = jnp.mean(x * x, axis=-1, keepdims=True)
    h = (x * lax.rsqrt(ms + RMS_EPS) * g1_ref[...]).astype(BF16)
    nt_dims = (((1,), (1,)), ((), ()))

    def proj(lo, hi):
        return jnp.dot(h, w_ref[:, lo:hi], preferred_element_type=F32)

    def proj_t(wt_ref):
        return lax.dot_general(wt_ref[...], h, nt_dims, preferred_element_type=F32)

    c0, c1, c2, c3 = COL_HY, COL_HY + COL_Q, COL_HY + COL_Q + COL_K, COL_HY + COL_Q + COL_K + COL_V
    hy_ref[...] = proj(0, c0).astype(BF16)
    qt = proj_t(wqt_ref)
    msq = jnp.dot(bd_ref[...], (qt * qt).astype(BF16), preferred_element_type=F32)
    qt_ref[...] = (qt * lax.rsqrt(msq + RMS_EPS) * qg_ref[...]).astype(BF16)
    kk = proj(c1, c2)
    msk = jnp.dot((kk * kk).astype(BF16), bd_ref[...], preferred_element_type=F32)
    k_ref[...] = (kk * lax.rsqrt(msk + RMS_EPS) * kg_ref[...]).astype(BF16)
    vt_ref[...] = proj_t(wvt_ref).astype(BF16)
    sg_ref[...] = jax.nn.sigmoid(proj(c3, IN_COLS)).astype(BF16)


def _prep(x, g1, w_in_bf, wqt, wvt, bd, qg_col, kg, *, tm, hy_pad_blocks):
    B, S, D = x.shape
    nt = S // tm
    tok = lambda w: pl.BlockSpec((None, tm, w), lambda b, i: (b, i, 0))
    tok_t = lambda w: pl.BlockSpec((None, w, tm), lambda b, i: (b, 0, i))
    return pl.pallas_call(
        _prep_kernel,
        grid=(B, nt),
        in_specs=[tok(D), _const_spec((1, D)), _const_spec((D, IN_COLS)), _const_spec((COL_Q, D)),
                  _const_spec((COL_V, D)), _const_spec((COL_Q, COL_Q)), _const_spec((COL_Q, 1)),
                  _const_spec((1, COL_K))],
        out_specs=[pl.BlockSpec((None, tm, COL_HY), lambda b, i: (b, i + hy_pad_blocks, 0)),
                   tok_t(COL_Q), tok(COL_K), tok_t(COL_V), tok(COL_G)],
        out_shape=[jax.ShapeDtypeStruct((B, S + hy_pad_blocks * tm, COL_HY), BF16),
                   jax.ShapeDtypeStruct((B, COL_Q, S), BF16), jax.ShapeDtypeStruct((B, S, COL_K), BF16),
                   jax.ShapeDtypeStruct((B, COL_V, S), BF16), jax.ShapeDtypeStruct((B, S, COL_G), BF16)],
        compiler_params=_cparams(("parallel", "parallel"), 48),
        name="prep",
    )(x, g1, w_in_bf, wqt, wvt, bd, qg_col, kg)


def _filt_kernel(w1t_ref, w1c_ref, w1s_ref, b1_ref, fr1_ref, w2t_ref, b2_ref, fr2_ref, w3f_ref, w3b_ref,
                 bands_ref, deltas_ref, kt_ref, hid_ref, *, n_seq, s_real):
    lk = kt_ref.shape[1]
    hi = lax.Precision.HIGHEST
    u = lax.broadcasted_iota(jnp.int32, (1, lk), 1)
    d = u - s_real
    pos = jnp.abs(d).astype(F32)
    t = pos / float(max(n_seq - 1, 1))

    @pl.when(pl.program_id(0) == 0)
    def _():
        w = 2.0 * math.pi * pos / float(n_seq)
        ang = bands_ref[...] * w
        pre = (w1t_ref[...] * t
               + jnp.dot(w1c_ref[...], jnp.cos(ang), precision=hi, preferred_element_type=F32)
               - jnp.dot(w1s_ref[...], jnp.sin(ang), precision=hi, preferred_element_type=F32)
               + b1_ref[...])
        h1 = jnp.sin(fr1_ref[...] * pre)
        h2 = jnp.sin(fr2_ref[...] * (jnp.dot(w2t_ref[...], h1, precision=hi, preferred_element_type=F32)
                                     + b2_ref[...]))
        hid_ref[...] = h2

    h2 = hid_ref[...]
    hf = jnp.dot(w3f_ref[...], h2, precision=hi, preferred_element_type=F32)
    hb = jnp.dot(w3b_ref[...], h2, precision=hi, preferred_element_type=F32)
    decay = jnp.exp(-t * deltas_ref[...])
    valid = pos <= float(n_seq - 1)
    kun = jnp.where(valid, jnp.where(d >= 0, hf, hb) * decay, 0.0)
    nrm = lax.rsqrt(jnp.sum(kun * kun, axis=1, keepdims=True) + RMS_EPS)
    kt_ref[...] = kun * nrm


def _hyena_filter(fw1, fb1, ffr1, fw2, fb2, ffr2, fw3, *, s_real):
    n_seq = s_real + N_META
    lk = 2 * s_real + CONV_BLK
    cf = 64
    col = lambda a: a.reshape(-1, 1)
    w1t = fw1.T
    bands = jnp.linspace(1e-4, FILT_BANDS - 1, FILT_BANDS, dtype=F32).reshape(-1, 1)
    max_decay = math.log(DECAY_TARGET) / FAST_DECAY_PCT
    min_decay = math.log(DECAY_TARGET) / SLOW_DECAY_PCT
    deltas = jnp.abs(jnp.linspace(min_decay, max_decay, HY_WIDTH, dtype=F32)).reshape(-1, 1)
    w3t = fw3.T
    h = FILT_HIDDEN
    return pl.pallas_call(
        functools.partial(_filt_kernel, n_seq=n_seq, s_real=s_real),
        grid=(HY_WIDTH // cf,),
        in_specs=[_const_spec((h, 1)), _const_spec((h, FILT_BANDS)), _const_spec((h, FILT_BANDS)),
                  _const_spec((h, 1)), _const_spec((h, 1)), _const_spec((h, h)), _const_spec((h, 1)),
                  _const_spec((h, 1)),
                  pl.BlockSpec((cf, h), lambda i: (i, 0)), pl.BlockSpec((cf, h), lambda i: (i, 0)),
                  _const_spec((FILT_BANDS, 1)), pl.BlockSpec((cf, 1), lambda i: (i, 0))],
        out_specs=pl.BlockSpec((cf, lk), lambda i: (i, 0)),
        out_shape=jax.ShapeDtypeStruct((HY_WIDTH, lk), F32),
        scratch_shapes=[pltpu.VMEM((h, lk), F32)],
        compiler_params=_cparams(("arbitrary",), 48),
        name="hyena_filter",
    )(w1t[:, 0:1], w1t[:, 1:1 + FILT_BANDS], w1t[:, 1 + FILT_BANDS:], col(fb1), col(ffr1), fw2.T, col(fb2),
      col(ffr2), w3t[:HY_WIDTH], w3t[HY_WIDTH:], bands, deltas)


def _hy_pre_kernel(x0_ref, x1_ref, v_ref, p0_ref, p1_ref, pv_ref, n0_ref, n1_ref, nv_ref,
                   scw_ref, scb_ref, skip_ref, zt_ref, a_ref, bt_ref, zs_ref):
    jt = pl.program_id(1)
    last = pl.num_programs(1) - 1
    nb_batch = x0_ref.shape[0]
    row = lax.broadcasted_iota(jnp.int32, (CONV_BLK, LANES), 0)
    keep_next = (jt < last).astype(F32)
    z_keep = jnp.logical_or(jt > 0, row >= CONV_BLK - N_META)

    def short_conv(cur_ref, prev_ref, next_ref, comp, b):
        cur = cur_ref[b].astype(F32)
        prev = prev_ref[b, 15:16, :].astype(F32)
        nxt = next_ref[b, 0:1, :].astype(F32) * keep_next
        up = jnp.where(row == 0, prev, pltpu.roll(cur, 1, 0))
        dn = jnp.where(row == CONV_BLK - 1, nxt, pltpu.roll(cur, CONV_BLK - 1, 0))
        w = scw_ref[comp]
        return up * w[0:1] + cur * w[1:2] + dn * w[2:3] + scb_ref[comp]

    def body(b, carry):
        x0c = short_conv(x0_ref, p0_ref, n0_ref, 0, b)
        x1c = short_conv(x1_ref, p1_ref, n1_ref, 1, b)
        vc = short_conv(v_ref, pv_ref, nv_ref, 2, b)
        z = jnp.where(z_keep, x1c * vc, 0.0)
        zs_ref[b] = z.T
        a_ref[b] = x0c.astype(BF16)
        bt_ref[b] = (x0c * z * skip_ref[...]).astype(BF16)
        return carry

    lax.fori_loop(0, nb_batch, body, 0)
    zt_ref[...] = pltpu.einshape("bcl->cbl", zs_ref[...]).astype(BF16)


def _hy_pre(hy_ext, scw, scb, skip, *, s_real):
    B = hy_ext.shape[0]
    nb = s_real // CONV_BLK
    ncb = HY_WIDTH // LANES
    sub = 16
    r16 = CONV_BLK // sub
    last16 = (s_real + HY_PAD) // sub - 1
    cur = lambda comp: pl.BlockSpec((B, CONV_BLK, LANES), lambda cb, jt: (0, jt + 1, comp * ncb + cb))
    prv = lambda comp: pl.BlockSpec((B, sub, LANES), lambda cb, jt: (0, r16 * (jt + 1) - 1, comp * ncb + cb))
    nxt = lambda comp: pl.BlockSpec(
        (B, sub, LANES), lambda cb, jt: (0, jnp.minimum(r16 * (jt + 2), last16), comp * ncb + cb))
    nat = pl.BlockSpec((B, CONV_BLK, LANES), lambda cb, jt: (0, jnp.maximum(jt - 1, 0), cb))
    scw_r = scw.reshape(HY_SHORT, 3, ncb, LANES).transpose(2, 1, 0, 3)
    scb_r = scb.reshape(3, ncb, 1, LANES).transpose(1, 0, 2, 3)
    skip_r = skip.reshape(ncb, 1, LANES)
    return pl.pallas_call(
        _hy_pre_kernel,
        grid=(ncb, nb + 1),
        in_specs=[cur(0), cur(1), cur(2), prv(0), prv(1), prv(2), nxt(0), nxt(1), nxt(2),
                  pl.BlockSpec((None, 3, HY_SHORT, LANES), lambda cb, jt: (cb, 0, 0, 0)),
                  pl.BlockSpec((None, 3, 1, LANES), lambda cb, jt: (cb, 0, 0, 0)),
                  pl.BlockSpec((None, 1, LANES), lambda cb, jt: (cb, 0, 0))],
        out_specs=[pl.BlockSpec((LANES, None, B, CONV_BLK), lambda cb, jt: (cb, jt, 0, 0)), nat, nat],
        out_shape=[jax.ShapeDtypeStruct((HY_WIDTH, nb + 1, B, CONV_BLK), BF16),
                   jax.ShapeDtypeStruct((B, s_real, HY_WIDTH), BF16),
                   jax.ShapeDtypeStruct((B, s_real, HY_WIDTH), BF16)],
        scratch_shapes=[pltpu.VMEM((B, LANES, CONV_BLK), F32)],
        compiler_params=_cparams(("parallel", "arbitrary"), 48),
        name="hyena_pre",
    )(hy_ext, hy_ext, hy_ext, hy_ext, hy_ext, hy_ext, hy_ext, hy_ext, hy_ext, scw_r, scb_r, skip_r)


def _hy_conv_kernel(kt_ref, z_ref, y_ref, g_ref, *, nb):
    cbk = z_ref.shape[0]
    B = z_ref.shape[2]
    s_real = nb * CONV_BLK
    nq = 2 * s_real // LANES + 1

    def channel(c, carry):
        def build(q, carry2):
            off = pl.multiple_of(2 * s_real - LANES * q, LANES)
            w = kt_ref[pl.ds(c, 1), pl.ds(off, 2 * LANES)]
            rolled = pltpu.roll(jnp.broadcast_to(w, (LANES, 2 * LANES)), 0, 1, stride=1, stride_axis=0)
            g_ref[pl.ds(pl.multiple_of(LANES * q, LANES), LANES), :] = rolled[:, LANES:].astype(BF16)
            return carry2

        lax.fori_loop(0, nq, build, 0)
        y_ref[c] = jnp.zeros(y_ref.shape[1:], F32)
        for delta in range(-(nb - 1), nb + 1):
            base = CONV_BLK * (nb - delta)
            tile = jnp.concatenate([g_ref[base + LANES:base + LANES + CONV_BLK, :],
                                    g_ref[base:base + CONV_BLK, :]], axis=1)
            jb_lo = max(0, 1 - delta)
            n = min(nb, nb - delta) - jb_lo + 1
            o_lo = jb_lo + delta - 1
            lhs = z_ref[c, jb_lo:jb_lo + n].reshape(n * B, CONV_BLK)
            y_ref[c, o_lo:o_lo + n] += jnp.dot(lhs, tile, preferred_element_type=F32).reshape(n, B, CONV_BLK)
        return carry

    lax.fori_loop(0, cbk, channel, 0)


def _hy_conv(kt, zt, *, cbk=8):
    C, nb1, B, _ = zt.shape
    nb = nb1 - 1
    lk = kt.shape[1]
    return pl.pallas_call(
        functools.partial(_hy_conv_kernel, nb=nb),
        grid=(C // cbk,),
        in_specs=[pl.BlockSpec((cbk, lk), lambda i: (i, 0)),
                  pl.BlockSpec((cbk, nb1, B, CONV_BLK), lambda i: (i, 0, 0, 0))],
        out_specs=pl.BlockSpec((cbk, nb, B, CONV_BLK), lambda i: (i, 0, 0, 0)),
        out_shape=jax.ShapeDtypeStruct((C, nb, B, CONV_BLK), F32),
        scratch_shapes=[pltpu.VMEM((2 * nb * CONV_BLK + LANES, LANES), BF16)],
        compiler_params=_cparams(("parallel",), 48),
        name="hyena_conv",
    )(kt, zt)


def _hy_post_kernel(yt_ref, a_ref, bt_ref, o_ref, ys_ref):
    ys_ref[...] = pltpu.einshape("cbl->bcl", yt_ref[...])

    def body(b, carry):
        y = ys_ref[b].T
        o_ref[b] = (a_ref[b].astype(F32) * y + bt_ref[b].astype(F32)).astype(BF16)
        return carry

    lax.fori_loop(0, a_ref.shape[0], body, 0)


def _hy_post(yt, a, bt):
    C, nb, B, _ = yt.shape
    nat = pl.BlockSpec((B, CONV_BLK, LANES), lambda cb, ib: (0, ib, cb))
    return pl.pallas_call(
        _hy_post_kernel,
        grid=(C // LANES, nb),
        in_specs=[pl.BlockSpec((LANES, None, B, CONV_BLK), lambda cb, ib: (cb, ib, 0, 0)), nat, nat],
        out_specs=nat,
        out_shape=jax.ShapeDtypeStruct(a.shape, BF16),
        scratch_shapes=[pltpu.VMEM((B, LANES, CONV_BLK), F32)],
        compiler_params=_cparams(("parallel", "parallel"), 48),
        name="hyena_post",
    )(yt, a, bt)


def _attn_kernel(far_ref, qt_ref, k_ref, vt_ref, km_ref, vtm_ref, near_ref, bmeta_ref, lam_ref, sg_ref,
                 o_ref, m_ref, l_ref, acc_ref, *, blk):
    h = pl.program_id(1)
    i = pl.program_id(2)
    nk = k_ref.shape[0] // blk
    qt = qt_ref[...]
    rowi = lax.broadcasted_iota(jnp.int32, qt.shape, 0)
    zero = jnp.zeros_like(qt)
    qmaps = (jnp.where(rowi < ATT_HEAD_DIM, qt, zero), jnp.where(rowi >= ATT_HEAD_DIM, qt, zero))

    def step(kblk, vtblk, bias, first=False):
        for c in range(2):
            s = jnp.dot(kblk, qmaps[c], preferred_element_type=F32) + bias
            smax = jnp.max(s, axis=0, keepdims=True)
            if first:
                m_new = smax
                p = jnp.exp2(s - m_new)
                l_ref[c] = jnp.sum(p, axis=0, keepdims=True)
                acc_ref[c] = jnp.dot(vtblk, p.astype(BF16), preferred_element_type=F32)
            else:
                m_old = m_ref[c]
                m_new = jnp.maximum(m_old, smax)
                alpha = jnp.exp2(m_old - m_new)
                p = jnp.exp2(s - m_new)
                l_ref[c] = alpha * l_ref[c] + jnp.sum(p, axis=0, keepdims=True)
                acc_ref[c] = alpha * acc_ref[c] + jnp.dot(vtblk, p.astype(BF16), preferred_element_type=F32)
            m_ref[c] = m_new

    def kv(j):
        r0 = pl.multiple_of(j * blk, blk)
        return k_ref[pl.ds(r0, blk), :], vt_ref[:, pl.ds(r0, blk)]

    step(km_ref[...], vtm_ref[...], bmeta_ref[jnp.minimum(i, 1)], first=True)

    def far_loop(lo, hi, c_far):
        def body(j, carry):
            kb, vb = kv(j)
            step(kb, vb, c_far)
            return carry
        lax.fori_loop(lo, hi, body, 0)

    far_loop(0, jnp.maximum(i - 1, 0), far_ref[2 * h])
    for dj in (-1, 0, 1):
        @pl.when(jnp.logical_and(i + dj >= 0, i + dj < nk))
        def _():
            kb, vb = kv(i + dj)
            step(kb, vb, near_ref[dj + 1])
    far_loop(jnp.minimum(i + 2, nk), nk, far_ref[2 * h + 1])

    lq = lam_ref[...]
    lam = (jnp.exp(jnp.sum(lq[0:1] * lq[1:2], axis=1, keepdims=True))
           - jnp.exp(jnp.sum(lq[2:3] * lq[3:4], axis=1, keepdims=True)) + LAM_INIT)
    ot = acc_ref[0] / l_ref[0] - lam * (acc_ref[1] / l_ref[1])
    ot = ot * lax.rsqrt(jnp.mean(ot * ot, axis=0, keepdims=True) + RMS_EPS) * sg_ref[...] * (1.0 - LAM_INIT)
    o_ref[...] = ot.T.astype(BF16)


def _t5_bucket(rel):
    nb = N_BUCKETS // 2
    max_exact = nb // 2
    ret = jnp.where(rel > 0, nb, 0)
    n = jnp.abs(rel)
    nf = jnp.maximum(n, 1).astype(F32)
    large = max_exact + (jnp.log(nf / max_exact) / math.log(MAX_DISTANCE / max_exact)
                         * (nb - max_exact)).astype(jnp.int32)
    large = jnp.minimum(large, nb - 1)
    return ret + jnp.where(n < max_exact, n, large)


def _attn_bias_tables(rel_bias, blk):
    table = rel_bias.astype(F32) * LOG2E
    r = jnp.arange(blk, dtype=jnp.int32)
    rel = jnp.stack([dj * blk + r[:, None] - r[None, :] for dj in (-1, 0, 1)])
    near = jnp.transpose(table[_t5_bucket(rel)], (3, 0, 1, 2))
    m = jnp.arange(N_META, dtype=jnp.int32)
    rel_m0 = m[:, None] - (N_META + r[None, :])
    rel_m1 = rel_m0 - blk
    bmeta = jnp.transpose(table[_t5_bucket(jnp.stack([rel_m0, rel_m1]))], (3, 0, 1, 2))
    far = table[_t5_bucket(jnp.array([-(blk + 1), blk + 1], jnp.int32))]
    return near, bmeta, far.T.reshape(-1)


def _attention(qt, k, vt, kmeta, vtmeta, near, bmeta, far, lam, subln_col, *, blk):
    B, S, _ = k.shape
    grid_spec = pltpu.PrefetchScalarGridSpec(
        num_scalar_prefetch=1,
        grid=(B, ATT_HEADS, S // blk),
        in_specs=[pl.BlockSpec((None, ATT_V_DIM, blk), lambda b, h, i, *_: (b, h, i)),
                  pl.BlockSpec((None, S, ATT_V_DIM), lambda b, h, i, *_: (b, 0, h)),
                  pl.BlockSpec((None, ATT_V_DIM, S), lambda b, h, i, *_: (b, h, 0)),
                  pl.BlockSpec((N_META, ATT_V_DIM), lambda b, h, i, *_: (0, h)),
                  pl.BlockSpec((ATT_V_DIM, N_META), lambda b, h, i, *_: (h, 0)),
                  pl.BlockSpec((None, 3, blk, blk), lambda b, h, i, *_: (h, 0, 0, 0)),
                  pl.BlockSpec((None, 2, N_META, blk), lambda b, h, i, *_: (h, 0, 0, 0)),
                  pl.BlockSpec((4, ATT_HEAD_DIM), lambda b, h, i, *_: (0, 0)),
                  pl.BlockSpec((ATT_V_DIM, 1), lambda b, h, i, *_: (0, 0))],
        out_specs=pl.BlockSpec((None, blk, ATT_V_DIM), lambda b, h, i, *_: (b, i, h)),
        scratch_shapes=[pltpu.VMEM((2, 1, blk), F32), pltpu.VMEM((2, 1, blk), F32),
                        pltpu.VMEM((2, ATT_V_DIM, blk), F32)],
    )
    return pl.pallas_call(
        functools.partial(_attn_kernel, blk=blk),
        grid_spec=grid_spec,
        out_shape=jax.ShapeDtypeStruct((B, S, ATT_WIDTH), BF16),
        compiler_params=_cparams(("parallel", "parallel", "arbitrary"), 48),
        name="diff_attention",
    )(far, qt, k, vt, kmeta, vtmeta, near, bmeta, lam, subln_col)


def _merge_kernel(x_ref, ya_ref, yb_ref, sg_ref, pa_ref, pb_ref, wo_ref, g2_ref, wrh_ref, wrl_ref, br_ref,
                  tri_ref, x1_ref, h2_ref, ti_ref, tg_ref, rk_ref, cnt_ref):
    ma = jnp.dot(ya_ref[...], pa_ref[...], preferred_element_type=F32)
    mb = jnp.dot(yb_ref[...], pb_ref[...], preferred_element_type=F32)
    m = sg_ref[:, :D_MODEL].astype(F32) * ma + sg_ref[:, D_MODEL:].astype(F32) * mb
    x1 = x_ref[...] + jnp.dot(m.astype(BF16), wo_ref[...], preferred_element_type=F32)
    x1_ref[...] = x1
    h2 = x1 * lax.rsqrt(jnp.mean(x1 * x1, axis=-1, keepdims=True) + RMS_EPS) * g2_ref[...]
    h2_hi = h2.astype(BF16)
    h2_ref[...] = h2_hi
    h2_lo = (h2 - h2_hi.astype(F32)).astype(BF16)
    wrh = wrh_ref[...]
    logits = (jnp.dot(h2_hi, wrh, preferred_element_type=F32) + jnp.dot(h2_lo, wrh, preferred_element_type=F32)
              + jnp.dot(h2_hi, wrl_ref[...], preferred_element_type=F32) + br_ref[...])
    lane = lax.broadcasted_iota(jnp.int32, logits.shape, 1)
    slot = lax.broadcasted_iota(jnp.int32, ti_ref.shape, 1)
    top_i = jnp.zeros(ti_ref.shape, jnp.int32)
    top_v = jnp.zeros(tg_ref.shape, F32)
    work = logits
    picks = []
    for kk in range(TOP_K):
        mx = jnp.max(work, axis=1, keepdims=True)
        idx = jnp.min(jnp.where(work == mx, lane, N_EXPERTS), axis=1, keepdims=True)
        pick = lane == idx
        picks.append(pick)
        top_i = jnp.where(slot == kk, idx, top_i)
        top_v = jnp.where(slot == kk, mx, top_v)
        work = jnp.where(pick, -jnp.inf, work)
    ex = jnp.exp(top_v - jnp.max(top_v, axis=1, keepdims=True))
    ti_ref[...] = top_i
    tg_ref[...] = ex / jnp.sum(ex, axis=1, keepdims=True)

    @pl.when(pl.program_id(0) == 0)
    def _():
        cnt_ref[...] = jnp.zeros_like(cnt_ref)

    chosen = jnp.where(work == -jnp.inf, 1.0, 0.0)
    before = cnt_ref[...] + jnp.dot(tri_ref[...], chosen.astype(BF16), preferred_element_type=F32)
    rank = jnp.zeros(rk_ref.shape, F32)
    for kk in range(TOP_K):
        rk = jnp.sum(jnp.where(picks[kk], before, 0.0), axis=1, keepdims=True)
        rank = jnp.where(slot == kk, rk, rank)
    rk_ref[...] = rank.astype(jnp.int32)
    cnt_ref[...] += jnp.sum(chosen, axis=0, keepdims=True)


def _merge(x, ya, yb, sg, pa, pb, wo, g2, wr_hi, wr_lo, br, *, tm):
    T, D = x.shape
    tok = lambda w: pl.BlockSpec((tm, w), lambda i: (i, 0))
    tri = jnp.tril(jnp.ones((tm, tm), F32), -1).astype(BF16)
    return pl.pallas_call(
        _merge_kernel,
        grid=(T // tm,),
        in_specs=[tok(D), tok(HY_WIDTH), tok(ATT_WIDTH), tok(COL_G), _const_spec((HY_WIDTH, D)),
                  _const_spec((ATT_WIDTH, D)), _const_spec((D, D)), _const_spec((1, D)),
                  _const_spec((D, N_EXPERTS)), _const_spec((D, N_EXPERTS)), _const_spec((1, N_EXPERTS)),
                  _const_spec((tm, tm))],
        out_specs=[tok(D), tok(D), tok(TOP_K), tok(TOP_K), tok(TOP_K),
                   pl.BlockSpec((1, N_EXPERTS), lambda i: (0, 0))],
        out_shape=[jax.ShapeDtypeStruct((T, D), F32), jax.ShapeDtypeStruct((T, D), BF16),
                   jax.ShapeDtypeStruct((T, TOP_K), jnp.int32), jax.ShapeDtypeStruct((T, TOP_K), F32),
                   jax.ShapeDtypeStruct((T, TOP_K), jnp.int32), jax.ShapeDtypeStruct((1, N_EXPERTS), F32)],
        compiler_params=_cparams(("arbitrary",), 48),
        name="merge_router",
    )(x, ya, yb, sg, pa, pb, wo, g2, wr_hi, wr_lo, br, tri)


def _moe_kernel(be_ref, nu_ref, xs_ref, w1_ref, b1_ref, w2_ref, b2_ref, ys_ref):
    @pl.when(pl.program_id(0) < nu_ref[0])
    def _():
        u = jnp.dot(xs_ref[...], w1_ref[...], preferred_element_type=F32) + b1_ref[...]
        glu = jnp.minimum(u[:, :D_FF], SWIGLU_LIMIT)
        lin = jnp.clip(u[:, D_FF:], -SWIGLU_LIMIT, SWIGLU_LIMIT)
        a = glu * jax.nn.sigmoid(SWIGLU_ALPHA * glu) * (lin + 1.0)
        y = jnp.dot(a.astype(BF16), w2_ref[...], preferred_element_type=F32) + b2_ref[...]
        ys_ref[...] = y.astype(BF16)

    @pl.when(pl.program_id(0) >= nu_ref[0])
    def _():
        ys_ref[...] = jnp.zeros_like(ys_ref)


def _moe_experts(block_e, n_used, xs, w1, b1, w2, b2, *, tm):
    P, D = xs.shape
    grid_spec = pltpu.PrefetchScalarGridSpec(
        num_scalar_prefetch=2,
        grid=(P // tm,),
        in_specs=[pl.BlockSpec((tm, D), lambda i, be, nu: (i, 0)),
                  pl.BlockSpec((None, D, 2 * D_FF), lambda i, be, nu: (be[i], 0, 0)),
                  pl.BlockSpec((None, 1, 2 * D_FF), lambda i, be, nu: (be[i], 0, 0)),
                  pl.BlockSpec((None, D_FF, D), lambda i, be, nu: (be[i], 0, 0)),
                  pl.BlockSpec((None, 1, D), lambda i, be, nu: (be[i], 0, 0))],
        out_specs=pl.BlockSpec((tm, D), lambda i, be, nu: (i, 0)),
    )
    return pl.pallas_call(
        _moe_kernel,
        grid_spec=grid_spec,
        out_shape=jax.ShapeDtypeStruct((P, D), BF16),
        compiler_params=_cparams(("arbitrary",), 48),
        name="moe_experts",
    )(block_e, n_used, xs, w1, b1, w2, b2)


def _combine_kernel(x1_ref, yg_ref, tg_ref, o_ref):
    acc = x1_ref[...]
    g = tg_ref[...]
    for kk in range(TOP_K):
        acc = acc + g[:, kk:kk + 1] * yg_ref[kk].astype(F32)
    o_ref[...] = acc


def _combine(x1, yg, tg, *, tm):
    T, D = x1.shape
    return pl.pallas_call(
        _combine_kernel,
        grid=(T // tm,),
        in_specs=[pl.BlockSpec((tm, D), lambda i: (i, 0)), pl.BlockSpec((TOP_K, tm, D), lambda i: (0, i, 0)),
                  pl.BlockSpec((tm, TOP_K), lambda i: (i, 0))],
        out_specs=pl.BlockSpec((tm, D), lambda i: (i, 0)),
        out_shape=jax.ShapeDtypeStruct((T, D), F32),
        compiler_params=_cparams(("parallel",), 48),
        name="moe_combine",
    )(x1, yg, tg)


def _route(top_i, rank, counts, *, tm):
    T = top_i.shape[0]
    A = T * TOP_K
    counts = counts.reshape(N_EXPERTS).astype(jnp.int32)
    padded = ((counts + tm - 1) // tm) * tm
    pad_end = jnp.cumsum(padded)
    pad_start = pad_end - padded
    onehot = top_i[:, :, None] == jnp.arange(N_EXPERTS, dtype=jnp.int32)[None, None, :]
    slot_of = jnp.sum(jnp.where(onehot, pad_start[None, None, :], 0), axis=-1) + rank
    n_blocks = A // tm + N_EXPERTS
    tok = jnp.broadcast_to(jnp.arange(T, dtype=jnp.int32)[:, None], (T, TOP_K))
    slot_tok = jnp.zeros((n_blocks * tm,), jnp.int32).at[slot_of.reshape(A)].set(tok.reshape(A))
    block_e = jnp.minimum(
        jnp.searchsorted(pad_end, jnp.arange(n_blocks, dtype=jnp.int32) * tm, side='right'), N_EXPERTS - 1
    ).astype(jnp.int32)
    n_used = (pad_end[-1] // tm).astype(jnp.int32).reshape(1)
    return slot_tok, slot_of, block_e, n_used


def _trunk(x, meta_proj, w, tabs, *, tm_prep, tm_tok, tm_moe, attn_blk):
    B, S, D = x.shape
    T = B * S
    hy_m, _, k_m, vt_m, _ = meta_proj
    hy_ext, qt, k, vt, sg = _prep(x, w['g1'], w['w_in'], w['wqt'], w['wvt'], w['bd'], w['qg'], w['kg'], tm=tm_prep,
                                  hy_pad_blocks=HY_PAD // tm_prep)
    front = jnp.concatenate([jnp.zeros((HY_PAD - N_META, COL_HY), BF16), hy_m[0]], axis=0)
    hy_ext = lax.dynamic_update_slice(hy_ext, jnp.broadcast_to(front[None], (B, HY_PAD, COL_HY)), (0, 0, 0))
    kt = _hyena_filter(w['fw1'], w['fb1'], w['ffr1'], w['fw2'], w['fb2'], w['ffr2'], w['fw3'], s_real=S)
    zt, a, bt = _hy_pre(hy_ext, w['scw'], w['scb'], w['skip'], s_real=S)
    ya = _hy_post(_hy_conv(kt, zt), a, bt)
    near, bmeta, far = tabs
    yb = _attention(qt, k, vt, k_m[0], vt_m[0], near, bmeta, far, w['lam'], w['subln'], blk=attn_blk)
    x1, h2, top_i, top_g, rank, counts = _merge(
        x.reshape(T, D), ya.reshape(T, HY_WIDTH), yb.reshape(T, ATT_WIDTH), sg.reshape(T, COL_G),
        w['pa'], w['pb'], w['wo'], w['g2'], w['wr_hi'], w['wr_lo'], w['br'], tm=tm_tok)
    slot_tok, slot_of, block_e, n_used = _route(top_i, rank, counts, tm=tm_moe)
    xs = jnp.take(h2, slot_tok, axis=0)
    ys = _moe_experts(block_e, n_used, xs, w['w1'], w['b1'], w['w2'], w['b2'], tm=tm_moe)
    yg = jnp.take(ys, slot_of.T, axis=0)
    return _combine(x1, yg, top_g, tm=tm_tok).reshape(B, S, D)


def kernel(x_prompt, x_sample, meta_tokens, rel_bias, norm1_g, w_in, short_conv_w, short_conv_b, filt_w1, filt_b1,
           filt_freq1, filt_w2, filt_b2, filt_freq2, filt_w3, hy_skip, proj_a, q_norm_g, k_norm_g, lambda_q1,
           lambda_k1, lambda_q2, lambda_k2, subln_g, proj_b, w_out, norm2_g, w_router, b_router, w_mlp1, b_mlp1,
           w_mlp2, b_mlp2):
    l = 0
    n_hm = COL_Q // ATT_HEAD_DIM
    w_in_bf = w_in[l].astype(BF16)
    c_q, c_v = COL_HY, COL_HY + COL_Q + COL_K
    wr_hi = w_router[l].astype(BF16)
    w = {
        'g1': norm1_g[l][None], 'w_in': w_in_bf,
        'wqt': w_in_bf[:, c_q:c_q + COL_Q].T, 'wvt': w_in_bf[:, c_v:c_v + COL_V].T,
        'bd': jnp.kron(jnp.eye(n_hm, dtype=F32), jnp.full((ATT_HEAD_DIM, ATT_HEAD_DIM), 1.0 / ATT_HEAD_DIM, F32)
                       ).astype(BF16),
        'qg': (jnp.tile(q_norm_g[l], n_hm) * (ATT_HEAD_DIM ** -0.5 * LOG2E))[:, None],
        'kg': jnp.tile(k_norm_g[l], n_hm)[None],
        'wr_hi': wr_hi, 'wr_lo': (w_router[l] - wr_hi.astype(F32)).astype(BF16),
        'fw1': filt_w1[l], 'fb1': filt_b1[l], 'ffr1': filt_freq1[l], 'fw2': filt_w2[l], 'fb2': filt_b2[l],
        'ffr2': filt_freq2[l], 'fw3': filt_w3[l],
        'scw': short_conv_w[l], 'scb': short_conv_b[l], 'skip': hy_skip[l],
        'lam': jnp.stack([lambda_q1[l], lambda_k1[l], lambda_q2[l], lambda_k2[l]]), 'subln': subln_g[l][:, None],
        'pa': proj_a[l].astype(BF16), 'pb': proj_b[l].astype(BF16), 'wo': w_out[l].astype(BF16),
        'g2': norm2_g[l][None], 'br': b_router[l][None],
        'w1': w_mlp1[l].astype(BF16), 'b1': b_mlp1[l][:, None, :], 'w2': w_mlp2[l].astype(BF16),
        'b2': b_mlp2[l][:, None, :],
    }
    meta_proj = _prep(meta_tokens[None], w['g1'], w['w_in'], w['wqt'], w['wvt'], w['bd'], w['qg'], w['kg'],
                      tm=N_META, hy_pad_blocks=0)
    tabs = _attn_bias_tables(rel_bias, TILES['attn_blk'])
    y_prompt = _trunk(x_prompt, meta_proj, w, tabs, **TILES)
    y_sample = _trunk(x_sample, meta_proj, w, tabs, **TILES)
    return (y_prompt, y_sample)
```

```python
import functools
import math

import jax
import jax.numpy as jnp
import numpy as np
from jax import lax
from jax.experimental import pallas as pl
from jax.experimental.pallas import tpu as pltpu
from jax.experimental.pallas import tpu_sc as plsc

F32 = jnp.float32
BF16 = jnp.bfloat16

D_MODEL = 1024
N_META = 16
RMS_EPS = 1e-6
HY_WIDTH = 512
HY_SHORT = 3
FILT_BANDS = 16
FILT_EMB = 1 + 2 * FILT_BANDS
FILT_HIDDEN = 64
DECAY_TARGET = 1e-2
FAST_DECAY_PCT = 0.3
SLOW_DECAY_PCT = 1.5
ATT_HEADS = 4
ATT_HEAD_DIM = 64
ATT_V_DIM = 128
ATT_WIDTH = 512
N_BUCKETS = 32
MAX_DISTANCE = 128
N_EXPERTS = 32
TOP_K = 4
D_FF = 1024
SWIGLU_LIMIT = 7.0
SWIGLU_ALPHA = 1.702
COL_HY = 3 * HY_WIDTH
COL_Q = 512
COL_K = 512
COL_V = 512
COL_G = 2 * D_MODEL
IN_COLS = COL_HY + COL_Q + COL_K + COL_V + COL_G
LAM_INIT = 0.8 - 0.6 * math.exp(-0.3 * 0)
LOG2E = 1.4426950408889634

V7X_VMEM_BYTES = 64 * 1024 * 1024
LANES = 128
CONV_BLK = 256
HY_PAD = 512
TILES = dict(tm_prep=512, tm_tok=512, tm_moe=512, attn_blk=512)


def _cparams(sem, vmem_mb):
    return pltpu.CompilerParams(dimension_semantics=sem, vmem_limit_bytes=vmem_mb * 1024 * 1024)


def _const_spec(shape):
    nd = len(shape)
    return pl.BlockSpec(shape, lambda *_: (0,) * nd, pipeline_mode=pl.Buffered(1))


def _pack_rows(x):
    n = x.shape[1] // 2
    lo = pltpu.bitcast(x[:, :n].astype(BF16).astype(F32), jnp.uint32)
    hi = pltpu.bitcast(x[:, n:].astype(BF16).astype(F32), jnp.uint32)
    return (hi & jnp.uint32(0xFFFF0000)) | (lo >> 16)


def _unpack_rows(w):
    lo = pltpu.bitcast(w << 16, F32)
    hi = pltpu.bitcast(w & jnp.uint32(0xFFFF0000), F32)
    return lo, hi


def _sc_gather_rows(table, idx, *, chunk=128):
    info = plsc.get_sparse_core_info()
    n_workers = info.num_cores * info.num_subcores
    n_rows, width = idx.shape[0], table.shape[1]
    per_worker = n_rows // n_workers
    mesh = plsc.VectorSubcoreMesh(core_axis_name="c", subcore_axis_name="s")

    @functools.partial(
        pl.kernel, mesh=mesh, out_type=jax.ShapeDtypeStruct((n_rows, width), table.dtype),
        scratch_types=[pltpu.VMEM((chunk,), jnp.int32), pltpu.VMEM((chunk, width), table.dtype),
                       pltpu.SemaphoreType.DMA])
    def gather(table_hbm, idx_hbm, out_hbm, idx_v, rows_v, sem):
        base = (lax.axis_index("s") * info.num_cores + lax.axis_index("c")) * per_worker

        @pl.loop(0, per_worker // chunk)
        def _(j):
            off = pl.multiple_of(base + j * chunk, chunk)
            pltpu.sync_copy(idx_hbm.at[pl.ds(off, chunk)], idx_v)
            pltpu.async_copy(table_hbm.at[idx_v], rows_v, sem).wait()
            pltpu.sync_copy(rows_v, out_hbm.at[pl.ds(off, chunk)])

    return gather(table, idx)


def _prep_kernel(x_ref, front_ref, g1_ref, w_ref, wqt_ref, wvt_ref, bd_ref, qg_ref, kg_ref,
                 hy_ref, qt_ref, k_ref, vt_ref, sg_ref, *, lead):
    if lead:
        @pl.when(pl.program_id(1) == 0)
        def _():
            hy_ref[...] = front_ref[...]

        @pl.when(pl.program_id(1) > 0)
        def _():
            _prep_tile(x_ref, g1_ref, w_ref, wqt_ref, wvt_ref, bd_ref, qg_ref, kg_ref,
                       hy_ref, qt_ref, k_ref, vt_ref, sg_ref)
    else:
        _prep_tile(x_ref, g1_ref, w_ref, wqt_ref, wvt_ref, bd_ref, qg_ref, kg_ref,
                   hy_ref, qt_ref, k_ref, vt_ref, sg_ref)


def _prep_tile(x_ref, g1_ref, w_ref, wqt_ref, wvt_ref, bd_ref, qg_ref, kg_ref,
               hy_ref, qt_ref, k_ref, vt_ref, sg_ref):
    x = x_ref[...]
    ms = jnp.mean(x * x, axis=-1, keepdims=True)
    h = (x * lax.rsqrt(ms + RMS_EPS) * g1_ref[...]).astype(BF16)
    nt_dims = (((1,), (1,)), ((), ()))

    def proj(lo, hi):
        return jnp.dot(h, w_ref[:, lo:hi], preferred_element_type=F32)

    def proj_t(wt_ref):
        return lax.dot_general(wt_ref[...], h, nt_dims, preferred_element_type=F32)

    c0, c1, c2, c3 = COL_HY, COL_HY + COL_Q, COL_HY + COL_Q + COL_K, COL_HY + COL_Q + COL_K + COL_V
    hy_ref[...] = proj(0, c0).astype(BF16)
    qt = proj_t(wqt_ref)
    msq = jnp.dot(bd_ref[...], (qt * qt).astype(BF16), preferred_element_type=F32)
    qt_ref[...] = (qt * lax.rsqrt(msq + RMS_EPS) * qg_ref[...]).astype(BF16)
    kk = proj(c1, c2)
    msk = jnp.dot((kk * kk).astype(BF16), bd_ref[...], preferred_element_type=F32)
    k_ref[...] = (kk * lax.rsqrt(msk + RMS_EPS) * kg_ref[...]).astype(BF16)
    vt_ref[...] = proj_t(wvt_ref).astype(BF16)
    sg_ref[...] = jax.nn.sigmoid(proj(c3, IN_COLS)).astype(BF16)


def _prep(x, front, g1, w_in_bf, wqt, wvt, bd, qg_col, kg, *, tm):
    B, S, D = x.shape
    nt = S // tm
    lead = 0 if front is None else 1
    if front is None:
        front = jnp.zeros((tm, COL_HY), BF16)
    src = lambda i: jnp.maximum(i - lead, 0)
    tok = lambda w: pl.BlockSpec((None, tm, w), lambda b, i: (b, src(i), 0))
    tok_t = lambda w: pl.BlockSpec((None, w, tm), lambda b, i: (b, 0, src(i)))
    return pl.pallas_call(
        functools.partial(_prep_kernel, lead=lead),
        grid=(B, nt + lead),
        in_specs=[tok(D), _const_spec((tm, COL_HY)), _const_spec((1, D)), _const_spec((D, IN_COLS)),
                  _const_spec((COL_Q, D)), _const_spec((COL_V, D)), _const_spec((COL_Q, COL_Q)),
                  _const_spec((COL_Q, 1)), _const_spec((1, COL_K))],
        out_specs=[pl.BlockSpec((None, tm, COL_HY), lambda b, i: (b, i, 0)),
                   tok_t(COL_Q), tok(COL_K), tok_t(COL_V), tok(COL_G)],
        out_shape=[jax.ShapeDtypeStruct((B, S + lead * tm, COL_HY), BF16),
                   jax.ShapeDtypeStruct((B, COL_Q, S), BF16), jax.ShapeDtypeStruct((B, S, COL_K), BF16),
                   jax.ShapeDtypeStruct((B, COL_V, S), BF16), jax.ShapeDtypeStruct((B, S, COL_G), BF16)],
        compiler_params=_cparams(("parallel", "arbitrary"), 48),
        name="prep",
    )(x, front, g1, w_in_bf, wqt, wvt, bd, qg_col, kg)


def _filt_kernel(w1t_ref, w1c_ref, w1s_ref, b1_ref, fr1_ref, w2t_ref, b2_ref, fr2_ref, w3f_ref, w3b_ref,
                 bands_ref, deltas_ref, kt_ref, hid_ref, *, n_seq, s_real):
    lk = kt_ref.shape[1]
    hi = lax.Precision.HIGHEST
    u = lax.broadcasted_iota(jnp.int32, (1, lk), 1)
    d = u - s_real
    pos = jnp.abs(d).astype(F32)
    t = pos / float(max(n_seq - 1, 1))

    @pl.when(pl.program_id(0) == 0)
    def _():
        w = 2.0 * math.pi * pos / float(n_seq)
        ang = bands_ref[...] * w
        pre = (w1t_ref[...] * t
               + jnp.dot(w1c_ref[...], jnp.cos(ang), precision=hi, preferred_element_type=F32)
               - jnp.dot(w1s_ref[...], jnp.sin(ang), precision=hi, preferred_element_type=F32)
               + b1_ref[...])
        h1 = jnp.sin(fr1_ref[...] * pre)
        h2 = jnp.sin(fr2_ref[...] * (jnp.dot(w2t_ref[...], h1, precision=hi, preferred_element_type=F32)
                                     + b2_ref[...]))
        hid_ref[...] = h2

    h2 = hid_ref[...]
    hf = jnp.dot(w3f_ref[...], h2, precision=hi, preferred_element_type=F32)
    hb = jnp.dot(w3b_ref[...], h2, precision=hi, preferred_element_type=F32)
    decay = jnp.exp(-t * deltas_ref[...])
    valid = pos <= float(n_seq - 1)
    kun = jnp.where(valid, jnp.where(d >= 0, hf, hb) * decay, 0.0)
    nrm = lax.rsqrt(jnp.sum(kun * kun, axis=1, keepdims=True) + RMS_EPS)
    kt_ref[...] = kun * nrm


def _hyena_filter(fw1, fb1, ffr1, fw2, fb2, ffr2, fw3, *, s_real):
    n_seq = s_real + N_META
    lk = 2 * s_real + CONV_BLK
    cf = 64
    col = lambda a: a.reshape(-1, 1)
    w1t = fw1.T
    bands = jnp.linspace(1e-4, FILT_BANDS - 1, FILT_BANDS, dtype=F32).reshape(-1, 1)
    max_decay = math.log(DECAY_TARGET) / FAST_DECAY_PCT
    min_decay = math.log(DECAY_TARGET) / SLOW_DECAY_PCT
    deltas = jnp.abs(jnp.linspace(min_decay, max_decay, HY_WIDTH, dtype=F32)).reshape(-1, 1)
    w3t = fw3.T
    h = FILT_HIDDEN
    return pl.pallas_call(
        functools.partial(_filt_kernel, n_seq=n_seq, s_real=s_real),
        grid=(HY_WIDTH // cf,),
        in_specs=[_const_spec((h, 1)), _const_spec((h, FILT_BANDS)), _const_spec((h, FILT_BANDS)),
                  _const_spec((h, 1)), _const_spec((h, 1)), _const_spec((h, h)), _const_spec((h, 1)),
                  _const_spec((h, 1)),
                  pl.BlockSpec((cf, h), lambda i: (i, 0)), pl.BlockSpec((cf, h), lambda i: (i, 0)),
                  _const_spec((FILT_BANDS, 1)), pl.BlockSpec((cf, 1), lambda i: (i, 0))],
        out_specs=pl.BlockSpec((cf, lk), lambda i: (i, 0)),
        out_shape=jax.ShapeDtypeStruct((HY_WIDTH, lk), F32),
        scratch_shapes=[pltpu.VMEM((h, lk), F32)],
        compiler_params=_cparams(("arbitrary",), 48),
        name="hyena_filter",
    )(w1t[:, 0:1], w1t[:, 1:1 + FILT_BANDS], w1t[:, 1 + FILT_BANDS:], col(fb1), col(ffr1), fw2.T, col(fb2),
      col(ffr2), w3t[:HY_WIDTH], w3t[HY_WIDTH:], bands, deltas)


def _hy_pre_kernel(x0_ref, x1_ref, v_ref, p0_ref, p1_ref, pv_ref, n0_ref, n1_ref, nv_ref,
                   scw_ref, scb_ref, skip_ref, zt_ref, a_ref, bt_ref, zs_ref):
    jt = pl.program_id(1)
    last = pl.num_programs(1) - 1
    nb_batch = x0_ref.shape[0]
    row = lax.broadcasted_iota(jnp.int32, (CONV_BLK, LANES), 0)
    keep_next = (jt < last).astype(F32)
    z_keep = jnp.logical_or(jt > 0, row >= CONV_BLK - N_META)

    def short_conv(cur_ref, prev_ref, next_ref, comp, b):
        cur = cur_ref[b].astype(F32)
        prev = prev_ref[b, 15:16, :].astype(F32)
        nxt = next_ref[b, 0:1, :].astype(F32) * keep_next
        up = jnp.where(row == 0, prev, pltpu.roll(cur, 1, 0))
        dn = jnp.where(row == CONV_BLK - 1, nxt, pltpu.roll(cur, CONV_BLK - 1, 0))
        w = scw_ref[comp]
        return up * w[0:1] + cur * w[1:2] + dn * w[2:3] + scb_ref[comp]

    def body(b, carry):
        x0c = short_conv(x0_ref, p0_ref, n0_ref, 0, b)
        x1c = short_conv(x1_ref, p1_ref, n1_ref, 1, b)
        vc = short_conv(v_ref, pv_ref, nv_ref, 2, b)
        z = jnp.where(z_keep, x1c * vc, 0.0)
        zs_ref[b] = z.T
        a_ref[b] = x0c.astype(BF16)
        bt_ref[b] = (x0c * z * skip_ref[...]).astype(BF16)
        return carry

    lax.fori_loop(0, nb_batch, body, 0)
    zt_ref[...] = pltpu.einshape("bcl->cbl", zs_ref[...]).astype(BF16)


def _hy_pre(hy_ext, scw, scb, skip, *, s_real):
    B = hy_ext.shape[0]
    nb = s_real // CONV_BLK
    ncb = HY_WIDTH // LANES
    sub = 16
    r16 = CONV_BLK // sub
    last16 = (s_real + HY_PAD) // sub - 1
    cur = lambda comp: pl.BlockSpec((B, CONV_BLK, LANES), lambda cb, jt: (0, jt + 1, comp * ncb + cb))
    prv = lambda comp: pl.BlockSpec((B, sub, LANES), lambda cb, jt: (0, r16 * (jt + 1) - 1, comp * ncb + cb))
    nxt = lambda comp: pl.BlockSpec(
        (B, sub, LANES), lambda cb, jt: (0, jnp.minimum(r16 * (jt + 2), last16), comp * ncb + cb))
    nat = pl.BlockSpec((B, CONV_BLK, LANES), lambda cb, jt: (0, jnp.maximum(jt - 1, 0), cb))
    scw_r = scw.reshape(HY_SHORT, 3, ncb, LANES).transpose(2, 1, 0, 3)
    scb_r = scb.reshape(3, ncb, 1, LANES).transpose(1, 0, 2, 3)
    skip_r = skip.reshape(ncb, 1, LANES)
    return pl.pallas_call(
        _hy_pre_kernel,
        grid=(ncb, nb + 1),
        in_specs=[cur(0), cur(1), cur(2), prv(0), prv(1), prv(2), nxt(0), nxt(1), nxt(2),
                  pl.BlockSpec((None, 3, HY_SHORT, LANES), lambda cb, jt: (cb, 0, 0, 0)),
                  pl.BlockSpec((None, 3, 1, LANES), lambda cb, jt: (cb, 0, 0, 0)),
                  pl.BlockSpec((None, 1, LANES), lambda cb, jt: (cb, 0, 0))],
        out_specs=[pl.BlockSpec((LANES, None, B, CONV_BLK), lambda cb, jt: (cb, jt, 0, 0)), nat, nat],
        out_shape=[jax.ShapeDtypeStruct((HY_WIDTH, nb + 1, B, CONV_BLK), BF16),
                   jax.ShapeDtypeStruct((B, s_real, HY_WIDTH), BF16),
                   jax.ShapeDtypeStruct((B, s_real, HY_WIDTH), BF16)],
        scratch_shapes=[pltpu.VMEM((B, LANES, CONV_BLK), F32)],
        compiler_params=_cparams(("parallel", "arbitrary"), 48),
        name="hyena_pre",
    )(hy_ext, hy_ext, hy_ext, hy_ext, hy_ext, hy_ext, hy_ext, hy_ext, hy_ext, scw_r, scb_r, skip_r)


def _hy_conv_kernel(kt_ref, z_ref, y_ref, g_ref, *, nb):
    cbk = z_ref.shape[0]
    B = z_ref.shape[2]
    s_real = nb * CONV_BLK
    nq = 2 * s_real // LANES + 1

    def channel(c, carry):
        def build(q, carry2):
            off = pl.multiple_of(2 * s_real - LANES * q, LANES)
            w = kt_ref[pl.ds(c, 1), pl.ds(off, 2 * LANES)]
            rolled = pltpu.roll(jnp.broadcast_to(w, (LANES, 2 * LANES)), 0, 1, stride=1, stride_axis=0)
            g_ref[pl.ds(pl.multiple_of(LANES * q, LANES), LANES), :] = rolled[:, LANES:].astype(BF16)
            return carry2

        lax.fori_loop(0, nq, build, 0)
        y_ref[c] = jnp.zeros(y_ref.shape[1:], F32)
        for delta in range(-(nb - 1), nb + 1):
            base = CONV_BLK * (nb - delta)
            tile = jnp.concatenate([g_ref[base + LANES:base + LANES + CONV_BLK, :],
                                    g_ref[base:base + CONV_BLK, :]], axis=1)
            jb_lo = max(0, 1 - delta)
            n = min(nb, nb - delta) - jb_lo + 1
            o_lo = jb_lo + delta - 1
            lhs = z_ref[c, jb_lo:jb_lo + n].reshape(n * B, CONV_BLK)
            y_ref[c, o_lo:o_lo + n] += jnp.dot(lhs, tile, preferred_element_type=F32).reshape(n, B, CONV_BLK)
        return carry

    lax.fori_loop(0, cbk, channel, 0)


def _hy_conv(kt, zt, *, cbk=8):
    C, nb1, B, _ = zt.shape
    nb = nb1 - 1
    lk = kt.shape[1]
    return pl.pallas_call(
        functools.partial(_hy_conv_kernel, nb=nb),
        grid=(C // cbk,),
        in_specs=[pl.BlockSpec((cbk, lk), lambda i: (i, 0)),
                  pl.BlockSpec((cbk, nb1, B, CONV_BLK), lambda i: (i, 0, 0, 0))],
        out_specs=pl.BlockSpec((cbk, nb, B, CONV_BLK), lambda i: (i, 0, 0, 0)),
        out_shape=jax.ShapeDtypeStruct((C, nb, B, CONV_BLK), F32),
        scratch_shapes=[pltpu.VMEM((2 * nb * CONV_BLK + LANES, LANES), BF16)],
        compiler_params=_cparams(("parallel",), 48),
        name="hyena_conv",
    )(kt, zt)


def _hy_post_kernel(yt_ref, a_ref, bt_ref, o_ref, ys_ref):
    ys_ref[...] = pltpu.einshape("cbl->bcl", yt_ref[...])

    def body(b, carry):
        y = ys_ref[b].T
        o_ref[b] = (a_ref[b].astype(F32) * y + bt_ref[b].astype(F32)).astype(BF16)
        return carry

    lax.fori_loop(0, a_ref.shape[0], body, 0)


def _hy_post(yt, a, bt):
    C, nb, B, _ = yt.shape
    nat = pl.BlockSpec((B, CONV_BLK, LANES), lambda cb, ib: (0, ib, cb))
    return pl.pallas_call(
        _hy_post_kernel,
        grid=(C // LANES, nb),
        in_specs=[pl.BlockSpec((LANES, None, B, CONV_BLK), lambda cb, ib: (cb, ib, 0, 0)), nat, nat],
        out_specs=nat,
        out_shape=jax.ShapeDtypeStruct(a.shape, BF16),
        scratch_shapes=[pltpu.VMEM((B, LANES, CONV_BLK), F32)],
        compiler_params=_cparams(("parallel", "parallel"), 48),
        name="hyena_post",
    )(yt, a, bt)


def _attn_kernel(far_ref, qt_ref, k_ref, vt_ref, km_ref, vtm_ref, near_ref, bmeta_ref, lam_ref, sg_ref,
                 o_ref, m_ref, l_ref, acc_ref, *, blk):
    h = pl.program_id(1)
    i = pl.program_id(2)
    nk = k_ref.shape[0] // blk
    qt = qt_ref[...]
    rowi = lax.broadcasted_iota(jnp.int32, qt.shape, 0)
    zero = jnp.zeros_like(qt)
    qmaps = (jnp.where(rowi < ATT_HEAD_DIM, qt, zero), jnp.where(rowi >= ATT_HEAD_DIM, qt, zero))

    def step(kblk, vtblk, bias, first=False):
        for c in range(2):
            s = jnp.dot(kblk, qmaps[c], preferred_element_type=F32) + bias
            smax = jnp.max(s, axis=0, keepdims=True)
            if first:
                m_new = smax
                p = jnp.exp2(s - m_new)
                l_ref[c] = jnp.sum(p, axis=0, keepdims=True)
                acc_ref[c] = jnp.dot(vtblk, p.astype(BF16), preferred_element_type=F32)
            else:
                m_old = m_ref[c]
                m_new = jnp.maximum(m_old, smax)
                alpha = jnp.exp2(m_old - m_new)
                p = jnp.exp2(s - m_new)
                l_ref[c] = alpha * l_ref[c] + jnp.sum(p, axis=0, keepdims=True)
                acc_ref[c] = alpha * acc_ref[c] + jnp.dot(vtblk, p.astype(BF16), preferred_element_type=F32)
            m_ref[c] = m_new

    def kv(j):
        r0 = pl.multiple_of(j * blk, blk)
        return k_ref[pl.ds(r0, blk), :], vt_ref[:, pl.ds(r0, blk)]

    step(km_ref[...], vtm_ref[...], bmeta_ref[jnp.minimum(i, 1)], first=True)

    def far_loop(lo, hi, c_far):
        def body(j, carry):
            kb, vb = kv(j)
            step(kb, vb, c_far)
            return carry
        lax.fori_loop(lo, hi, body, 0)

    far_loop(0, jnp.maximum(i - 1, 0), far_ref[2 * h])
    for dj in (-1, 0, 1):
        @pl.when(jnp.logical_and(i + dj >= 0, i + dj < nk))
        def _():
            kb, vb = kv(i + dj)
            step(kb, vb, near_ref[dj + 1])
    far_loop(jnp.minimum(i + 2, nk), nk, far_ref[2 * h + 1])

    lq = lam_ref[...]
    lam = (jnp.exp(jnp.sum(lq[0:1] * lq[1:2], axis=1, keepdims=True))
           - jnp.exp(jnp.sum(lq[2:3] * lq[3:4], axis=1, keepdims=True)) + LAM_INIT)
    ot = acc_ref[0] / l_ref[0] - lam * (acc_ref[1] / l_ref[1])
    ot = ot * lax.rsqrt(jnp.mean(ot * ot, axis=0, keepdims=True) + RMS_EPS) * sg_ref[...] * (1.0 - LAM_INIT)
    o_ref[...] = ot.T.astype(BF16)


def _t5_bucket(rel):
    nb = N_BUCKETS // 2
    max_exact = nb // 2
    ret = jnp.where(rel > 0, nb, 0)
    n = jnp.abs(rel)
    nf = jnp.maximum(n, 1).astype(F32)
    large = max_exact + (jnp.log(nf / max_exact) / math.log(MAX_DISTANCE / max_exact)
                         * (nb - max_exact)).astype(jnp.int32)
    large = jnp.minimum(large, nb - 1)
    return ret + jnp.where(n < max_exact, n, large)


def _attn_bias_tables(rel_bias, blk):
    table = rel_bias.astype(F32) * LOG2E

    def bias_of(rel):
        bucket = _t5_bucket(rel)
        sel = bucket[None] == jnp.arange(N_BUCKETS, dtype=jnp.int32).reshape((N_BUCKETS,) + (1,) * rel.ndim)
        return jnp.stack([jnp.sum(jnp.where(sel, table[:, h].reshape((N_BUCKETS,) + (1,) * rel.ndim), 0.0), axis=0)
                          for h in range(ATT_HEADS)])

    r = jnp.arange(blk, dtype=jnp.int32)
    rel = jnp.stack([dj * blk + r[:, None] - r[None, :] for dj in (-1, 0, 1)])
    near = bias_of(rel)
    m = jnp.arange(N_META, dtype=jnp.int32)
    rel_m0 = m[:, None] - (N_META + r[None, :])
    rel_m1 = rel_m0 - blk
    bmeta = bias_of(jnp.stack([rel_m0, rel_m1]))
    far = bias_of(jnp.array([-(blk + 1), blk + 1], jnp.int32))
    return near, bmeta, far.reshape(-1)


def _attention(qt, k, vt, kmeta, vtmeta, near, bmeta, far, lam, subln_col, *, blk):
    B, S, _ = k.shape
    grid_spec = pltpu.PrefetchScalarGridSpec(
        num_scalar_prefetch=1,
        grid=(B, ATT_HEADS, S // blk),
        in_specs=[pl.BlockSpec((None, ATT_V_DIM, blk), lambda b, h, i, *_: (b, h, i)),
                  pl.BlockSpec((None, S, ATT_V_DIM), lambda b, h, i, *_: (b, 0, h)),
                  pl.BlockSpec((None, ATT_V_DIM, S), lambda b, h, i, *_: (b, h, 0)),
                  pl.BlockSpec((N_META, ATT_V_DIM), lambda b, h, i, *_: (0, h)),
                  pl.BlockSpec((ATT_V_DIM, N_META), lambda b, h, i, *_: (h, 0)),
                  pl.BlockSpec((None, 3, blk, blk), lambda b, h, i, *_: (h, 0, 0, 0)),
                  pl.BlockSpec((None, 2, N_META, blk), lambda b, h, i, *_: (h, 0, 0, 0)),
                  pl.BlockSpec((4, ATT_HEAD_DIM), lambda b, h, i, *_: (0, 0)),
                  pl.BlockSpec((ATT_V_DIM, 1), lambda b, h, i, *_: (0, 0))],
        out_specs=pl.BlockSpec((None, blk, ATT_V_DIM), lambda b, h, i, *_: (b, i, h)),
        scratch_shapes=[pltpu.VMEM((2, 1, blk), F32), pltpu.VMEM((2, 1, blk), F32),
                        pltpu.VMEM((2, ATT_V_DIM, blk), F32)],
    )
    return pl.pallas_call(
        functools.partial(_attn_kernel, blk=blk),
        grid_spec=grid_spec,
        out_shape=jax.ShapeDtypeStruct((B, S, ATT_WIDTH), BF16),
        compiler_params=_cparams(("parallel", "parallel", "arbitrary"), 48),
        name="diff_attention",
    )(far, qt, k, vt, kmeta, vtmeta, near, bmeta, lam, subln_col)


def _merge_kernel(x_ref, ya_ref, yb_ref, sg_ref, pa_ref, pb_ref, wo_ref, g2_ref, wrh_ref, wrl_ref, br_ref,
                  tri_ref, x1_ref, h2_ref, ti_ref, tg_ref, rk_ref, cnt_ref):
    ma = jnp.dot(ya_ref[...], pa_ref[...], preferred_element_type=F32)
    mb = jnp.dot(yb_ref[...], pb_ref[...], preferred_element_type=F32)
    m = sg_ref[:, :D_MODEL].astype(F32) * ma + sg_ref[:, D_MODEL:].astype(F32) * mb
    x1 = x_ref[...] + jnp.dot(m.astype(BF16), wo_ref[...], preferred_element_type=F32)
    x1_ref[...] = x1
    h2 = x1 * lax.rsqrt(jnp.mean(x1 * x1, axis=-1, keepdims=True) + RMS_EPS) * g2_ref[...]
    h2_hi = h2.astype(BF16)
    h2_ref[...] = _pack_rows(h2)
    h2_lo = (h2 - h2_hi.astype(F32)).astype(BF16)
    wrh = wrh_ref[...]
    logits = (jnp.dot(h2_hi, wrh, preferred_element_type=F32) + jnp.dot(h2_lo, wrh, preferred_element_type=F32)
              + jnp.dot(h2_hi, wrl_ref[...], preferred_element_type=F32) + br_ref[...])
    lane = lax.broadcasted_iota(jnp.int32, logits.shape, 1)
    slot = lax.broadcasted_iota(jnp.int32, ti_ref.shape, 1)
    top_i = jnp.zeros(ti_ref.shape, jnp.int32)
    top_v = jnp.zeros(tg_ref.shape, F32)
    work = logits
    picks = []
    for kk in range(TOP_K):
        mx = jnp.max(work, axis=1, keepdims=True)
        idx = jnp.min(jnp.where(work == mx, lane, N_EXPERTS), axis=1, keepdims=True)
        pick = lane == idx
        picks.append(pick)
        top_i = jnp.where(slot == kk, idx, top_i)
        top_v = jnp.where(slot == kk, mx, top_v)
        work = jnp.where(pick, -jnp.inf, work)
    ex = jnp.exp(top_v - jnp.max(top_v, axis=1, keepdims=True))
    ti_ref[...] = top_i
    tg_ref[...] = ex / jnp.sum(ex, axis=1, keepdims=True)

    @pl.when(pl.program_id(0) == 0)
    def _():
        cnt_ref[...] = jnp.zeros_like(cnt_ref)

    chosen = jnp.where(work == -jnp.inf, 1.0, 0.0)
    before = cnt_ref[...] + jnp.dot(tri_ref[...], chosen.astype(BF16), preferred_element_type=F32)
    rank = jnp.zeros(rk_ref.shape, F32)
    for kk in range(TOP_K):
        rk = jnp.sum(jnp.where(picks[kk], before, 0.0), axis=1, keepdims=True)
        rank = jnp.where(slot == kk, rk, rank)
    rk_ref[...] = rank.astype(jnp.int32)
    cnt_ref[...] += jnp.sum(chosen, axis=0, keepdims=True)


def _merge(x, ya, yb, sg, pa, pb, wo, g2, wr_hi, wr_lo, br, *, tm):
    T, D = x.shape
    tok = lambda w: pl.BlockSpec((tm, w), lambda i: (i, 0))
    tri = jnp.tril(jnp.ones((tm, tm), F32), -1).astype(BF16)
    return pl.pallas_call(
        _merge_kernel,
        grid=(T // tm,),
        in_specs=[tok(D), tok(HY_WIDTH), tok(ATT_WIDTH), tok(COL_G), _const_spec((HY_WIDTH, D)),
                  _const_spec((ATT_WIDTH, D)), _const_spec((D, D)), _const_spec((1, D)),
                  _const_spec((D, N_EXPERTS)), _const_spec((D, N_EXPERTS)), _const_spec((1, N_EXPERTS)),
                  _const_spec((tm, tm))],
        out_specs=[tok(D), tok(D // 2), tok(TOP_K), tok(TOP_K), tok(TOP_K),
                   pl.BlockSpec((1, N_EXPERTS), lambda i: (0, 0))],
        out_shape=[jax.ShapeDtypeStruct((T, D), F32), jax.ShapeDtypeStruct((T, D // 2), jnp.uint32),
                   jax.ShapeDtypeStruct((T, TOP_K), jnp.int32), jax.ShapeDtypeStruct((T, TOP_K), F32),
                   jax.ShapeDtypeStruct((T, TOP_K), jnp.int32), jax.ShapeDtypeStruct((1, N_EXPERTS), F32)],
        compiler_params=_cparams(("arbitrary",), 48),
        name="merge_router",
    )(x, ya, yb, sg, pa, pb, wo, g2, wr_hi, wr_lo, br, tri)


def _moe_kernel(iblk_ref, iexp_ref, start_ref, count_ref, nitem_ref, xs_ref, w1_ref, b1_ref, w2_ref, b2_ref, ys_ref):
    i = pl.program_id(0)
    tm = xs_ref.shape[0]

    @pl.when(i < nitem_ref[0])
    def _():
        blk = iblk_ref[i]
        e = iexp_ref[i]
        x = jnp.concatenate(_unpack_rows(xs_ref[...]), axis=1).astype(BF16)
        u = jnp.dot(x, w1_ref[...], preferred_element_type=F32) + b1_ref[...]
        glu = jnp.minimum(u[:, :D_FF], SWIGLU_LIMIT)
        lin = jnp.clip(u[:, D_FF:], -SWIGLU_LIMIT, SWIGLU_LIMIT)
        a = glu * jax.nn.sigmoid(SWIGLU_ALPHA * glu) * (lin + 1.0)
        y = _pack_rows(jnp.dot(a.astype(BF16), w2_ref[...], preferred_element_type=F32) + b2_ref[...])
        row = lax.broadcasted_iota(jnp.int32, (tm, 1), 0) + blk * tm
        mine = jnp.logical_and(row >= start_ref[e], row < start_ref[e] + count_ref[e])
        first_visit = jnp.logical_or(i == 0, iblk_ref[jnp.maximum(i - 1, 0)] != blk)

        @pl.when(first_visit)
        def _():
            ys_ref[...] = jnp.where(mine, y, jnp.zeros_like(y))

        @pl.when(jnp.logical_not(first_visit))
        def _():
            ys_ref[...] = jnp.where(mine, y, ys_ref[...])


def _moe_experts(item_blk, item_exp, start, count, n_items, xs, w1, b1, w2, b2, *, tm):
    A, D = xs.shape[0], xs.shape[1] * 2
    grid_spec = pltpu.PrefetchScalarGridSpec(
        num_scalar_prefetch=5,
        grid=(item_blk.shape[0],),
        in_specs=[pl.BlockSpec((tm, D // 2), lambda i, ib, ie, *_: (ib[i], 0)),
                  pl.BlockSpec((None, D, 2 * D_FF), lambda i, ib, ie, *_: (ie[i], 0, 0)),
                  pl.BlockSpec((None, 1, 2 * D_FF), lambda i, ib, ie, *_: (ie[i], 0, 0)),
                  pl.BlockSpec((None, D_FF, D), lambda i, ib, ie, *_: (ie[i], 0, 0)),
                  pl.BlockSpec((None, 1, D), lambda i, ib, ie, *_: (ie[i], 0, 0))],
        out_specs=pl.BlockSpec((tm, D // 2), lambda i, ib, ie, *_: (ib[i], 0)),
    )
    return pl.pallas_call(
        _moe_kernel,
        grid_spec=grid_spec,
        out_shape=jax.ShapeDtypeStruct((A, D // 2), jnp.uint32),
        compiler_params=_cparams(("arbitrary",), 48),
        name="moe_experts",
    )(item_blk, item_exp, start, count, n_items, xs, w1, b1, w2, b2)


def _combine_kernel(x1_ref, yg_ref, tg_ref, o_ref):
    half = D_MODEL // 2
    acc_lo = x1_ref[:, :half]
    acc_hi = x1_ref[:, half:]
    g = tg_ref[...]
    for kk in range(TOP_K):
        lo, hi = _unpack_rows(yg_ref[kk])
        acc_lo = acc_lo + g[:, kk:kk + 1] * lo
        acc_hi = acc_hi + g[:, kk:kk + 1] * hi
    o_ref[:, :half] = acc_lo
    o_ref[:, half:] = acc_hi


def _combine(x1, yg, tg, *, tm):
    T, D = x1.shape
    return pl.pallas_call(
        _combine_kernel,
        grid=(T // tm,),
        in_specs=[pl.BlockSpec((tm, D), lambda i: (i, 0)), pl.BlockSpec((TOP_K, tm, D // 2), lambda i: (0, i, 0)),
                  pl.BlockSpec((tm, TOP_K), lambda i: (i, 0))],
        out_specs=pl.BlockSpec((tm, D), lambda i: (i, 0)),
        out_shape=jax.ShapeDtypeStruct((T, D), F32),
        compiler_params=_cparams(("parallel",), 48),
        name="moe_combine",
    )(x1, yg, tg)


def _lookup(table, idx):
    n = table.shape[0]
    sel = idx[None] == jnp.arange(n, dtype=jnp.int32).reshape((n,) + (1,) * idx.ndim)
    return jnp.sum(jnp.where(sel, table.reshape((n,) + (1,) * idx.ndim), 0), axis=0)


def _route(top_i, rank, counts, *, tm):
    T = top_i.shape[0]
    A = T * TOP_K
    counts = counts.reshape(N_EXPERTS).astype(jnp.int32)
    start = jnp.cumsum(counts) - counts
    flat_e = top_i.reshape(A // LANES, LANES)
    slot_of = (_lookup(start, flat_e) + rank.reshape(A // LANES, LANES)).reshape(T, TOP_K)
    slot_tok = (jnp.argsort(top_i.reshape(A)) // TOP_K).astype(jnp.int32)
    first_blk = start // tm
    n_it = jnp.where(counts > 0, (start + counts - 1) // tm - first_blk + 1, 0)
    it_end = jnp.cumsum(n_it)
    n_items = it_end[-1]
    n_max = A // tm + N_EXPERTS
    i = jnp.minimum(jnp.arange(n_max, dtype=jnp.int32), n_items - 1)
    item_exp = jnp.sum(it_end[None, :] <= i[:, None], axis=1, dtype=jnp.int32)
    item_blk = _lookup(first_blk, item_exp) + i - _lookup(it_end - n_it, item_exp)
    return slot_tok, slot_of, item_blk.astype(jnp.int32), item_exp, start, counts, n_items.reshape(1)


def _trunk(x, meta_proj, w, tabs, *, tm_prep, tm_tok, tm_moe, attn_blk):
    B, S, D = x.shape
    T = B * S
    hy_m, _, k_m, vt_m, _ = meta_proj
    assert tm_prep == HY_PAD
    front = jnp.concatenate([jnp.zeros((HY_PAD - N_META, COL_HY), BF16), hy_m[0]], axis=0)
    hy_ext, qt, k, vt, sg = _prep(x, front, w['g1'], w['w_in'], w['wqt'], w['wvt'], w['bd'], w['qg'], w['kg'],
                                  tm=tm_prep)
    kt = _hyena_filter(w['fw1'], w['fb1'], w['ffr1'], w['fw2'], w['fb2'], w['ffr2'], w['fw3'], s_real=S)
    zt, a, bt = _hy_pre(hy_ext, w['scw'], w['scb'], w['skip'], s_real=S)
    ya = _hy_post(_hy_conv(kt, zt), a, bt)
    near, bmeta, far = tabs
    yb = _attention(qt, k, vt, k_m[0], vt_m[0], near, bmeta, far, w['lam'], w['subln'], blk=attn_blk)
    x1, h2, top_i, top_g, rank, counts = _merge(
        x.reshape(T, D), ya.reshape(T, HY_WIDTH), yb.reshape(T, ATT_WIDTH), sg.reshape(T, COL_G),
        w['pa'], w['pb'], w['wo'], w['g2'], w['wr_hi'], w['wr_lo'], w['br'], tm=tm_tok)
    slot_tok, slot_of, item_blk, item_exp, start, cnt, n_items = _route(top_i, rank, counts, tm=tm_moe)
    xs = _sc_gather_rows(h2, slot_tok)
    ys = _moe_experts(item_blk, item_exp, start, cnt, n_items, xs, w['w1'], w['b1'], w['w2'], w['b2'], tm=tm_moe)
    yg = _sc_gather_rows(ys, slot_of.T.reshape(T * TOP_K)).reshape(TOP_K, T, D // 2)
    return _combine(x1, yg, top_g, tm=tm_tok).reshape(B, S, D)


def kernel(x_prompt, x_sample, meta_tokens, rel_bias, norm1_g, w_in, short_conv_w, short_conv_b, filt_w1, filt_b1,
           filt_freq1, filt_w2, filt_b2, filt_freq2, filt_w3, hy_skip, proj_a, q_norm_g, k_norm_g, lambda_q1,
           lambda_k1, lambda_q2, lambda_k2, subln_g, proj_b, w_out, norm2_g, w_router, b_router, w_mlp1, b_mlp1,
           w_mlp2, b_mlp2):
    l = 0
    n_hm = COL_Q // ATT_HEAD_DIM
    w_in_bf = w_in[l].astype(BF16)
    c_q, c_v = COL_HY, COL_HY + COL_Q + COL_K
    wr_hi = w_router[l].astype(BF16)
    w = {
        'g1': norm1_g[l][None], 'w_in': w_in_bf,
        'wqt': w_in_bf[:, c_q:c_q + COL_Q].T, 'wvt': w_in_bf[:, c_v:c_v + COL_V].T,
        'bd': jnp.kron(jnp.eye(n_hm, dtype=F32), jnp.full((ATT_HEAD_DIM, ATT_HEAD_DIM), 1.0 / ATT_HEAD_DIM, F32)
                       ).astype(BF16),
        'qg': (jnp.tile(q_norm_g[l], n_hm) * (ATT_HEAD_DIM ** -0.5 * LOG2E))[:, None],
        'kg': jnp.tile(k_norm_g[l], n_hm)[None],
        'wr_hi': wr_hi, 'wr_lo': (w_router[l] - wr_hi.astype(F32)).astype(BF16),
        'fw1': filt_w1[l], 'fb1': filt_b1[l], 'ffr1': filt_freq1[l], 'fw2': filt_w2[l], 'fb2': filt_b2[l],
        'ffr2': filt_freq2[l], 'fw3': filt_w3[l],
        'scw': short_conv_w[l], 'scb': short_conv_b[l], 'skip': hy_skip[l],
        'lam': jnp.stack([lambda_q1[l], lambda_k1[l], lambda_q2[l], lambda_k2[l]]), 'subln': subln_g[l][:, None],
        'pa': proj_a[l].astype(BF16), 'pb': proj_b[l].astype(BF16), 'wo': w_out[l].astype(BF16),
        'g2': norm2_g[l][None], 'br': b_router[l][None],
        'w1': w_mlp1[l].astype(BF16), 'b1': b_mlp1[l][:, None, :], 'w2': w_mlp2[l].astype(BF16),
        'b2': b_mlp2[l][:, None, :],
    }
    meta_proj = _prep(meta_tokens[None], None, w['g1'], w['w_in'], w['wqt'], w['wvt'], w['bd'], w['qg'], w['kg'],
                      tm=N_META)
    tabs = _attn_bias_tables(rel_bias, TILES['attn_blk'])
    y_prompt = _trunk(x_prompt, meta_proj, w, tabs, **TILES)
    y_sample = _trunk(x_sample, meta_proj, w, tabs, **TILES)
    return (y_prompt, y_sample)
```

```python
import functools
import math

import jax
import jax.numpy as jnp
import numpy as np
from jax import lax
from jax.experimental import pallas as pl
from jax.experimental.pallas import tpu as pltpu
from jax.experimental.pallas import tpu_sc as plsc

F32 = jnp.float32
BF16 = jnp.bfloat16

D_MODEL = 1024
N_META = 16
RMS_EPS = 1e-6
HY_WIDTH = 512
HY_SHORT = 3
FILT_BANDS = 16
FILT_EMB = 1 + 2 * FILT_BANDS
FILT_HIDDEN = 64
DECAY_TARGET = 1e-2
FAST_DECAY_PCT = 0.3
SLOW_DECAY_PCT = 1.5
ATT_HEADS = 4
ATT_HEAD_DIM = 64
ATT_V_DIM = 128
ATT_WIDTH = 512
N_BUCKETS = 32
MAX_DISTANCE = 128
N_EXPERTS = 32
TOP_K = 4
D_FF = 1024
SWIGLU_LIMIT = 7.0
SWIGLU_ALPHA = 1.702
COL_HY = 3 * HY_WIDTH
COL_Q = 512
COL_K = 512
COL_V = 512
COL_G = 2 * D_MODEL
IN_COLS = COL_HY + COL_Q + COL_K + COL_V + COL_G
LAM_INIT = 0.8 - 0.6 * math.exp(-0.3 * 0)
LOG2E = 1.4426950408889634

V7X_VMEM_BYTES = 64 * 1024 * 1024
LANES = 128
CONV_BLK = 256
HY_PAD = 512
TILES = dict(tm_prep=512, tm_tok=512, tm_moe=512, attn_blk=512)


def _cparams(sem, vmem_mb):
    return pltpu.CompilerParams(dimension_semantics=sem, vmem_limit_bytes=vmem_mb * 1024 * 1024)


def _const_spec(shape):
    nd = len(shape)
    return pl.BlockSpec(shape, lambda *_: (0,) * nd, pipeline_mode=pl.Buffered(1))


def _pack_rows(x):
    n = x.shape[1] // 2
    lo = pltpu.bitcast(x[:, :n].astype(BF16).astype(F32), jnp.uint32)
    hi = pltpu.bitcast(x[:, n:].astype(BF16).astype(F32), jnp.uint32)
    return (hi & jnp.uint32(0xFFFF0000)) | (lo >> 16)


def _unpack_rows(w):
    lo = pltpu.bitcast(w << 16, F32)
    hi = pltpu.bitcast(w & jnp.uint32(0xFFFF0000), F32)
    return lo, hi


def _sc_gather_rows(table, idx, *, chunk=128):
    info = plsc.get_sparse_core_info()
    n_workers = info.num_cores * info.num_subcores
    n_rows, width = idx.shape[0], table.shape[1]
    per_worker = n_rows // n_workers
    mesh = plsc.VectorSubcoreMesh(core_axis_name="c", subcore_axis_name="s")

    @functools.partial(
        pl.kernel, mesh=mesh, out_type=jax.ShapeDtypeStruct((n_rows, width), table.dtype),
        scratch_types=[pltpu.VMEM((chunk,), jnp.int32), pltpu.VMEM((chunk, width), table.dtype),
                       pltpu.SemaphoreType.DMA])
    def gather(table_hbm, idx_hbm, out_hbm, idx_v, rows_v, sem):
        base = (lax.axis_index("s") * info.num_cores + lax.axis_index("c")) * per_worker

        @pl.loop(0, per_worker // chunk)
        def _(j):
            off = pl.multiple_of(base + j * chunk, chunk)
            pltpu.sync_copy(idx_hbm.at[pl.ds(off, chunk)], idx_v)
            pltpu.async_copy(table_hbm.at[idx_v], rows_v, sem).wait()
            pltpu.sync_copy(rows_v, out_hbm.at[pl.ds(off, chunk)])

    return gather(table, idx)


def _prep_kernel(x_ref, front_ref, g1_ref, w_ref, wqt_ref, wvt_ref, bd_ref, qg_ref, kg_ref,
                 hy_ref, qt_ref, k_ref, vt_ref, sg_ref, *, lead):
    if lead:
        @pl.when(pl.program_id(1) == 0)
        def _():
            hy_ref[...] = front_ref[...]

        @pl.when(pl.program_id(1) > 0)
        def _():
            _prep_tile(x_ref, g1_ref, w_ref, wqt_ref, wvt_ref, bd_ref, qg_ref, kg_ref,
                       hy_ref, qt_ref, k_ref, vt_ref, sg_ref)
    else:
        _prep_tile(x_ref, g1_ref, w_ref, wqt_ref, wvt_ref, bd_ref, qg_ref, kg_ref,
                   hy_ref, qt_ref, k_ref, vt_ref, sg_ref)


def _prep_tile(x_ref, g1_ref, w_ref, wqt_ref, wvt_ref, bd_ref, qg_ref, kg_ref,
               hy_ref, qt_ref, k_ref, vt_ref, sg_ref):
    x = x_ref[...]
    ms = jnp.mean(x * x, axis=-1, keepdims=True)
    h = (x * lax.rsqrt(ms + RMS_EPS) * g1_ref[...]).astype(BF16)
    nt_dims = (((1,), (1,)), ((), ()))

    def proj(lo, hi):
        return jnp.dot(h, w_ref[:, lo:hi], preferred_element_type=F32)

    def proj_t(wt_ref):
        return lax.dot_general(wt_ref[...], h, nt_dims, preferred_element_type=F32)

    c0, c1, c2, c3 = COL_HY, COL_HY + COL_Q, COL_HY + COL_Q + COL_K, COL_HY + COL_Q + COL_K + COL_V
    hy_ref[...] = proj(0, c0).astype(BF16)
    qt = proj_t(wqt_ref)
    msq = jnp.dot(bd_ref[...], (qt * qt).astype(BF16), preferred_element_type=F32)
    qt_ref[...] = (qt * lax.rsqrt(msq + RMS_EPS) * qg_ref[...]).astype(BF16)
    kk = proj(c1, c2)
    msk = jnp.dot((kk * kk).astype(BF16), bd_ref[...], preferred_element_type=F32)
    k_ref[...] = (kk * lax.rsqrt(msk + RMS_EPS) * kg_ref[...]).astype(BF16)
    vt_ref[...] = proj_t(wvt_ref).astype(BF16)
    sg_ref[...] = jax.nn.sigmoid(proj(c3, IN_COLS)).astype(BF16)


def _prep(x, front, g1, w_in_bf, wqt, wvt, bd, qg_col, kg, *, tm):
    B, S, D = x.shape
    nt = S // tm
    lead = 0 if front is None else 1
    if front is None:
        front = jnp.zeros((tm, COL_HY), BF16)
    src = lambda i: jnp.maximum(i - lead, 0)
    tok = lambda w: pl.BlockSpec((None, tm, w), lambda b, i: (b, src(i), 0))
    tok_t = lambda w: pl.BlockSpec((None, w, tm), lambda b, i: (b, 0, src(i)))
    return pl.pallas_call(
        functools.partial(_prep_kernel, lead=lead),
        grid=(B, nt + lead),
        in_specs=[tok(D), _const_spec((tm, COL_HY)), _const_spec((1, D)), _const_spec((D, IN_COLS)),
                  _const_spec((COL_Q, D)), _const_spec((COL_V, D)), _const_spec((COL_Q, COL_Q)),
                  _const_spec((COL_Q, 1)), _const_spec((1, COL_K))],
        out_specs=[pl.BlockSpec((None, tm, COL_HY), lambda b, i: (b, i, 0)),
                   tok_t(COL_Q), tok(COL_K), tok_t(COL_V), tok(COL_G)],
        out_shape=[jax.ShapeDtypeStruct((B, S + lead * tm, COL_HY), BF16),
                   jax.ShapeDtypeStruct((B, COL_Q, S), BF16), jax.ShapeDtypeStruct((B, S, COL_K), BF16),
                   jax.ShapeDtypeStruct((B, COL_V, S), BF16), jax.ShapeDtypeStruct((B, S, COL_G), BF16)],
        compiler_params=_cparams(("parallel", "arbitrary"), 48),
        name="prep",
    )(x, front, g1, w_in_bf, wqt, wvt, bd, qg_col, kg)


def _filt_kernel(w1t_ref, w1c_ref, w1s_ref, b1_ref, fr1_ref, w2t_ref, b2_ref, fr2_ref, w3f_ref, w3b_ref,
                 bands_ref, deltas_ref, kt_ref, hid_ref, *, n_seq, s_real):
    lk = kt_ref.shape[1]
    hi = lax.Precision.HIGHEST
    u = lax.broadcasted_iota(jnp.int32, (1, lk), 1)
    d = u - s_real
    pos = jnp.abs(d).astype(F32)
    t = pos / float(max(n_seq - 1, 1))

    @pl.when(pl.program_id(0) == 0)
    def _():
        w = 2.0 * math.pi * pos / float(n_seq)
        ang = bands_ref[...] * w
        pre = (w1t_ref[...] * t
               + jnp.dot(w1c_ref[...], jnp.cos(ang), precision=hi, preferred_element_type=F32)
               - jnp.dot(w1s_ref[...], jnp.sin(ang), precision=hi, preferred_element_type=F32)
               + b1_ref[...])
        h1 = jnp.sin(fr1_ref[...] * pre)
        h2 = jnp.sin(fr2_ref[...] * (jnp.dot(w2t_ref[...], h1, precision=hi, preferred_element_type=F32)
                                     + b2_ref[...]))
        hid_ref[...] = h2

    h2 = hid_ref[...]
    hf = jnp.dot(w3f_ref[...], h2, precision=hi, preferred_element_type=F32)
    hb = jnp.dot(w3b_ref[...], h2, precision=hi, preferred_element_type=F32)
    decay = jnp.exp(-t * deltas_ref[...])
    valid = pos <= float(n_seq - 1)
    kun = jnp.where(valid, jnp.where(d >= 0, hf, hb) * decay, 0.0)
    nrm = lax.rsqrt(jnp.sum(kun * kun, axis=1, keepdims=True) + RMS_EPS)
    kt_ref[...] = kun * nrm


def _hyena_filter(fw1, fb1, ffr1, fw2, fb2, ffr2, fw3, *, s_real):
    n_seq = s_real + N_META
    lk = 2 * s_real + CONV_BLK
    cf = 64
    col = lambda a: a.reshape(-1, 1)
    w1t = fw1.T
    bands = jnp.linspace(1e-4, FILT_BANDS - 1, FILT_BANDS, dtype=F32).reshape(-1, 1)
    max_decay = math.log(DECAY_TARGET) / FAST_DECAY_PCT
    min_decay = math.log(DECAY_TARGET) / SLOW_DECAY_PCT
    deltas = jnp.abs(jnp.linspace(min_decay, max_decay, HY_WIDTH, dtype=F32)).reshape(-1, 1)
    w3t = fw3.T
    h = FILT_HIDDEN
    return pl.pallas_call(
        functools.partial(_filt_kernel, n_seq=n_seq, s_real=s_real),
        grid=(HY_WIDTH // cf,),
        in_specs=[_const_spec((h, 1)), _const_spec((h, FILT_BANDS)), _const_spec((h, FILT_BANDS)),
                  _const_spec((h, 1)), _const_spec((h, 1)), _const_spec((h, h)), _const_spec((h, 1)),
                  _const_spec((h, 1)),
                  pl.BlockSpec((cf, h), lambda i: (i, 0)), pl.BlockSpec((cf, h), lambda i: (i, 0)),
                  _const_spec((FILT_BANDS, 1)), pl.BlockSpec((cf, 1), lambda i: (i, 0))],
        out_specs=pl.BlockSpec((cf, lk), lambda i: (i, 0)),
        out_shape=jax.ShapeDtypeStruct((HY_WIDTH, lk), F32),
        scratch_shapes=[pltpu.VMEM((h, lk), F32)],
        compiler_params=_cparams(("arbitrary",), 48),
        name="hyena_filter",
    )(w1t[:, 0:1], w1t[:, 1:1 + FILT_BANDS], w1t[:, 1 + FILT_BANDS:], col(fb1), col(ffr1), fw2.T, col(fb2),
      col(ffr2), w3t[:HY_WIDTH], w3t[HY_WIDTH:], bands, deltas)


def _hy_pre_kernel(x0_ref, x1_ref, v_ref, p0_ref, p1_ref, pv_ref, n0_ref, n1_ref, nv_ref,
                   scw_ref, scb_ref, skip_ref, zt_ref, a_ref, bt_ref, zs_ref):
    jt = pl.program_id(1)
    last = pl.num_programs(1) - 1
    nb_batch = x0_ref.shape[0]
    row = lax.broadcasted_iota(jnp.int32, (CONV_BLK, LANES), 0)
    keep_next = (jt < last).astype(F32)
    z_keep = jnp.logical_or(jt > 0, row >= CONV_BLK - N_META)

    def short_conv(cur_ref, prev_ref, next_ref, comp, b):
        cur = cur_ref[b].astype(F32)
        prev = prev_ref[b, 15:16, :].astype(F32)
        nxt = next_ref[b, 0:1, :].astype(F32) * keep_next
        up = jnp.where(row == 0, prev, pltpu.roll(cur, 1, 0))
        dn = jnp.where(row == CONV_BLK - 1, nxt, pltpu.roll(cur, CONV_BLK - 1, 0))
        w = scw_ref[comp]
        return up * w[0:1] + cur * w[1:2] + dn * w[2:3] + scb_ref[comp]

    def body(b, carry):
        x0c = short_conv(x0_ref, p0_ref, n0_ref, 0, b)
        x1c = short_conv(x1_ref, p1_ref, n1_ref, 1, b)
        vc = short_conv(v_ref, pv_ref, nv_ref, 2, b)
        z = jnp.where(z_keep, x1c * vc, 0.0)
        zs_ref[b] = z.T
        a_ref[b] = x0c.astype(BF16)
        bt_ref[b] = (x0c * z * skip_ref[...]).astype(BF16)
        return carry

    lax.fori_loop(0, nb_batch, body, 0)
    zt_ref[...] = pltpu.einshape("bcl->cbl", zs_ref[...]).astype(BF16)


def _hy_pre(hy_ext, scw, scb, skip, *, s_real):
    B = hy_ext.shape[0]
    nb = s_real // CONV_BLK
    ncb = HY_WIDTH // LANES
    sub = 16
    r16 = CONV_BLK // sub
    last16 = (s_real + HY_PAD) // sub - 1
    cur = lambda comp: pl.BlockSpec((B, CONV_BLK, LANES), lambda cb, jt: (0, jt + 1, comp * ncb + cb))
    prv = lambda comp: pl.BlockSpec((B, sub, LANES), lambda cb, jt: (0, r16 * (jt + 1) - 1, comp * ncb + cb))
    nxt = lambda comp: pl.BlockSpec(
        (B, sub, LANES), lambda cb, jt: (0, jnp.minimum(r16 * (jt + 2), last16), comp * ncb + cb))
    nat = pl.BlockSpec((B, CONV_BLK, LANES), lambda cb, jt: (0, jnp.maximum(jt - 1, 0), cb))
    scw_r = scw.reshape(HY_SHORT, 3, ncb, LANES).transpose(2, 1, 0, 3)
    scb_r = scb.reshape(3, ncb, 1, LANES).transpose(1, 0, 2, 3)
    skip_r = skip.reshape(ncb, 1, LANES)
    return pl.pallas_call(
        _hy_pre_kernel,
        grid=(ncb, nb + 1),
        in_specs=[cur(0), cur(1), cur(2), prv(0), prv(1), prv(2), nxt(0), nxt(1), nxt(2),
                  pl.BlockSpec((None, 3, HY_SHORT, LANES), lambda cb, jt: (cb, 0, 0, 0)),
                  pl.BlockSpec((None, 3, 1, LANES), lambda cb, jt: (cb, 0, 0, 0)),
                  pl.BlockSpec((None, 1, LANES), lambda cb, jt: (cb, 0, 0))],
        out_specs=[pl.BlockSpec((LANES, None, B, CONV_BLK), lambda cb, jt: (cb, jt, 0, 0)), nat, nat],
        out_shape=[jax.ShapeDtypeStruct((HY_WIDTH, nb + 1, B, CONV_BLK), BF16),
                   jax.ShapeDtypeStruct((B, s_real, HY_WIDTH), BF16),
                   jax.ShapeDtypeStruct((B, s_real, HY_WIDTH), BF16)],
        scratch_shapes=[pltpu.VMEM((B, LANES, CONV_BLK), F32)],
        compiler_params=_cparams(("parallel", "arbitrary"), 48),
        name="hyena_pre",
    )(hy_ext, hy_ext, hy_ext, hy_ext, hy_ext, hy_ext, hy_ext, hy_ext, hy_ext, scw_r, scb_r, skip_r)


def _hy_conv_kernel(kt_ref, z_ref, y_ref, g_ref, *, nb):
    cbk = z_ref.shape[0]
    B = z_ref.shape[2]
    s_real = nb * CONV_BLK
    nq = 2 * s_real // LANES + 1
    upper = (lax.broadcasted_iota(jnp.int32, (LANES, LANES), 1)
             >= lax.broadcasted_iota(jnp.int32, (LANES, LANES), 0))

    def channel(c, carry):
        kt_row = kt_ref[pl.ds(c, 1), :]
        for q in range(nq):
            off = 2 * s_real - LANES * q
            skew = lambda lo: pltpu.roll(jnp.broadcast_to(kt_row[:, lo:lo + LANES], (LANES, LANES)),
                                         0, 1, stride=1, stride_axis=0)
            blk_g = jnp.where(upper, skew(off + LANES), skew(off))
            g_ref[LANES * q:LANES * (q + 1), :] = blk_g.astype(BF16)
        y_ref[c] = jnp.zeros(y_ref.shape[1:], F32)
        for delta in range(-(nb - 1), nb + 1):
            base = CONV_BLK * (nb - delta)
            tile = jnp.concatenate([g_ref[base + LANES:base + LANES + CONV_BLK, :],
                                    g_ref[base:base + CONV_BLK, :]], axis=1)
            jb_lo = max(0, 1 - delta)
            n = min(nb, nb - delta) - jb_lo + 1
            o_lo = jb_lo + delta - 1
            lhs = z_ref[c, jb_lo:jb_lo + n].reshape(n * B, CONV_BLK)
            y_ref[c, o_lo:o_lo + n] += jnp.dot(lhs, tile, preferred_element_type=F32).reshape(n, B, CONV_BLK)
        return carry

    lax.fori_loop(0, cbk, channel, 0)


def _hy_conv(kt, zt, *, cbk=8):
    C, nb1, B, _ = zt.shape
    nb = nb1 - 1
    lk = kt.shape[1]
    return pl.pallas_call(
        functools.partial(_hy_conv_kernel, nb=nb),
        grid=(C // cbk,),
        in_specs=[pl.BlockSpec((cbk, lk), lambda i: (i, 0)),
                  pl.BlockSpec((cbk, nb1, B, CONV_BLK), lambda i: (i, 0, 0, 0))],
        out_specs=pl.BlockSpec((cbk, nb, B, CONV_BLK), lambda i: (i, 0, 0, 0)),
        out_shape=jax.ShapeDtypeStruct((C, nb, B, CONV_BLK), F32),
        scratch_shapes=[pltpu.VMEM((2 * nb * CONV_BLK + LANES, LANES), BF16)],
        compiler_params=_cparams(("parallel",), 48),
        name="hyena_conv",
    )(kt, zt)


def _hy_post_kernel(yt_ref, a_ref, bt_ref, o_ref, ys_ref):
    ys_ref[...] = pltpu.einshape("cbl->bcl", yt_ref[...])

    def body(b, carry):
        y = ys_ref[b].T
        o_ref[b] = (a_ref[b].astype(F32) * y + bt_ref[b].astype(F32)).astype(BF16)
        return carry

    lax.fori_loop(0, a_ref.shape[0], body, 0)


def _hy_post(yt, a, bt):
    C, nb, B, _ = yt.shape
    nat = pl.BlockSpec((B, CONV_BLK, LANES), lambda cb, ib: (0, ib, cb))
    return pl.pallas_call(
        _hy_post_kernel,
        grid=(C // LANES, nb),
        in_specs=[pl.BlockSpec((LANES, None, B, CONV_BLK), lambda cb, ib: (cb, ib, 0, 0)), nat, nat],
        out_specs=nat,
        out_shape=jax.ShapeDtypeStruct(a.shape, BF16),
        scratch_shapes=[pltpu.VMEM((B, LANES, CONV_BLK), F32)],
        compiler_params=_cparams(("parallel", "parallel"), 48),
        name="hyena_post",
    )(yt, a, bt)


def _attn_kernel(qt_ref, k_ref, vt_ref, km_ref, vtm_ref, bias_ref, bmeta_ref, lam_ref, sg_ref,
                 o_ref, m_ref, l_ref, acc_ref, s_ref, bm_ref, *, blk):
    i = pl.program_id(2)
    nk = k_ref.shape[0] // blk
    qt = qt_ref[...]
    rowi = lax.broadcasted_iota(jnp.int32, qt.shape, 0)
    zero = jnp.zeros_like(qt)
    qmaps = (jnp.where(rowi < ATT_HEAD_DIM, qt, zero), jnp.where(rowi >= ATT_HEAD_DIM, qt, zero))

    kmeta, vtmeta, bias_m = km_ref[...], vtm_ref[...], bmeta_ref[jnp.minimum(i, 1)]
    for c in range(2):
        s = jnp.dot(kmeta, qmaps[c], preferred_element_type=F32) + bias_m
        m0 = jnp.max(s, axis=0, keepdims=True)
        p = jnp.exp2(s - m0)
        m_ref[c] = m0
        l_ref[c] = jnp.sum(p, axis=0, keepdims=True)
        acc_ref[c] = jnp.dot(vtmeta, p.astype(BF16), preferred_element_type=F32)

    def scores(j, slot):
        kblk = k_ref[j * blk:(j + 1) * blk, :]
        bias = bias_ref[jnp.clip(j - i, -2, 2) + 2]
        for c in range(2):
            s = jnp.dot(kblk, qmaps[c], preferred_element_type=F32) + bias
            s_ref[slot, c] = s
            bm_ref[slot, c] = jnp.max(s, axis=0, keepdims=True)

    def consume(j, slot):
        vtblk = vt_ref[:, j * blk:(j + 1) * blk]
        for c in range(2):
            m_old = m_ref[c]
            m_new = jnp.maximum(m_old, bm_ref[slot, c])
            alpha = jnp.exp2(m_old - m_new)
            p = jnp.exp2(s_ref[slot, c] - m_new)
            l_ref[c] = alpha * l_ref[c] + jnp.sum(p, axis=0, keepdims=True)
            acc_ref[c] = alpha * acc_ref[c] + jnp.dot(vtblk, p.astype(BF16), preferred_element_type=F32)
            m_ref[c] = m_new

    scores(0, 0)
    for j in range(nk):
        if j + 1 < nk:
            scores(j + 1, (j + 1) % 2)
        consume(j, j % 2)

    lq = lam_ref[...]
    lam = (jnp.exp(jnp.sum(lq[0:1] * lq[1:2], axis=1, keepdims=True))
           - jnp.exp(jnp.sum(lq[2:3] * lq[3:4], axis=1, keepdims=True)) + LAM_INIT)
    ot = acc_ref[0] / l_ref[0] - lam * (acc_ref[1] / l_ref[1])
    ot = ot * lax.rsqrt(jnp.mean(ot * ot, axis=0, keepdims=True) + RMS_EPS) * sg_ref[...] * (1.0 - LAM_INIT)
    o_ref[...] = ot.T.astype(BF16)


def _t5_bucket(rel):
    nb = N_BUCKETS // 2
    max_exact = nb // 2
    ret = jnp.where(rel > 0, nb, 0)
    n = jnp.abs(rel)
    nf = jnp.maximum(n, 1).astype(F32)
    large = max_exact + (jnp.log(nf / max_exact) / math.log(MAX_DISTANCE / max_exact)
                         * (nb - max_exact)).astype(jnp.int32)
    large = jnp.minimum(large, nb - 1)
    return ret + jnp.where(n < max_exact, n, large)


def _attn_bias_tables(rel_bias, blk):
    assert blk >= MAX_DISTANCE
    table = rel_bias.astype(F32) * LOG2E

    def bias_of(rel):
        bucket = _t5_bucket(rel)
        sel = bucket[None] == jnp.arange(N_BUCKETS, dtype=jnp.int32).reshape((N_BUCKETS,) + (1,) * rel.ndim)
        return jnp.stack([jnp.sum(jnp.where(sel, table[:, h].reshape((N_BUCKETS,) + (1,) * rel.ndim), 0.0), axis=0)
                          for h in range(ATT_HEADS)])

    r = jnp.arange(blk, dtype=jnp.int32)
    rel = jnp.stack([dj * blk + r[:, None] - r[None, :] for dj in (-2, -1, 0, 1, 2)])
    tiles = bias_of(rel)
    m = jnp.arange(N_META, dtype=jnp.int32)
    rel_m0 = m[:, None] - (N_META + r[None, :])
    rel_m1 = rel_m0 - blk
    bmeta = bias_of(jnp.stack([rel_m0, rel_m1]))
    return tiles, bmeta


def _attention(qt, k, vt, kmeta, vtmeta, tiles, bmeta, lam, subln_col, *, blk):
    B, S, _ = k.shape
    return pl.pallas_call(
        functools.partial(_attn_kernel, blk=blk),
        grid=(B, ATT_HEADS, S // blk),
        in_specs=[pl.BlockSpec((None, ATT_V_DIM, blk), lambda b, h, i: (b, h, i)),
                  pl.BlockSpec((None, S, ATT_V_DIM), lambda b, h, i: (b, 0, h)),
                  pl.BlockSpec((None, ATT_V_DIM, S), lambda b, h, i: (b, h, 0)),
                  pl.BlockSpec((N_META, ATT_V_DIM), lambda b, h, i: (0, h)),
                  pl.BlockSpec((ATT_V_DIM, N_META), lambda b, h, i: (h, 0)),
                  pl.BlockSpec((None, 5, blk, blk), lambda b, h, i: (h, 0, 0, 0)),
                  pl.BlockSpec((None, 2, N_META, blk), lambda b, h, i: (h, 0, 0, 0)),
                  pl.BlockSpec((4, ATT_HEAD_DIM), lambda b, h, i: (0, 0)),
                  pl.BlockSpec((ATT_V_DIM, 1), lambda b, h, i: (0, 0))],
        out_specs=pl.BlockSpec((None, blk, ATT_V_DIM), lambda b, h, i: (b, i, h)),
        scratch_shapes=[pltpu.VMEM((2, 1, blk), F32), pltpu.VMEM((2, 1, blk), F32),
                        pltpu.VMEM((2, ATT_V_DIM, blk), F32),
                        pltpu.VMEM((2, 2, blk, blk), F32), pltpu.VMEM((2, 2, 1, blk), F32)],
        out_shape=jax.ShapeDtypeStruct((B, S, ATT_WIDTH), BF16),
        compiler_params=_cparams(("parallel", "parallel", "arbitrary"), 56),
        name="diff_attention",
    )(qt, k, vt, kmeta, vtmeta, tiles, bmeta, lam, subln_col)


def _merge_kernel(x_ref, ya_ref, yb_ref, sg_ref, pa_ref, pb_ref, wo_ref, g2_ref, wrh_ref, wrl_ref, br_ref,
                  tri_ref, x1_ref, h2_ref, ti_ref, tg_ref, rk_ref, cnt_ref):
    ma = jnp.dot(ya_ref[...], pa_ref[...], preferred_element_type=F32)
    mb = jnp.dot(yb_ref[...], pb_ref[...], preferred_element_type=F32)
    m = sg_ref[:, :D_MODEL].astype(F32) * ma + sg_ref[:, D_MODEL:].astype(F32) * mb
    x1 = x_ref[...] + jnp.dot(m.astype(BF16), wo_ref[...], preferred_element_type=F32)
    x1_ref[...] = x1
    h2 = x1 * lax.rsqrt(jnp.mean(x1 * x1, axis=-1, keepdims=True) + RMS_EPS) * g2_ref[...]
    h2_hi = h2.astype(BF16)
    h2_ref[...] = _pack_rows(h2)
    h2_lo = (h2 - h2_hi.astype(F32)).astype(BF16)
    wrh = wrh_ref[...]
    logits = (jnp.dot(h2_hi, wrh, preferred_element_type=F32) + jnp.dot(h2_lo, wrh, preferred_element_type=F32)
              + jnp.dot(h2_hi, wrl_ref[...], preferred_element_type=F32) + br_ref[...])
    lane = lax.broadcasted_iota(jnp.int32, logits.shape, 1)
    slot = lax.broadcasted_iota(jnp.int32, ti_ref.shape, 1)
    top_i = jnp.zeros(ti_ref.shape, jnp.int32)
    top_v = jnp.zeros(tg_ref.shape, F32)
    work = logits
    picks = []
    for kk in range(TOP_K):
        mx = jnp.max(work, axis=1, keepdims=True)
        idx = jnp.min(jnp.where(work == mx, lane, N_EXPERTS), axis=1, keepdims=True)
        pick = lane == idx
        picks.append(pick)
        top_i = jnp.where(slot == kk, idx, top_i)
        top_v = jnp.where(slot == kk, mx, top_v)
        work = jnp.where(pick, -jnp.inf, work)
    ex = jnp.exp(top_v - jnp.max(top_v, axis=1, keepdims=True))
    ti_ref[...] = top_i
    tg_ref[...] = ex / jnp.sum(ex, axis=1, keepdims=True)

    @pl.when(pl.program_id(0) == 0)
    def _():
        cnt_ref[...] = jnp.zeros_like(cnt_ref)

    chosen = jnp.where(work == -jnp.inf, 1.0, 0.0)
    before = cnt_ref[...] + jnp.dot(tri_ref[...], chosen.astype(BF16), preferred_element_type=F32)
    rank = jnp.zeros(rk_ref.shape, F32)
    for kk in range(TOP_K):
        rk = jnp.sum(jnp.where(picks[kk], before, 0.0), axis=1, keepdims=True)
        rank = jnp.where(slot == kk, rk, rank)
    rk_ref[...] = rank.astype(jnp.int32)
    cnt_ref[...] += jnp.sum(chosen, axis=0, keepdims=True)


def _merge(x, ya, yb, sg, pa, pb, wo, g2, wr_hi, wr_lo, br, *, tm):
    T, D = x.shape
    tok = lambda w: pl.BlockSpec((tm, w), lambda i: (i, 0))
    tri = jnp.tril(jnp.ones((tm, tm), F32), -1).astype(BF16)
    return pl.pallas_call(
        _merge_kernel,
        grid=(T // tm,),
        in_specs=[tok(D), tok(HY_WIDTH), tok(ATT_WIDTH), tok(COL_G), _const_spec((HY_WIDTH, D)),
                  _const_spec((ATT_WIDTH, D)), _const_spec((D, D)), _const_spec((1, D)),
                  _const_spec((D, N_EXPERTS)), _const_spec((D, N_EXPERTS)), _const_spec((1, N_EXPERTS)),
                  _const_spec((tm, tm))],
        out_specs=[tok(D), tok(D // 2), tok(TOP_K), tok(TOP_K), tok(TOP_K),
                   pl.BlockSpec((1, N_EXPERTS), lambda i: (0, 0))],
        out_shape=[jax.ShapeDtypeStruct((T, D), F32), jax.ShapeDtypeStruct((T, D // 2), jnp.uint32),
                   jax.ShapeDtypeStruct((T, TOP_K), jnp.int32), jax.ShapeDtypeStruct((T, TOP_K), F32),
                   jax.ShapeDtypeStruct((T, TOP_K), jnp.int32), jax.ShapeDtypeStruct((1, N_EXPERTS), F32)],
        compiler_params=_cparams(("arbitrary",), 48),
        name="merge_router",
    )(x, ya, yb, sg, pa, pb, wo, g2, wr_hi, wr_lo, br, tri)


def _moe_kernel(iblk_ref, iexp_ref, start_ref, count_ref, nitem_ref, xs_ref, w1_ref, b1_ref, w2_ref, b2_ref, ys_ref):
    i = pl.program_id(0)
    tm = xs_ref.shape[0]

    @pl.when(i < nitem_ref[0])
    def _():
        blk = iblk_ref[i]
        e = iexp_ref[i]
        x = jnp.concatenate(_unpack_rows(xs_ref[...]), axis=1).astype(BF16)
        u = jnp.dot(x, w1_ref[...], preferred_element_type=F32) + b1_ref[...]
        glu = jnp.minimum(u[:, :D_FF], SWIGLU_LIMIT)
        lin = jnp.clip(u[:, D_FF:], -SWIGLU_LIMIT, SWIGLU_LIMIT)
        a = glu * jax.nn.sigmoid(SWIGLU_ALPHA * glu) * (lin + 1.0)
        y = _pack_rows(jnp.dot(a.astype(BF16), w2_ref[...], preferred_element_type=F32) + b2_ref[...])
        row = lax.broadcasted_iota(jnp.int32, (tm, 1), 0) + blk * tm
        mine = jnp.logical_and(row >= start_ref[e], row < start_ref[e] + count_ref[e])
        first_visit = jnp.logical_or(i == 0, iblk_ref[jnp.maximum(i - 1, 0)] != blk)

        @pl.when(first_visit)
        def _():
            ys_ref[...] = jnp.where(mine, y, jnp.zeros_like(y))

        @pl.when(jnp.logical_not(first_visit))
        def _():
            ys_ref[...] = jnp.where(mine, y, ys_ref[...])


def _moe_experts(item_blk, item_exp, start, count, n_items, xs, w1, b1, w2, b2, *, tm):
    A, D = xs.shape[0], xs.shape[1] * 2
    grid_spec = pltpu.PrefetchScalarGridSpec(
        num_scalar_prefetch=5,
        grid=(item_blk.shape[0],),
        in_specs=[pl.BlockSpec((tm, D // 2), lambda i, ib, ie, *_: (ib[i], 0)),
                  pl.BlockSpec((None, D, 2 * D_FF), lambda i, ib, ie, *_: (ie[i], 0, 0)),
                  pl.BlockSpec((None, 1, 2 * D_FF), lambda i, ib, ie, *_: (ie[i], 0, 0)),
                  pl.BlockSpec((None, D_FF, D), lambda i, ib, ie, *_: (ie[i], 0, 0)),
                  pl.BlockSpec((None, 1, D), lambda i, ib, ie, *_: (ie[i], 0, 0))],
        out_specs=pl.BlockSpec((tm, D // 2), lambda i, ib, ie, *_: (ib[i], 0)),
    )
    return pl.pallas_call(
        _moe_kernel,
        grid_spec=grid_spec,
        out_shape=jax.ShapeDtypeStruct((A, D // 2), jnp.uint32),
        compiler_params=_cparams(("arbitrary",), 48),
        name="moe_experts",
    )(item_blk, item_exp, start, count, n_items, xs, w1, b1, w2, b2)


def _combine_kernel(x1_ref, yg_ref, tg_ref, o_ref):
    half = D_MODEL // 2
    acc_lo = x1_ref[:, :half]
    acc_hi = x1_ref[:, half:]
    g = tg_ref[...]
    for kk in range(TOP_K):
        lo, hi = _unpack_rows(yg_ref[kk])
        acc_lo = acc_lo + g[:, kk:kk + 1] * lo
        acc_hi = acc_hi + g[:, kk:kk + 1] * hi
    o_ref[:, :half] = acc_lo
    o_ref[:, half:] = acc_hi


def _combine(x1, yg, tg, *, tm):
    T, D = x1.shape
    return pl.pallas_call(
        _combine_kernel,
        grid=(T // tm,),
        in_specs=[pl.BlockSpec((tm, D), lambda i: (i, 0)), pl.BlockSpec((TOP_K, tm, D // 2), lambda i: (0, i, 0)),
                  pl.BlockSpec((tm, TOP_K), lambda i: (i, 0))],
        out_specs=pl.BlockSpec((tm, D), lambda i: (i, 0)),
        out_shape=jax.ShapeDtypeStruct((T, D), F32),
        compiler_params=_cparams(("parallel",), 48),
        name="moe_combine",
    )(x1, yg, tg)


def _lookup(table, idx):
    n = table.shape[0]
    sel = idx[None] == jnp.arange(n, dtype=jnp.int32).reshape((n,) + (1,) * idx.ndim)
    return jnp.sum(jnp.where(sel, table.reshape((n,) + (1,) * idx.ndim), 0), axis=0)


def _route(top_i, rank, counts, *, tm):
    T = top_i.shape[0]
    A = T * TOP_K
    counts = counts.reshape(N_EXPERTS).astype(jnp.int32)
    start = jnp.cumsum(counts) - counts
    flat_e = top_i.reshape(A // LANES, LANES)
    slot_of = (_lookup(start, flat_e) + rank.reshape(A // LANES, LANES)).reshape(T, TOP_K)
    slot_tok = (jnp.argsort(top_i.reshape(A)) // TOP_K).astype(jnp.int32)
    first_blk = start // tm
    n_it = jnp.where(counts > 0, (start + counts - 1) // tm - first_blk + 1, 0)
    it_end = jnp.cumsum(n_it)
    n_items = it_end[-1]
    n_max = A // tm + N_EXPERTS
    i = jnp.minimum(jnp.arange(n_max, dtype=jnp.int32), n_items - 1)
    item_exp = jnp.sum(it_end[None, :] <= i[:, None], axis=1, dtype=jnp.int32)
    item_blk = _lookup(first_blk, item_exp) + i - _lookup(it_end - n_it, item_exp)
    return slot_tok, slot_of, item_blk.astype(jnp.int32), item_exp, start, counts, n_items.reshape(1)


def _trunk(x, meta_proj, w, tabs, *, tm_prep, tm_tok, tm_moe, attn_blk):
    B, S, D = x.shape
    T = B * S
    hy_m, _, k_m, vt_m, _ = meta_proj
    assert tm_prep == HY_PAD
    front = jnp.concatenate([jnp.zeros((HY_PAD - N_META, COL_HY), BF16), hy_m[0]], axis=0)
    hy_ext, qt, k, vt, sg = _prep(x, front, w['g1'], w['w_in'], w['wqt'], w['wvt'], w['bd'], w['qg'], w['kg'],
                                  tm=tm_prep)
    kt = _hyena_filter(w['fw1'], w['fb1'], w['ffr1'], w['fw2'], w['fb2'], w['ffr2'], w['fw3'], s_real=S)
    zt, a, bt = _hy_pre(hy_ext, w['scw'], w['scb'], w['skip'], s_real=S)
    ya = _hy_post(_hy_conv(kt, zt), a, bt)
    tiles, bmeta = tabs
    yb = _attention(qt, k, vt, k_m[0], vt_m[0], tiles, bmeta, w['lam'], w['subln'], blk=attn_blk)
    x1, h2, top_i, top_g, rank, counts = _merge(
        x.reshape(T, D), ya.reshape(T, HY_WIDTH), yb.reshape(T, ATT_WIDTH), sg.reshape(T, COL_G),
        w['pa'], w['pb'], w['wo'], w['g2'], w['wr_hi'], w['wr_lo'], w['br'], tm=tm_tok)
    slot_tok, slot_of, item_blk, item_exp, start, cnt, n_items = _route(top_i, rank, counts, tm=tm_moe)
    xs = _sc_gather_rows(h2, slot_tok)
    ys = _moe_experts(item_blk, item_exp, start, cnt, n_items, xs, w['w1'], w['b1'], w['w2'], w['b2'], tm=tm_moe)
    yg = _sc_gather_rows(ys, slot_of.T.reshape(T * TOP_K)).reshape(TOP_K, T, D // 2)
    return _combine(x1, yg, top_g, tm=tm_tok).reshape(B, S, D)


def kernel(x_prompt, x_sample, meta_tokens, rel_bias, norm1_g, w_in, short_conv_w, short_conv_b, filt_w1, filt_b1,
           filt_freq1, filt_w2, filt_b2, filt_freq2, filt_w3, hy_skip, proj_a, q_norm_g, k_norm_g, lambda_q1,
           lambda_k1, lambda_q2, lambda_k2, subln_g, proj_b, w_out, norm2_g, w_router, b_router, w_mlp1, b_mlp1,
           w_mlp2, b_mlp2):
    l = 0
    n_hm = COL_Q // ATT_HEAD_DIM
    w_in_bf = w_in[l].astype(BF16)
    c_q, c_v = COL_HY, COL_HY + COL_Q + COL_K
    wr_hi = w_router[l].astype(BF16)
    w = {
        'g1': norm1_g[l][None], 'w_in': w_in_bf,
        'wqt': w_in_bf[:, c_q:c_q + COL_Q].T, 'wvt': w_in_bf[:, c_v:c_v + COL_V].T,
        'bd': jnp.kron(jnp.eye(n_hm, dtype=F32), jnp.full((ATT_HEAD_DIM, ATT_HEAD_DIM), 1.0 / ATT_HEAD_DIM, F32)
                       ).astype(BF16),
        'qg': (jnp.tile(q_norm_g[l], n_hm) * (ATT_HEAD_DIM ** -0.5 * LOG2E))[:, None],
        'kg': jnp.tile(k_norm_g[l], n_hm)[None],
        'wr_hi': wr_hi, 'wr_lo': (w_router[l] - wr_hi.astype(F32)).astype(BF16),
        'fw1': filt_w1[l], 'fb1': filt_b1[l], 'ffr1': filt_freq1[l], 'fw2': filt_w2[l], 'fb2': filt_b2[l],
        'ffr2': filt_freq2[l], 'fw3': filt_w3[l],
        'scw': short_conv_w[l], 'scb': short_conv_b[l], 'skip': hy_skip[l],
        'lam': jnp.stack([lambda_q1[l], lambda_k1[l], lambda_q2[l], lambda_k2[l]]), 'subln': subln_g[l][:, None],
        'pa': proj_a[l].astype(BF16), 'pb': proj_b[l].astype(BF16), 'wo': w_out[l].astype(BF16),
        'g2': norm2_g[l][None], 'br': b_router[l][None],
        'w1': w_mlp1[l].astype(BF16), 'b1': b_mlp1[l][:, None, :], 'w2': w_mlp2[l].astype(BF16),
        'b2': b_mlp2[l][:, None, :],
    }
    meta_proj = _prep(meta_tokens[None], None, w['g1'], w['w_in'], w['wqt'], w['wvt'], w['bd'], w['qg'], w['kg'],
                      tm=N_META)
    tabs = _attn_bias_tables(rel_bias, TILES['attn_blk'])
    y_prompt = _trunk(x_prompt, meta_proj, w, tabs, **TILES)
    y_sample = _trunk(x_sample, meta_proj, w, tabs, **TILES)
    return (y_prompt, y_sample)
```

```python
import functools
import math

import jax
import jax.numpy as jnp
import numpy as np
from jax import lax
from jax.experimental import pallas as pl
from jax.experimental.pallas import tpu as pltpu
from jax.experimental.pallas import tpu_sc as plsc

F32 = jnp.float32
BF16 = jnp.bfloat16

D_MODEL = 1024
N_META = 16
RMS_EPS = 1e-6
HY_WIDTH = 512
HY_SHORT = 3
FILT_BANDS = 16
FILT_EMB = 1 + 2 * FILT_BANDS
FILT_HIDDEN = 64
DECAY_TARGET = 1e-2
FAST_DECAY_PCT = 0.3
SLOW_DECAY_PCT = 1.5
ATT_HEADS = 4
ATT_HEAD_DIM = 64
ATT_V_DIM = 128
ATT_WIDTH = 512
N_BUCKETS = 32
MAX_DISTANCE = 128
N_EXPERTS = 32
TOP_K = 4
D_FF = 1024
SWIGLU_LIMIT = 7.0
SWIGLU_ALPHA = 1.702
COL_HY = 3 * HY_WIDTH
COL_Q = 512
COL_K = 512
COL_V = 512
COL_G = 2 * D_MODEL
IN_COLS = COL_HY + COL_Q + COL_K + COL_V + COL_G
LAM_INIT = 0.8 - 0.6 * math.exp(-0.3 * 0)
LOG2E = 1.4426950408889634

V7X_VMEM_BYTES = 64 * 1024 * 1024
LANES = 128
CONV_BLK = 256
HY_PAD = 512
TILES = dict(tm_prep=512, tm_tok=512, tm_moe=512, attn_blk=512)


def _cparams(sem, vmem_mb):
    return pltpu.CompilerParams(dimension_semantics=sem, vmem_limit_bytes=vmem_mb * 1024 * 1024)


def _const_spec(shape):
    nd = len(shape)
    return pl.BlockSpec(shape, lambda *_: (0,) * nd, pipeline_mode=pl.Buffered(1))


def _pack_rows(x):
    n = x.shape[1] // 2
    lo = pltpu.bitcast(x[:, :n].astype(BF16).astype(F32), jnp.uint32)
    hi = pltpu.bitcast(x[:, n:].astype(BF16).astype(F32), jnp.uint32)
    return (hi & jnp.uint32(0xFFFF0000)) | (lo >> 16)


def _unpack_rows(w):
    lo = pltpu.bitcast(w << 16, F32)
    hi = pltpu.bitcast(w & jnp.uint32(0xFFFF0000), F32)
    return lo, hi


def _sc_gather_rows(table, idx, *, chunk=128):
    info = plsc.get_sparse_core_info()
    n_workers = info.num_cores * info.num_subcores
    n_rows, width = idx.shape[0], table.shape[1]
    per_worker = n_rows // n_workers
    mesh = plsc.VectorSubcoreMesh(core_axis_name="c", subcore_axis_name="s")

    @functools.partial(
        pl.kernel, mesh=mesh, out_type=jax.ShapeDtypeStruct((n_rows, width), table.dtype),
        scratch_types=[pltpu.VMEM((chunk,), jnp.int32), pltpu.VMEM((chunk, width), table.dtype),
                       pltpu.SemaphoreType.DMA])
    def gather(table_hbm, idx_hbm, out_hbm, idx_v, rows_v, sem):
        base = (lax.axis_index("s") * info.num_cores + lax.axis_index("c")) * per_worker

        @pl.loop(0, per_worker // chunk)
        def _(j):
            off = pl.multiple_of(base + j * chunk, chunk)
            pltpu.sync_copy(idx_hbm.at[pl.ds(off, chunk)], idx_v)
            pltpu.async_copy(table_hbm.at[idx_v], rows_v, sem).wait()
            pltpu.sync_copy(rows_v, out_hbm.at[pl.ds(off, chunk)])

    return gather(table, idx)


def _prep_kernel(x_ref, front_ref, g1_ref, w_ref, wqt_ref, wvt_ref, bd_ref, qg_ref, kg_ref,
                 hy_ref, qt_ref, k_ref, vt_ref, sg_ref, *, lead):
    if lead:
        @pl.when(pl.program_id(1) == 0)
        def _():
            hy_ref[...] = front_ref[...]

        @pl.when(pl.program_id(1) > 0)
        def _():
            _prep_tile(x_ref, g1_ref, w_ref, wqt_ref, wvt_ref, bd_ref, qg_ref, kg_ref,
                       hy_ref, qt_ref, k_ref, vt_ref, sg_ref)
    else:
        _prep_tile(x_ref, g1_ref, w_ref, wqt_ref, wvt_ref, bd_ref, qg_ref, kg_ref,
                   hy_ref, qt_ref, k_ref, vt_ref, sg_ref)


def _prep_tile(x_ref, g1_ref, w_ref, wqt_ref, wvt_ref, bd_ref, qg_ref, kg_ref,
               hy_ref, qt_ref, k_ref, vt_ref, sg_ref):
    x = x_ref[...]
    ms = jnp.mean(x * x, axis=-1, keepdims=True)
    h = (x * lax.rsqrt(ms + RMS_EPS) * g1_ref[...]).astype(BF16)
    nt_dims = (((1,), (1,)), ((), ()))

    def proj(lo, hi):
        return jnp.dot(h, w_ref[:, lo:hi], preferred_element_type=F32)

    def proj_t(wt_ref):
        return lax.dot_general(wt_ref[...], h, nt_dims, preferred_element_type=F32)

    c0, c1, c2, c3 = COL_HY, COL_HY + COL_Q, COL_HY + COL_Q + COL_K, COL_HY + COL_Q + COL_K + COL_V
    hy_ref[...] = proj(0, c0).astype(BF16)
    qt = proj_t(wqt_ref)
    msq = jnp.dot(bd_ref[...], (qt * qt).astype(BF16), preferred_element_type=F32)
    qt_ref[...] = (qt * lax.rsqrt(msq + RMS_EPS) * qg_ref[...]).astype(BF16)
    kk = proj(c1, c2)
    msk = jnp.dot((kk * kk).astype(BF16), bd_ref[...], preferred_element_type=F32)
    k_ref[...] = (kk * lax.rsqrt(msk + RMS_EPS) * kg_ref[...]).astype(BF16)
    vt_ref[...] = proj_t(wvt_ref).astype(BF16)
    sg_ref[...] = jax.nn.sigmoid(proj(c3, IN_COLS)).astype(BF16)


def _prep(x, front, g1, w_in_bf, wqt, wvt, bd, qg_col, kg, *, tm):
    B, S, D = x.shape
    nt = S // tm
    lead = 0 if front is None else 1
    if front is None:
        front = jnp.zeros((tm, COL_HY), BF16)
    src = lambda i: jnp.maximum(i - lead, 0)
    tok = lambda w: pl.BlockSpec((None, tm, w), lambda b, i: (b, src(i), 0))
    tok_t = lambda w: pl.BlockSpec((None, w, tm), lambda b, i: (b, 0, src(i)))
    return pl.pallas_call(
        functools.partial(_prep_kernel, lead=lead),
        grid=(B, nt + lead),
        in_specs=[tok(D), _const_spec((tm, COL_HY)), _const_spec((1, D)), _const_spec((D, IN_COLS)),
                  _const_spec((COL_Q, D)), _const_spec((COL_V, D)), _const_spec((COL_Q, COL_Q)),
                  _const_spec((COL_Q, 1)), _const_spec((1, COL_K))],
        out_specs=[pl.BlockSpec((None, tm, COL_HY), lambda b, i: (b, i, 0)),
                   tok_t(COL_Q), tok(COL_K), tok_t(COL_V), tok(COL_G)],
        out_shape=[jax.ShapeDtypeStruct((B, S + lead * tm, COL_HY), BF16),
                   jax.ShapeDtypeStruct((B, COL_Q, S), BF16), jax.ShapeDtypeStruct((B, S, COL_K), BF16),
                   jax.ShapeDtypeStruct((B, COL_V, S), BF16), jax.ShapeDtypeStruct((B, S, COL_G), BF16)],
        compiler_params=_cparams(("parallel", "arbitrary"), 48),
        name="prep",
    )(x, front, g1, w_in_bf, wqt, wvt, bd, qg_col, kg)


def _filt_kernel(w1t_ref, w1c_ref, w1s_ref, b1_ref, fr1_ref, w2t_ref, b2_ref, fr2_ref, w3f_ref, w3b_ref,
                 bands_ref, deltas_ref, kt_ref, hid_ref, *, n_seq, s_real):
    lk = kt_ref.shape[1]
    hi = lax.Precision.HIGHEST
    u = lax.broadcasted_iota(jnp.int32, (1, lk), 1)
    d = u - s_real
    pos = jnp.abs(d).astype(F32)
    t = pos / float(max(n_seq - 1, 1))

    @pl.when(pl.program_id(0) == 0)
    def _():
        w = 2.0 * math.pi * pos / float(n_seq)
        ang = bands_ref[...] * w
        pre = (w1t_ref[...] * t
               + jnp.dot(w1c_ref[...], jnp.cos(ang), precision=hi, preferred_element_type=F32)
               - jnp.dot(w1s_ref[...], jnp.sin(ang), precision=hi, preferred_element_type=F32)
               + b1_ref[...])
        h1 = jnp.sin(fr1_ref[...] * pre)
        h2 = jnp.sin(fr2_ref[...] * (jnp.dot(w2t_ref[...], h1, precision=hi, preferred_element_type=F32)
                                     + b2_ref[...]))
        hid_ref[...] = h2

    h2 = hid_ref[...]
    hf = jnp.dot(w3f_ref[...], h2, precision=hi, preferred_element_type=F32)
    hb = jnp.dot(w3b_ref[...], h2, precision=hi, preferred_element_type=F32)
    decay = jnp.exp(-t * deltas_ref[...])
    valid = pos <= float(n_seq - 1)
    kun = jnp.where(valid, jnp.where(d >= 0, hf, hb) * decay, 0.0)
    nrm = lax.rsqrt(jnp.sum(kun * kun, axis=1, keepdims=True) + RMS_EPS)
    kt_ref[...] = kun * nrm


def _hyena_filter(fw1, fb1, ffr1, fw2, fb2, ffr2, fw3, *, s_real):
    n_seq = s_real + N_META
    lk = 2 * s_real + CONV_BLK
    cf = 64
    col = lambda a: a.reshape(-1, 1)
    w1t = fw1.T
    bands = jnp.linspace(1e-4, FILT_BANDS - 1, FILT_BANDS, dtype=F32).reshape(-1, 1)
    max_decay = math.log(DECAY_TARGET) / FAST_DECAY_PCT
    min_decay = math.log(DECAY_TARGET) / SLOW_DECAY_PCT
    deltas = jnp.abs(jnp.linspace(min_decay, max_decay, HY_WIDTH, dtype=F32)).reshape(-1, 1)
    w3t = fw3.T
    h = FILT_HIDDEN
    return pl.pallas_call(
        functools.partial(_filt_kernel, n_seq=n_seq, s_real=s_real),
        grid=(HY_WIDTH // cf,),
        in_specs=[_const_spec((h, 1)), _const_spec((h, FILT_BANDS)), _const_spec((h, FILT_BANDS)),
                  _const_spec((h, 1)), _const_spec((h, 1)), _const_spec((h, h)), _const_spec((h, 1)),
                  _const_spec((h, 1)),
                  pl.BlockSpec((cf, h), lambda i: (i, 0)), pl.BlockSpec((cf, h), lambda i: (i, 0)),
                  _const_spec((FILT_BANDS, 1)), pl.BlockSpec((cf, 1), lambda i: (i, 0))],
        out_specs=pl.BlockSpec((cf, lk), lambda i: (i, 0)),
        out_shape=jax.ShapeDtypeStruct((HY_WIDTH, lk), F32),
        scratch_shapes=[pltpu.VMEM((h, lk), F32)],
        compiler_params=_cparams(("arbitrary",), 48),
        name="hyena_filter",
    )(w1t[:, 0:1], w1t[:, 1:1 + FILT_BANDS], w1t[:, 1 + FILT_BANDS:], col(fb1), col(ffr1), fw2.T, col(fb2),
      col(ffr2), w3t[:HY_WIDTH], w3t[HY_WIDTH:], bands, deltas)


def _hy_pre_kernel(x0_ref, x1_ref, v_ref, p0_ref, p1_ref, pv_ref, n0_ref, n1_ref, nv_ref,
                   scw_ref, scb_ref, skip_ref, zt_ref, a_ref, bt_ref, zs_ref):
    jt = pl.program_id(1)
    last = pl.num_programs(1) - 1
    nb_batch = x0_ref.shape[0]
    row = lax.broadcasted_iota(jnp.int32, (CONV_BLK, LANES), 0)
    keep_next = (jt < last).astype(F32)
    z_keep = jnp.logical_or(jt > 0, row >= CONV_BLK - N_META)

    def short_conv(cur_ref, prev_ref, next_ref, comp, b):
        cur = cur_ref[b].astype(F32)
        prev = prev_ref[b, 15:16, :].astype(F32)
        nxt = next_ref[b, 0:1, :].astype(F32) * keep_next
        up = jnp.where(row == 0, prev, pltpu.roll(cur, 1, 0))
        dn = jnp.where(row == CONV_BLK - 1, nxt, pltpu.roll(cur, CONV_BLK - 1, 0))
        w = scw_ref[comp]
        return up * w[0:1] + cur * w[1:2] + dn * w[2:3] + scb_ref[comp]

    def body(b, carry):
        x0c = short_conv(x0_ref, p0_ref, n0_ref, 0, b)
        x1c = short_conv(x1_ref, p1_ref, n1_ref, 1, b)
        vc = short_conv(v_ref, pv_ref, nv_ref, 2, b)
        z = jnp.where(z_keep, x1c * vc, 0.0)
        zs_ref[b] = z.T
        a_ref[b] = x0c.astype(BF16)
        bt_ref[b] = (x0c * z * skip_ref[...]).astype(BF16)
        return carry

    lax.fori_loop(0, nb_batch, body, 0)
    zt_ref[...] = pltpu.einshape("bcl->cbl", zs_ref[...]).astype(BF16)


def _hy_pre(hy_ext, scw, scb, skip, *, s_real):
    B = hy_ext.shape[0]
    nb = s_real // CONV_BLK
    ncb = HY_WIDTH // LANES
    sub = 16
    r16 = CONV_BLK // sub
    last16 = (s_real + HY_PAD) // sub - 1
    cur = lambda comp: pl.BlockSpec((B, CONV_BLK, LANES), lambda cb, jt: (0, jt + 1, comp * ncb + cb))
    prv = lambda comp: pl.BlockSpec((B, sub, LANES), lambda cb, jt: (0, r16 * (jt + 1) - 1, comp * ncb + cb))
    nxt = lambda comp: pl.BlockSpec(
        (B, sub, LANES), lambda cb, jt: (0, jnp.minimum(r16 * (jt + 2), last16), comp * ncb + cb))
    nat = pl.BlockSpec((B, CONV_BLK, LANES), lambda cb, jt: (0, jnp.maximum(jt - 1, 0), cb))
    scw_r = scw.reshape(HY_SHORT, 3, ncb, LANES).transpose(2, 1, 0, 3)
    scb_r = scb.reshape(3, ncb, 1, LANES).transpose(1, 0, 2, 3)
    skip_r = skip.reshape(ncb, 1, LANES)
    return pl.pallas_call(
        _hy_pre_kernel,
        grid=(ncb, nb + 1),
        in_specs=[cur(0), cur(1), cur(2), prv(0), prv(1), prv(2), nxt(0), nxt(1), nxt(2),
                  pl.BlockSpec((None, 3, HY_SHORT, LANES), lambda cb, jt: (cb, 0, 0, 0)),
                  pl.BlockSpec((None, 3, 1, LANES), lambda cb, jt: (cb, 0, 0, 0)),
                  pl.BlockSpec((None, 1, LANES), lambda cb, jt: (cb, 0, 0))],
        out_specs=[pl.BlockSpec((LANES, None, B, CONV_BLK), lambda cb, jt: (cb, jt, 0, 0)), nat, nat],
        out_shape=[jax.ShapeDtypeStruct((HY_WIDTH, nb + 1, B, CONV_BLK), BF16),
                   jax.ShapeDtypeStruct((B, s_real, HY_WIDTH), BF16),
                   jax.ShapeDtypeStruct((B, s_real, HY_WIDTH), BF16)],
        scratch_shapes=[pltpu.VMEM((B, LANES, CONV_BLK), F32)],
        compiler_params=_cparams(("parallel", "arbitrary"), 48),
        name="hyena_pre",
    )(hy_ext, hy_ext, hy_ext, hy_ext, hy_ext, hy_ext, hy_ext, hy_ext, hy_ext, scw_r, scb_r, skip_r)


def _hy_conv_kernel(kt_ref, z_ref, y_ref, g_ref, *, nb):
    cbk = z_ref.shape[0]
    B = z_ref.shape[2]
    s_real = nb * CONV_BLK
    nq = 2 * s_real // LANES + 1
    upper = (lax.broadcasted_iota(jnp.int32, (LANES, LANES), 1)
             >= lax.broadcasted_iota(jnp.int32, (LANES, LANES), 0))

    def channel(c, carry):
        kt_row = kt_ref[pl.ds(c, 1), :]
        for q in range(nq):
            off = 2 * s_real - LANES * q
            skew = lambda lo: pltpu.roll(jnp.broadcast_to(kt_row[:, lo:lo + LANES], (LANES, LANES)),
                                         0, 1, stride=1, stride_axis=0)
            blk_g = jnp.where(upper, skew(off + LANES), skew(off))
            g_ref[LANES * q:LANES * (q + 1), :] = blk_g.astype(BF16)
        y_ref[c] = jnp.zeros(y_ref.shape[1:], F32)
        for delta in range(-(nb - 1), nb + 1):
            base = CONV_BLK * (nb - delta)
            tile = jnp.concatenate([g_ref[base + LANES:base + LANES + CONV_BLK, :],
                                    g_ref[base:base + CONV_BLK, :]], axis=1)
            jb_lo = max(0, 1 - delta)
            n = min(nb, nb - delta) - jb_lo + 1
            o_lo = jb_lo + delta - 1
            lhs = z_ref[c, jb_lo:jb_lo + n].reshape(n * B, CONV_BLK)
            y_ref[c, o_lo:o_lo + n] += jnp.dot(lhs, tile, preferred_element_type=F32).reshape(n, B, CONV_BLK)
        return carry

    lax.fori_loop(0, cbk, channel, 0)


def _hy_conv(kt, zt, *, cbk=8):
    C, nb1, B, _ = zt.shape
    nb = nb1 - 1
    lk = kt.shape[1]
    return pl.pallas_call(
        functools.partial(_hy_conv_kernel, nb=nb),
        grid=(C // cbk,),
        in_specs=[pl.BlockSpec((cbk, lk), lambda i: (i, 0)),
                  pl.BlockSpec((cbk, nb1, B, CONV_BLK), lambda i: (i, 0, 0, 0))],
        out_specs=pl.BlockSpec((cbk, nb, B, CONV_BLK), lambda i: (i, 0, 0, 0)),
        out_shape=jax.ShapeDtypeStruct((C, nb, B, CONV_BLK), F32),
        scratch_shapes=[pltpu.VMEM((2 * nb * CONV_BLK + LANES, LANES), BF16)],
        compiler_params=_cparams(("parallel",), 48),
        name="hyena_conv",
    )(kt, zt)


def _hy_post_kernel(yt_ref, a_ref, bt_ref, o_ref, ys_ref):
    ys_ref[...] = pltpu.einshape("cbl->bcl", yt_ref[...])

    def body(b, carry):
        y = ys_ref[b].T
        o_ref[b] = (a_ref[b].astype(F32) * y + bt_ref[b].astype(F32)).astype(BF16)
        return carry

    lax.fori_loop(0, a_ref.shape[0], body, 0)


def _hy_post(yt, a, bt):
    C, nb, B, _ = yt.shape
    nat = pl.BlockSpec((B, CONV_BLK, LANES), lambda cb, ib: (0, ib, cb))
    return pl.pallas_call(
        _hy_post_kernel,
        grid=(C // LANES, nb),
        in_specs=[pl.BlockSpec((LANES, None, B, CONV_BLK), lambda cb, ib: (cb, ib, 0, 0)), nat, nat],
        out_specs=nat,
        out_shape=jax.ShapeDtypeStruct(a.shape, BF16),
        scratch_shapes=[pltpu.VMEM((B, LANES, CONV_BLK), F32)],
        compiler_params=_cparams(("parallel", "parallel"), 48),
        name="hyena_post",
    )(yt, a, bt)


def _attn_kernel(qt_ref, k_ref, vt_ref, km_ref, vtm_ref, bias_ref, bmeta_ref, lam_ref, sg_ref,
                 o_ref, m_ref, l_ref, acc_ref, s_ref, bm_ref, *, blk):
    i = pl.program_id(2)
    nk = k_ref.shape[0] // blk
    qt = qt_ref[...]
    rowi = lax.broadcasted_iota(jnp.int32, qt.shape, 0)
    zero = jnp.zeros_like(qt)
    qmaps = (jnp.where(rowi < ATT_HEAD_DIM, qt, zero), jnp.where(rowi >= ATT_HEAD_DIM, qt, zero))

    kmeta, vtmeta, bias_m = km_ref[...], vtm_ref[...], bmeta_ref[jnp.minimum(i, 1)]
    for c in range(2):
        s = jnp.dot(kmeta, qmaps[c], preferred_element_type=F32) + bias_m
        m0 = jnp.max(s, axis=0, keepdims=True)
        p = jnp.exp2(s - m0)
        m_ref[c] = m0
        l_ref[c] = jnp.sum(p, axis=0, keepdims=True)
        acc_ref[c] = jnp.dot(vtmeta, p.astype(BF16), preferred_element_type=F32)

    def scores(j, slot):
        kblk = k_ref[j * blk:(j + 1) * blk, :]
        bias = bias_ref[jnp.clip(j - i, -2, 2) + 2]
        for c in range(2):
            s = jnp.dot(kblk, qmaps[c], preferred_element_type=F32) + bias
            s_ref[slot, c] = s
            bm_ref[slot, c] = jnp.max(s, axis=0, keepdims=True)

    def consume(j, slot):
        vtblk = vt_ref[:, j * blk:(j + 1) * blk]
        for c in range(2):
            m_old = m_ref[c]
            m_new = jnp.maximum(m_old, bm_ref[slot, c])
            alpha = jnp.exp2(m_old - m_new)
            p = jnp.exp2(s_ref[slot, c] - m_new)
            l_ref[c] = alpha * l_ref[c] + jnp.sum(p, axis=0, keepdims=True)
            acc_ref[c] = alpha * acc_ref[c] + jnp.dot(vtblk, p.astype(BF16), preferred_element_type=F32)
            m_ref[c] = m_new

    scores(0, 0)
    for j in range(nk):
        if j + 1 < nk:
            scores(j + 1, (j + 1) % 2)
        consume(j, j % 2)

    lq = lam_ref[...]
    lam = (jnp.exp(jnp.sum(lq[0:1] * lq[1:2], axis=1, keepdims=True))
           - jnp.exp(jnp.sum(lq[2:3] * lq[3:4], axis=1, keepdims=True)) + LAM_INIT)
    ot = acc_ref[0] / l_ref[0] - lam * (acc_ref[1] / l_ref[1])
    ot = ot * lax.rsqrt(jnp.mean(ot * ot, axis=0, keepdims=True) + RMS_EPS) * sg_ref[...] * (1.0 - LAM_INIT)
    o_ref[...] = ot.T.astype(BF16)


def _t5_bucket(rel):
    nb = N_BUCKETS // 2
    max_exact = nb // 2
    ret = jnp.where(rel > 0, nb, 0)
    n = jnp.abs(rel)
    nf = jnp.maximum(n, 1).astype(F32)
    large = max_exact + (jnp.log(nf / max_exact) / math.log(MAX_DISTANCE / max_exact)
                         * (nb - max_exact)).astype(jnp.int32)
    large = jnp.minimum(large, nb - 1)
    return ret + jnp.where(n < max_exact, n, large)


def _attn_bias_tables(rel_bias, blk):
    assert blk >= MAX_DISTANCE
    table = rel_bias.astype(F32) * LOG2E

    def bias_of(rel):
        bucket = _t5_bucket(rel)
        sel = bucket[None] == jnp.arange(N_BUCKETS, dtype=jnp.int32).reshape((N_BUCKETS,) + (1,) * rel.ndim)
        return jnp.stack([jnp.sum(jnp.where(sel, table[:, h].reshape((N_BUCKETS,) + (1,) * rel.ndim), 0.0), axis=0)
                          for h in range(ATT_HEADS)])

    r = jnp.arange(blk, dtype=jnp.int32)
    rel = jnp.stack([dj * blk + r[:, None] - r[None, :] for dj in (-2, -1, 0, 1, 2)])
    tiles = bias_of(rel)
    m = jnp.arange(N_META, dtype=jnp.int32)
    rel_m0 = m[:, None] - (N_META + r[None, :])
    rel_m1 = rel_m0 - blk
    bmeta = bias_of(jnp.stack([rel_m0, rel_m1]))
    return tiles, bmeta


def _attention(qt, k, vt, kmeta, vtmeta, tiles, bmeta, lam, subln_col, *, blk):
    B, S, _ = k.shape
    return pl.pallas_call(
        functools.partial(_attn_kernel, blk=blk),
        grid=(B, ATT_HEADS, S // blk),
        in_specs=[pl.BlockSpec((None, ATT_V_DIM, blk), lambda b, h, i: (b, h, i)),
                  pl.BlockSpec((None, S, ATT_V_DIM), lambda b, h, i: (b, 0, h)),
                  pl.BlockSpec((None, ATT_V_DIM, S), lambda b, h, i: (b, h, 0)),
                  pl.BlockSpec((N_META, ATT_V_DIM), lambda b, h, i: (0, h)),
                  pl.BlockSpec((ATT_V_DIM, N_META), lambda b, h, i: (h, 0)),
                  pl.BlockSpec((None, 5, blk, blk), lambda b, h, i: (h, 0, 0, 0)),
                  pl.BlockSpec((None, 2, N_META, blk), lambda b, h, i: (h, 0, 0, 0)),
                  pl.BlockSpec((4, ATT_HEAD_DIM), lambda b, h, i: (0, 0)),
                  pl.BlockSpec((ATT_V_DIM, 1), lambda b, h, i: (0, 0))],
        out_specs=pl.BlockSpec((None, blk, ATT_V_DIM), lambda b, h, i: (b, i, h)),
        scratch_shapes=[pltpu.VMEM((2, 1, blk), F32), pltpu.VMEM((2, 1, blk), F32),
                        pltpu.VMEM((2, ATT_V_DIM, blk), F32),
                        pltpu.VMEM((2, 2, blk, blk), F32), pltpu.VMEM((2, 2, 1, blk), F32)],
        out_shape=jax.ShapeDtypeStruct((B, S, ATT_WIDTH), BF16),
        compiler_params=_cparams(("parallel", "parallel", "arbitrary"), 56),
        name="diff_attention",
    )(qt, k, vt, kmeta, vtmeta, tiles, bmeta, lam, subln_col)


def _merge_kernel(x_ref, ya_ref, yb_ref, sg_ref, pa_ref, pb_ref, wo_ref, g2_ref, wr2_ref, br_ref,
                  tri_ref, x1_ref, h2_ref, ti_ref, tg_ref, rk_ref, cnt_ref, lg_ref):
    ma = jnp.dot(ya_ref[...], pa_ref[...], preferred_element_type=F32)
    mb = jnp.dot(yb_ref[...], pb_ref[...], preferred_element_type=F32)
    m = sg_ref[:, :D_MODEL].astype(F32) * ma + sg_ref[:, D_MODEL:].astype(F32) * mb
    x1 = x_ref[...] + jnp.dot(m.astype(BF16), wo_ref[...], preferred_element_type=F32)
    x1_ref[...] = x1
    h2 = x1 * lax.rsqrt(jnp.mean(x1 * x1, axis=-1, keepdims=True) + RMS_EPS) * g2_ref[...]
    h2_hi = h2.astype(BF16)
    h2_ref[...] = _pack_rows(h2)
    h2_lo = (h2 - h2_hi.astype(F32)).astype(BF16)
    wr2 = wr2_ref[...]
    lg_ref[0] = jnp.dot(h2_hi, wr2, preferred_element_type=F32)
    lg_ref[1] = jnp.dot(h2_lo, wr2, preferred_element_type=F32)
    hi_t = lg_ref[0].T
    lo_t = lg_ref[1].T
    logits = hi_t[:N_EXPERTS] + hi_t[N_EXPERTS:] + lo_t[:N_EXPERTS] + br_ref[...]
    row = lax.broadcasted_iota(jnp.int32, logits.shape, 0)
    slot = lax.broadcasted_iota(jnp.int32, ti_ref.shape, 0)
    top_i = jnp.zeros(ti_ref.shape, jnp.int32)
    top_v = jnp.zeros(tg_ref.shape, F32)
    work = logits
    picks = []
    for kk in range(TOP_K):
        mx = jnp.max(work, axis=0, keepdims=True)
        idx = jnp.min(jnp.where(work == mx, row, N_EXPERTS), axis=0, keepdims=True)
        pick = row == idx
        picks.append(pick)
        top_i = jnp.where(slot == kk, idx, top_i)
        top_v = jnp.where(slot == kk, mx, top_v)
        work = jnp.where(pick, -jnp.inf, work)
    ex = jnp.exp(top_v - jnp.max(top_v, axis=0, keepdims=True))
    ti_ref[...] = top_i
    tg_ref[...] = ex / jnp.sum(ex, axis=0, keepdims=True)

    @pl.when(pl.program_id(0) == 0)
    def _():
        cnt_ref[...] = jnp.zeros_like(cnt_ref)

    chosen = jnp.where(work == -jnp.inf, 1.0, 0.0)
    before = cnt_ref[...] + jnp.dot(chosen.astype(BF16), tri_ref[...], preferred_element_type=F32)
    rank = jnp.zeros(rk_ref.shape, F32)
    for kk in range(TOP_K):
        rk = jnp.sum(jnp.where(picks[kk], before, 0.0), axis=0, keepdims=True)
        rank = jnp.where(slot == kk, rk, rank)
    rk_ref[...] = rank.astype(jnp.int32)
    cnt_ref[...] += jnp.sum(chosen, axis=1, keepdims=True)


def _merge(x, ya, yb, sg, pa, pb, wo, g2, wr_hi, wr_lo, br, *, tm):
    T, D = x.shape
    tok = lambda w: pl.BlockSpec((tm, w), lambda i: (i, 0))
    tok_t = pl.BlockSpec((TOP_K, tm), lambda i: (0, i))
    tri = jnp.triu(jnp.ones((tm, tm), F32), 1).astype(BF16)
    wr2 = jnp.concatenate([wr_hi, wr_lo], axis=1)
    return pl.pallas_call(
        _merge_kernel,
        grid=(T // tm,),
        in_specs=[tok(D), tok(HY_WIDTH), tok(ATT_WIDTH), tok(COL_G), _const_spec((HY_WIDTH, D)),
                  _const_spec((ATT_WIDTH, D)), _const_spec((D, D)), _const_spec((1, D)),
                  _const_spec((D, 2 * N_EXPERTS)), _const_spec((N_EXPERTS, 1)), _const_spec((tm, tm))],
        out_specs=[tok(D), tok(D // 2), tok_t, tok_t, tok_t, pl.BlockSpec((N_EXPERTS, 1), lambda i: (0, 0))],
        out_shape=[jax.ShapeDtypeStruct((T, D), F32), jax.ShapeDtypeStruct((T, D // 2), jnp.uint32),
                   jax.ShapeDtypeStruct((TOP_K, T), jnp.int32), jax.ShapeDtypeStruct((TOP_K, T), F32),
                   jax.ShapeDtypeStruct((TOP_K, T), jnp.int32), jax.ShapeDtypeStruct((N_EXPERTS, 1), F32)],
        scratch_shapes=[pltpu.VMEM((2, tm, 2 * N_EXPERTS), F32)],
        compiler_params=_cparams(("arbitrary",), 48),
        name="merge_router",
    )(x, ya, yb, sg, pa, pb, wo, g2, wr2, br.reshape(N_EXPERTS, 1), tri)


def _moe_kernel(iblk_ref, iexp_ref, start_ref, count_ref, nitem_ref, xs_ref, w1_ref, b1_ref, w2_ref, b2_ref, ys_ref,
                w1b_ref, w2b_ref):
    i = pl.program_id(0)
    tm = xs_ref.shape[0]

    @pl.when(jnp.logical_and(i < nitem_ref[0],
                             jnp.logical_or(i == 0, iexp_ref[jnp.maximum(i - 1, 0)] != iexp_ref[i])))
    def _():
        rows = 256
        for r in range(0, D_MODEL, rows):
            w1b_ref[r:r + rows, :] = w1_ref[r:r + rows, :].astype(BF16)
        for r in range(0, D_FF, rows):
            w2b_ref[r:r + rows, :] = w2_ref[r:r + rows, :].astype(BF16)

    @pl.when(i < nitem_ref[0])
    def _():
        blk = iblk_ref[i]
        e = iexp_ref[i]
        x = jnp.concatenate(_unpack_rows(xs_ref[...]), axis=1).astype(BF16)
        u = jnp.dot(x, w1b_ref[...], preferred_element_type=F32) + b1_ref[...]
        glu = jnp.minimum(u[:, :D_FF], SWIGLU_LIMIT)
        lin = jnp.clip(u[:, D_FF:], -SWIGLU_LIMIT, SWIGLU_LIMIT)
        a = glu * jax.nn.sigmoid(SWIGLU_ALPHA * glu) * (lin + 1.0)
        y = _pack_rows(jnp.dot(a.astype(BF16), w2b_ref[...], preferred_element_type=F32) + b2_ref[...])
        row = lax.broadcasted_iota(jnp.int32, (tm, 1), 0) + blk * tm
        mine = jnp.logical_and(row >= start_ref[e], row < start_ref[e] + count_ref[e])
        first_visit = jnp.logical_or(i == 0, iblk_ref[jnp.maximum(i - 1, 0)] != blk)

        @pl.when(first_visit)
        def _():
            ys_ref[...] = jnp.where(mine, y, jnp.zeros_like(y))

        @pl.when(jnp.logical_not(first_visit))
        def _():
            ys_ref[...] = jnp.where(mine, y, ys_ref[...])


def _moe_experts(item_blk, item_exp, start, count, n_items, xs, w1, b1, w2, b2, *, tm):
    A, D = xs.shape[0], xs.shape[1] * 2
    grid_spec = pltpu.PrefetchScalarGridSpec(
        num_scalar_prefetch=5,
        grid=(item_blk.shape[0],),
        in_specs=[pl.BlockSpec((tm, D // 2), lambda i, ib, ie, *_: (ib[i], 0)),
                  pl.BlockSpec((None, D, 2 * D_FF), lambda i, ib, ie, *_: (ie[i], 0, 0)),
                  pl.BlockSpec((None, 1, 2 * D_FF), lambda i, ib, ie, *_: (ie[i], 0, 0)),
                  pl.BlockSpec((None, D_FF, D), lambda i, ib, ie, *_: (ie[i], 0, 0)),
                  pl.BlockSpec((None, 1, D), lambda i, ib, ie, *_: (ie[i], 0, 0))],
        out_specs=pl.BlockSpec((tm, D // 2), lambda i, ib, ie, *_: (ib[i], 0)),
        scratch_shapes=[pltpu.VMEM((D, 2 * D_FF), BF16), pltpu.VMEM((D_FF, D), BF16)],
    )
    return pl.pallas_call(
        _moe_kernel,
        grid_spec=grid_spec,
        out_shape=jax.ShapeDtypeStruct((A, D // 2), jnp.uint32),
        compiler_params=_cparams(("arbitrary",), 56),
        name="moe_experts",
    )(item_blk, item_exp, start, count, n_items, xs, w1, b1, w2, b2)


def _combine_kernel(x1_ref, yg_ref, tg_ref, o_ref):
    half = D_MODEL // 2
    acc_lo = x1_ref[:, :half]
    acc_hi = x1_ref[:, half:]
    g = tg_ref[...]
    for kk in range(TOP_K):
        lo, hi = _unpack_rows(yg_ref[kk])
        acc_lo = acc_lo + g[:, kk:kk + 1] * lo
        acc_hi = acc_hi + g[:, kk:kk + 1] * hi
    o_ref[:, :half] = acc_lo
    o_ref[:, half:] = acc_hi


def _combine(x1, yg, tg, *, tm):
    T, D = x1.shape
    return pl.pallas_call(
        _combine_kernel,
        grid=(T // tm,),
        in_specs=[pl.BlockSpec((tm, D), lambda i: (i, 0)), pl.BlockSpec((TOP_K, tm, D // 2), lambda i: (0, i, 0)),
                  pl.BlockSpec((tm, TOP_K), lambda i: (i, 0))],
        out_specs=pl.BlockSpec((tm, D), lambda i: (i, 0)),
        out_shape=jax.ShapeDtypeStruct((T, D), F32),
        compiler_params=_cparams(("parallel",), 48),
        name="moe_combine",
    )(x1, yg, tg)


def _lookup(table, idx):
    n = table.shape[0]
    sel = idx[None] == jnp.arange(n, dtype=jnp.int32).reshape((n,) + (1,) * idx.ndim)
    return jnp.sum(jnp.where(sel, table.reshape((n,) + (1,) * idx.ndim), 0), axis=0)


def _route(top_i, rank, counts, *, tm):
    T = top_i.shape[1]
    A = T * TOP_K
    counts = counts.reshape(N_EXPERTS).astype(jnp.int32)
    start = jnp.cumsum(counts) - counts
    flat_e = top_i.reshape(A // LANES, LANES)
    slot_of = (_lookup(start, flat_e) + rank.reshape(A // LANES, LANES)).reshape(A)
    slot_tok = (jnp.argsort(top_i.T.reshape(A)) // TOP_K).astype(jnp.int32)
    first_blk = start // tm
    n_it = jnp.where(counts > 0, (start + counts - 1) // tm - first_blk + 1, 0)
    it_end = jnp.cumsum(n_it)
    n_items = it_end[-1]
    n_max = A // tm + N_EXPERTS
    i = jnp.minimum(jnp.arange(n_max, dtype=jnp.int32), n_items - 1)
    item_exp = jnp.sum(it_end[None, :] <= i[:, None], axis=1, dtype=jnp.int32)
    item_blk = _lookup(first_blk, item_exp) + i - _lookup(it_end - n_it, item_exp)
    return slot_tok, slot_of, item_blk.astype(jnp.int32), item_exp, start, counts, n_items.reshape(1)


def _trunk(x, meta_proj, w, tabs, *, tm_prep, tm_tok, tm_moe, attn_blk):
    B, S, D = x.shape
    T = B * S
    hy_m, _, k_m, vt_m, _ = meta_proj
    assert tm_prep == HY_PAD
    front = jnp.concatenate([jnp.zeros((HY_PAD - N_META, COL_HY), BF16), hy_m[0]], axis=0)
    hy_ext, qt, k, vt, sg = _prep(x, front, w['g1'], w['w_in'], w['wqt'], w['wvt'], w['bd'], w['qg'], w['kg'],
                                  tm=tm_prep)
    kt = _hyena_filter(w['fw1'], w['fb1'], w['ffr1'], w['fw2'], w['fb2'], w['ffr2'], w['fw3'], s_real=S)
    zt, a, bt = _hy_pre(hy_ext, w['scw'], w['scb'], w['skip'], s_real=S)
    ya = _hy_post(_hy_conv(kt, zt), a, bt)
    tiles, bmeta = tabs
    yb = _attention(qt, k, vt, k_m[0], vt_m[0], tiles, bmeta, w['lam'], w['subln'], blk=attn_blk)
    x1, h2, top_i, top_g, rank, counts = _merge(
        x.reshape(T, D), ya.reshape(T, HY_WIDTH), yb.reshape(T, ATT_WIDTH), sg.reshape(T, COL_G),
        w['pa'], w['pb'], w['wo'], w['g2'], w['wr_hi'], w['wr_lo'], w['br'], tm=tm_tok)
    slot_tok, slot_of, item_blk, item_exp, start, cnt, n_items = _route(top_i, rank, counts, tm=tm_moe)
    xs = _sc_gather_rows(h2, slot_tok)
    ys = _moe_experts(item_blk, item_exp, start, cnt, n_items, xs, w['w1'], w['b1'], w['w2'], w['b2'], tm=tm_moe)
    yg = _sc_gather_rows(ys, slot_of).reshape(TOP_K, T, D // 2)
    return _combine(x1, yg, top_g.T, tm=tm_tok).reshape(B, S, D)


def kernel(x_prompt, x_sample, meta_tokens, rel_bias, norm1_g, w_in, short_conv_w, short_conv_b, filt_w1, filt_b1,
           filt_freq1, filt_w2, filt_b2, filt_freq2, filt_w3, hy_skip, proj_a, q_norm_g, k_norm_g, lambda_q1,
           lambda_k1, lambda_q2, lambda_k2, subln_g, proj_b, w_out, norm2_g, w_router, b_router, w_mlp1, b_mlp1,
           w_mlp2, b_mlp2):
    l = 0
    n_hm = COL_Q // ATT_HEAD_DIM
    w_in_bf = w_in[l].astype(BF16)
    c_q, c_v = COL_HY, COL_HY + COL_Q + COL_K
    wr_hi = w_router[l].astype(BF16)
    w = {
        'g1': norm1_g[l][None], 'w_in': w_in_bf,
        'wqt': w_in_bf[:, c_q:c_q + COL_Q].T, 'wvt': w_in_bf[:, c_v:c_v + COL_V].T,
        'bd': jnp.kron(jnp.eye(n_hm, dtype=F32), jnp.full((ATT_HEAD_DIM, ATT_HEAD_DIM), 1.0 / ATT_HEAD_DIM, F32)
                       ).astype(BF16),
        'qg': (jnp.tile(q_norm_g[l], n_hm) * (ATT_HEAD_DIM ** -0.5 * LOG2E))[:, None],
        'kg': jnp.tile(k_norm_g[l], n_hm)[None],
        'wr_hi': wr_hi, 'wr_lo': (w_router[l] - wr_hi.astype(F32)).astype(BF16),
        'fw1': filt_w1[l], 'fb1': filt_b1[l], 'ffr1': filt_freq1[l], 'fw2': filt_w2[l], 'fb2': filt_b2[l],
        'ffr2': filt_freq2[l], 'fw3': filt_w3[l],
        'scw': short_conv_w[l], 'scb': short_conv_b[l], 'skip': hy_skip[l],
        'lam': jnp.stack([lambda_q1[l], lambda_k1[l], lambda_q2[l], lambda_k2[l]]), 'subln': subln_g[l][:, None],
        'pa': proj_a[l].astype(BF16), 'pb': proj_b[l].astype(BF16), 'wo': w_out[l].astype(BF16),
        'g2': norm2_g[l][None], 'br': b_router[l][None],
        'w1': w_mlp1[l], 'b1': b_mlp1[l][:, None, :], 'w2': w_mlp2[l], 'b2': b_mlp2[l][:, None, :],
    }
    meta_proj = _prep(meta_tokens[None], None, w['g1'], w['w_in'], w['wqt'], w['wvt'], w['bd'], w['qg'], w['kg'],
                      tm=N_META)
    tabs = _attn_bias_tables(rel_bias, TILES['attn_blk'])
    y_prompt = _trunk(x_prompt, meta_proj, w, tabs, **TILES)
    y_sample = _trunk(x_sample, meta_proj, w, tabs, **TILES)
    return (y_prompt, y_sample)
```

```python
import functools
import math

import jax
import jax.numpy as jnp
import numpy as np
from jax import lax
from jax.experimental import pallas as pl
from jax.experimental.pallas import tpu as pltpu
from jax.experimental.pallas import tpu_sc as plsc

F32 = jnp.float32
BF16 = jnp.bfloat16

D_MODEL = 1024
N_META = 16
RMS_EPS = 1e-6
HY_WIDTH = 512
HY_SHORT = 3
FILT_BANDS = 16
FILT_EMB = 1 + 2 * FILT_BANDS
FILT_HIDDEN = 64
DECAY_TARGET = 1e-2
FAST_DECAY_PCT = 0.3
SLOW_DECAY_PCT = 1.5
ATT_HEADS = 4
ATT_HEAD_DIM = 64
ATT_V_DIM = 128
ATT_WIDTH = 512
N_BUCKETS = 32
MAX_DISTANCE = 128
N_EXPERTS = 32
TOP_K = 4
D_FF = 1024
SWIGLU_LIMIT = 7.0
SWIGLU_ALPHA = 1.702
COL_HY = 3 * HY_WIDTH
COL_Q = 512
COL_K = 512
COL_V = 512
COL_G = 2 * D_MODEL
IN_COLS = COL_HY + COL_Q + COL_K + COL_V + COL_G
LAM_INIT = 0.8 - 0.6 * math.exp(-0.3 * 0)
LOG2E = 1.4426950408889634
ONES_ROWS = 16

V7X_VMEM_BYTES = 64 * 1024 * 1024
LANES = 128
CONV_BLK = 256
HY_PAD = 512
TILES = dict(tm_prep=512, tm_tok=512, tm_moe=512, attn_blk=512)


def _cparams(sem, vmem_mb):
    return pltpu.CompilerParams(dimension_semantics=sem, vmem_limit_bytes=vmem_mb * 1024 * 1024)


def _const_spec(shape):
    nd = len(shape)
    return pl.BlockSpec(shape, lambda *_: (0,) * nd, pipeline_mode=pl.Buffered(1))


def _pack_rows(x):
    n = x.shape[1] // 2
    lo = pltpu.bitcast(x[:, :n].astype(BF16).astype(F32), jnp.uint32)
    hi = pltpu.bitcast(x[:, n:].astype(BF16).astype(F32), jnp.uint32)
    return (hi & jnp.uint32(0xFFFF0000)) | (lo >> 16)


def _unpack_rows(w):
    lo = pltpu.bitcast(w << 16, F32)
    hi = pltpu.bitcast(w & jnp.uint32(0xFFFF0000), F32)
    return lo, hi


def _sc_gather_rows(table, idx, *, chunk=128):
    info = plsc.get_sparse_core_info()
    n_workers = info.num_cores * info.num_subcores
    n_rows, width = idx.shape[0], table.shape[1]
    per_worker = n_rows // n_workers
    mesh = plsc.VectorSubcoreMesh(core_axis_name="c", subcore_axis_name="s")

    @functools.partial(
        pl.kernel, mesh=mesh, out_type=jax.ShapeDtypeStruct((n_rows, width), table.dtype),
        scratch_types=[pltpu.VMEM((chunk,), jnp.int32), pltpu.VMEM((chunk, width), table.dtype),
                       pltpu.SemaphoreType.DMA])
    def gather(table_hbm, idx_hbm, out_hbm, idx_v, rows_v, sem):
        base = (lax.axis_index("s") * info.num_cores + lax.axis_index("c")) * per_worker

        @pl.loop(0, per_worker // chunk)
        def _(j):
            off = pl.multiple_of(base + j * chunk, chunk)
            pltpu.sync_copy(idx_hbm.at[pl.ds(off, chunk)], idx_v)
            pltpu.async_copy(table_hbm.at[idx_v], rows_v, sem).wait()
            pltpu.sync_copy(rows_v, out_hbm.at[pl.ds(off, chunk)])

    return gather(table, idx)


def _prep_kernel(x_ref, front_ref, g1_ref, w_ref, wqt_ref, wvt_ref, bd_ref, qg_ref, kg_ref,
                 hy_ref, qt_ref, k_ref, vt_ref, sg_ref, *, lead):
    if lead:
        @pl.when(pl.program_id(1) == 0)
        def _():
            hy_ref[...] = front_ref[...]

        @pl.when(pl.program_id(1) > 0)
        def _():
            _prep_tile(x_ref, g1_ref, w_ref, wqt_ref, wvt_ref, bd_ref, qg_ref, kg_ref,
                       hy_ref, qt_ref, k_ref, vt_ref, sg_ref)
    else:
        _prep_tile(x_ref, g1_ref, w_ref, wqt_ref, wvt_ref, bd_ref, qg_ref, kg_ref,
                   hy_ref, qt_ref, k_ref, vt_ref, sg_ref)


def _prep_tile(x_ref, g1_ref, w_ref, wqt_ref, wvt_ref, bd_ref, qg_ref, kg_ref,
               hy_ref, qt_ref, k_ref, vt_ref, sg_ref):
    x = x_ref[...]
    ms = jnp.mean(x * x, axis=-1, keepdims=True)
    h = (x * lax.rsqrt(ms + RMS_EPS) * g1_ref[...]).astype(BF16)
    nt_dims = (((1,), (1,)), ((), ()))

    def proj(lo, hi):
        return jnp.dot(h, w_ref[:, lo:hi], preferred_element_type=F32)

    def proj_t(wt_ref):
        return lax.dot_general(wt_ref[...], h, nt_dims, preferred_element_type=F32)

    c0, c1, c2, c3 = COL_HY, COL_HY + COL_Q, COL_HY + COL_Q + COL_K, COL_HY + COL_Q + COL_K + COL_V
    hy_ref[...] = proj(0, c0).astype(BF16)
    qt = proj_t(wqt_ref)
    msq = jnp.dot(bd_ref[...], (qt * qt).astype(BF16), preferred_element_type=F32)
    qt_ref[...] = (qt * lax.rsqrt(msq + RMS_EPS) * qg_ref[...]).astype(BF16)
    kk = proj(c1, c2)
    msk = jnp.dot((kk * kk).astype(BF16), bd_ref[...], preferred_element_type=F32)
    k_ref[...] = (kk * lax.rsqrt(msk + RMS_EPS) * kg_ref[...]).astype(BF16)
    vt_ref[...] = proj_t(wvt_ref).astype(BF16)
    sg_ref[...] = jax.nn.sigmoid(proj(c3, IN_COLS)).astype(BF16)


def _prep(x, front, g1, w_in_bf, wqt, wvt, bd, qg_col, kg, *, tm):
    B, S, D = x.shape
    nt = S // tm
    lead = 0 if front is None else 1
    if front is None:
        front = jnp.zeros((tm, COL_HY), BF16)
    src = lambda i: jnp.maximum(i - lead, 0)
    tok = lambda w: pl.BlockSpec((None, tm, w), lambda b, i: (b, src(i), 0))
    tok_t = lambda w: pl.BlockSpec((None, w, tm), lambda b, i: (b, 0, src(i)))
    return pl.pallas_call(
        functools.partial(_prep_kernel, lead=lead),
        grid=(B, nt + lead),
        in_specs=[tok(D), _const_spec((tm, COL_HY)), _const_spec((1, D)), _const_spec((D, IN_COLS)),
                  _const_spec((COL_Q, D)), _const_spec((COL_V, D)), _const_spec((COL_Q, COL_Q)),
                  _const_spec((COL_Q, 1)), _const_spec((1, COL_K))],
        out_specs=[pl.BlockSpec((None, tm, COL_HY), lambda b, i: (b, i, 0)),
                   tok_t(COL_Q), tok(COL_K), tok_t(COL_V), tok(COL_G)],
        out_shape=[jax.ShapeDtypeStruct((B, S + lead * tm, COL_HY), BF16),
                   jax.ShapeDtypeStruct((B, COL_Q, S), BF16), jax.ShapeDtypeStruct((B, S, COL_K), BF16),
                   jax.ShapeDtypeStruct((B, COL_V, S), BF16), jax.ShapeDtypeStruct((B, S, COL_G), BF16)],
        compiler_params=_cparams(("parallel", "arbitrary"), 48),
        name="prep",
    )(x, front, g1, w_in_bf, wqt, wvt, bd, qg_col, kg)


def _filt_kernel(w1t_ref, w1c_ref, w1s_ref, b1_ref, fr1_ref, w2t_ref, b2_ref, fr2_ref, w3f_ref, w3b_ref,
                 bands_ref, deltas_ref, kt_ref, hid_ref, *, n_seq, s_real):
    lk = kt_ref.shape[1]
    hi = lax.Precision.HIGHEST
    u = lax.broadcasted_iota(jnp.int32, (1, lk), 1)
    d = u - s_real
    pos = jnp.abs(d).astype(F32)
    t = pos / float(max(n_seq - 1, 1))

    @pl.when(pl.program_id(0) == 0)
    def _():
        w = 2.0 * math.pi * pos / float(n_seq)
        ang = bands_ref[...] * w
        pre = (w1t_ref[...] * t
               + jnp.dot(w1c_ref[...], jnp.cos(ang), precision=hi, preferred_element_type=F32)
               - jnp.dot(w1s_ref[...], jnp.sin(ang), precision=hi, preferred_element_type=F32)
               + b1_ref[...])
        h1 = jnp.sin(fr1_ref[...] * pre)
        h2 = jnp.sin(fr2_ref[...] * (jnp.dot(w2t_ref[...], h1, precision=hi, preferred_element_type=F32)
                                     + b2_ref[...]))
        hid_ref[...] = h2

    h2 = hid_ref[...]
    hf = jnp.dot(w3f_ref[...], h2, precision=hi, preferred_element_type=F32)
    hb = jnp.dot(w3b_ref[...], h2, precision=hi, preferred_element_type=F32)
    decay = jnp.exp(-t * deltas_ref[...])
    valid = pos <= float(n_seq - 1)
    kun = jnp.where(valid, jnp.where(d >= 0, hf, hb) * decay, 0.0)
    nrm = lax.rsqrt(jnp.sum(kun * kun, axis=1, keepdims=True) + RMS_EPS)
    kt_ref[...] = kun * nrm


def _hyena_filter(fw1, fb1, ffr1, fw2, fb2, ffr2, fw3, *, s_real):
    n_seq = s_real + N_META
    lk = 2 * s_real + CONV_BLK
    cf = 64
    col = lambda a: a.reshape(-1, 1)
    w1t = fw1.T
    bands = jnp.linspace(1e-4, FILT_BANDS - 1, FILT_BANDS, dtype=F32).reshape(-1, 1)
    max_decay = math.log(DECAY_TARGET) / FAST_DECAY_PCT
    min_decay = math.log(DECAY_TARGET) / SLOW_DECAY_PCT
    deltas = jnp.abs(jnp.linspace(min_decay, max_decay, HY_WIDTH, dtype=F32)).reshape(-1, 1)
    w3t = fw3.T
    h = FILT_HIDDEN
    return pl.pallas_call(
        functools.partial(_filt_kernel, n_seq=n_seq, s_real=s_real),
        grid=(HY_WIDTH // cf,),
        in_specs=[_const_spec((h, 1)), _const_spec((h, FILT_BANDS)), _const_spec((h, FILT_BANDS)),
                  _const_spec((h, 1)), _const_spec((h, 1)), _const_spec((h, h)), _const_spec((h, 1)),
                  _const_spec((h, 1)),
                  pl.BlockSpec((cf, h), lambda i: (i, 0)), pl.BlockSpec((cf, h), lambda i: (i, 0)),
                  _const_spec((FILT_BANDS, 1)), pl.BlockSpec((cf, 1), lambda i: (i, 0))],
        out_specs=pl.BlockSpec((cf, lk), lambda i: (i, 0)),
        out_shape=jax.ShapeDtypeStruct((HY_WIDTH, lk), F32),
        scratch_shapes=[pltpu.VMEM((h, lk), F32)],
        compiler_params=_cparams(("arbitrary",), 48),
        name="hyena_filter",
    )(w1t[:, 0:1], w1t[:, 1:1 + FILT_BANDS], w1t[:, 1 + FILT_BANDS:], col(fb1), col(ffr1), fw2.T, col(fb2),
      col(ffr2), w3t[:HY_WIDTH], w3t[HY_WIDTH:], bands, deltas)


def _hy_pre_kernel(x0_ref, x1_ref, v_ref, p0_ref, p1_ref, pv_ref, n0_ref, n1_ref, nv_ref,
                   scw_ref, scb_ref, skip_ref, zt_ref, a_ref, bt_ref, zs_ref):
    jt = pl.program_id(1)
    last = pl.num_programs(1) - 1
    nb_batch = x0_ref.shape[0]
    row = lax.broadcasted_iota(jnp.int32, (CONV_BLK, LANES), 0)
    keep_next = (jt < last).astype(F32)
    z_keep = jnp.logical_or(jt > 0, row >= CONV_BLK - N_META)

    def short_conv(cur_ref, prev_ref, next_ref, comp, b):
        cur = cur_ref[b].astype(F32)
        prev = prev_ref[b, 15:16, :].astype(F32)
        nxt = next_ref[b, 0:1, :].astype(F32) * keep_next
        up = jnp.where(row == 0, prev, pltpu.roll(cur, 1, 0))
        dn = jnp.where(row == CONV_BLK - 1, nxt, pltpu.roll(cur, CONV_BLK - 1, 0))
        w = scw_ref[comp]
        return up * w[0:1] + cur * w[1:2] + dn * w[2:3] + scb_ref[comp]

    def body(b, carry):
        x0c = short_conv(x0_ref, p0_ref, n0_ref, 0, b)
        x1c = short_conv(x1_ref, p1_ref, n1_ref, 1, b)
        vc = short_conv(v_ref, pv_ref, nv_ref, 2, b)
        z = jnp.where(z_keep, x1c * vc, 0.0)
        zs_ref[b] = z.T
        a_ref[b] = x0c.astype(BF16)
        bt_ref[b] = (x0c * z * skip_ref[...]).astype(BF16)
        return carry

    lax.fori_loop(0, nb_batch, body, 0)
    zt_ref[...] = pltpu.einshape("bcl->cbl", zs_ref[...]).astype(BF16)


def _hy_pre(hy_ext, scw, scb, skip, *, s_real):
    B = hy_ext.shape[0]
    nb = s_real // CONV_BLK
    ncb = HY_WIDTH // LANES
    sub = 16
    r16 = CONV_BLK // sub
    last16 = (s_real + HY_PAD) // sub - 1
    cur = lambda comp: pl.BlockSpec((B, CONV_BLK, LANES), lambda cb, jt: (0, jt + 1, comp * ncb + cb))
    prv = lambda comp: pl.BlockSpec((B, sub, LANES), lambda cb, jt: (0, r16 * (jt + 1) - 1, comp * ncb + cb))
    nxt = lambda comp: pl.BlockSpec(
        (B, sub, LANES), lambda cb, jt: (0, jnp.minimum(r16 * (jt + 2), last16), comp * ncb + cb))
    nat = pl.BlockSpec((B, CONV_BLK, LANES), lambda cb, jt: (0, jnp.maximum(jt - 1, 0), cb))
    scw_r = scw.reshape(HY_SHORT, 3, ncb, LANES).transpose(2, 1, 0, 3)
    scb_r = scb.reshape(3, ncb, 1, LANES).transpose(1, 0, 2, 3)
    skip_r = skip.reshape(ncb, 1, LANES)
    return pl.pallas_call(
        _hy_pre_kernel,
        grid=(ncb, nb + 1),
        in_specs=[cur(0), cur(1), cur(2), prv(0), prv(1), prv(2), nxt(0), nxt(1), nxt(2),
                  pl.BlockSpec((None, 3, HY_SHORT, LANES), lambda cb, jt: (cb, 0, 0, 0)),
                  pl.BlockSpec((None, 3, 1, LANES), lambda cb, jt: (cb, 0, 0, 0)),
                  pl.BlockSpec((None, 1, LANES), lambda cb, jt: (cb, 0, 0))],
        out_specs=[pl.BlockSpec((LANES, None, B, CONV_BLK), lambda cb, jt: (cb, jt, 0, 0)), nat, nat],
        out_shape=[jax.ShapeDtypeStruct((HY_WIDTH, nb + 1, B, CONV_BLK), BF16),
                   jax.ShapeDtypeStruct((B, s_real, HY_WIDTH), BF16),
                   jax.ShapeDtypeStruct((B, s_real, HY_WIDTH), BF16)],
        scratch_shapes=[pltpu.VMEM((B, LANES, CONV_BLK), F32)],
        compiler_params=_cparams(("parallel", "arbitrary"), 48),
        name="hyena_pre",
    )(hy_ext, hy_ext, hy_ext, hy_ext, hy_ext, hy_ext, hy_ext, hy_ext, hy_ext, scw_r, scb_r, skip_r)


def _hy_conv_kernel(kt_ref, z_ref, y_ref, g_ref, *, nb):
    cbk = z_ref.shape[0]
    B = z_ref.shape[2]
    s_real = nb * CONV_BLK
    nq = 2 * s_real // LANES + 1
    upper = (lax.broadcasted_iota(jnp.int32, (LANES, LANES), 1)
             >= lax.broadcasted_iota(jnp.int32, (LANES, LANES), 0))

    def channel(c, carry):
        kt_row = kt_ref[pl.ds(c, 1), :]
        for q in range(nq):
            off = 2 * s_real - LANES * q
            skew = lambda lo: pltpu.roll(jnp.broadcast_to(kt_row[:, lo:lo + LANES], (LANES, LANES)),
                                         0, 1, stride=1, stride_axis=0)
            blk_g = jnp.where(upper, skew(off + LANES), skew(off))
            g_ref[LANES * q:LANES * (q + 1), :] = blk_g.astype(BF16)
        y_ref[c] = jnp.zeros(y_ref.shape[1:], F32)
        for delta in range(-(nb - 1), nb + 1):
            base = CONV_BLK * (nb - delta)
            tile = jnp.concatenate([g_ref[base + LANES:base + LANES + CONV_BLK, :],
                                    g_ref[base:base + CONV_BLK, :]], axis=1)
            jb_lo = max(0, 1 - delta)
            n = min(nb, nb - delta) - jb_lo + 1
            o_lo = jb_lo + delta - 1
            lhs = z_ref[c, jb_lo:jb_lo + n].reshape(n * B, CONV_BLK)
            y_ref[c, o_lo:o_lo + n] += jnp.dot(lhs, tile, preferred_element_type=F32).reshape(n, B, CONV_BLK)
        return carry

    lax.fori_loop(0, cbk, channel, 0)


def _hy_conv(kt, zt, *, cbk=8):
    C, nb1, B, _ = zt.shape
    nb = nb1 - 1
    lk = kt.shape[1]
    return pl.pallas_call(
        functools.partial(_hy_conv_kernel, nb=nb),
        grid=(C // cbk,),
        in_specs=[pl.BlockSpec((cbk, lk), lambda i: (i, 0)),
                  pl.BlockSpec((cbk, nb1, B, CONV_BLK), lambda i: (i, 0, 0, 0))],
        out_specs=pl.BlockSpec((cbk, nb, B, CONV_BLK), lambda i: (i, 0, 0, 0)),
        out_shape=jax.ShapeDtypeStruct((C, nb, B, CONV_BLK), F32),
        scratch_shapes=[pltpu.VMEM((2 * nb * CONV_BLK + LANES, LANES), BF16)],
        compiler_params=_cparams(("parallel",), 48),
        name="hyena_conv",
    )(kt, zt)


def _hy_post_kernel(yt_ref, a_ref, bt_ref, o_ref, ys_ref):
    ys_ref[...] = pltpu.einshape("cbl->bcl", yt_ref[...])

    def body(b, carry):
        y = ys_ref[b].T
        o_ref[b] = (a_ref[b].astype(F32) * y + bt_ref[b].astype(F32)).astype(BF16)
        return carry

    lax.fori_loop(0, a_ref.shape[0], body, 0)


def _hy_post(yt, a, bt):
    C, nb, B, _ = yt.shape
    nat = pl.BlockSpec((B, CONV_BLK, LANES), lambda cb, ib: (0, ib, cb))
    return pl.pallas_call(
        _hy_post_kernel,
        grid=(C // LANES, nb),
        in_specs=[pl.BlockSpec((LANES, None, B, CONV_BLK), lambda cb, ib: (cb, ib, 0, 0)), nat, nat],
        out_specs=nat,
        out_shape=jax.ShapeDtypeStruct(a.shape, BF16),
        scratch_shapes=[pltpu.VMEM((B, LANES, CONV_BLK), F32)],
        compiler_params=_cparams(("parallel", "parallel"), 48),
        name="hyena_post",
    )(yt, a, bt)


def _attn_kernel(qt_ref, k_ref, vt_ref, km_ref, vtm_ref, bias_ref, bmeta_ref, lam_ref, sg_ref,
                 o_ref, m_ref, acc_ref, s_ref, bm_ref, *, blk):
    i = pl.program_id(2)
    nk = k_ref.shape[0] // blk
    assert nk % 2 == 0
    qt = qt_ref[...]
    rowi = lax.broadcasted_iota(jnp.int32, qt.shape, 0)
    zero = jnp.zeros_like(qt)
    qmaps = (jnp.where(rowi < ATT_HEAD_DIM, qt, zero), jnp.where(rowi >= ATT_HEAD_DIM, qt, zero))

    def with_ones(vt):
        return jnp.concatenate([vt, jnp.ones((ONES_ROWS, vt.shape[1]), BF16)], axis=0)

    kmeta, vtmeta, bias_m = km_ref[...], with_ones(vtm_ref[...]), bmeta_ref[jnp.minimum(i, 1)]
    for c in range(2):
        s = jnp.dot(kmeta, qmaps[c], preferred_element_type=F32) + bias_m
        m0 = jnp.max(s, axis=0, keepdims=True)
        m_ref[c] = m0
        acc_ref[c] = jnp.dot(vtmeta, jnp.exp2(s - m0).astype(BF16), preferred_element_type=F32)

    def scores(j, slot):
        kblk = k_ref[pl.ds(pl.multiple_of(j * blk, blk), blk), :]
        bias = bias_ref[jnp.clip(j - i, -2, 2) + 2]
        for c in range(2):
            s = jnp.dot(kblk, qmaps[c], preferred_element_type=F32) + bias
            s_ref[slot, c] = s
            bm_ref[slot, c] = jnp.max(s, axis=0, keepdims=True)

    def consume(j, slot):
        vtblk = with_ones(vt_ref[:, pl.ds(pl.multiple_of(j * blk, blk), blk)])
        for c in range(2):
            m_old = m_ref[c]
            m_new = jnp.maximum(m_old, bm_ref[slot, c])
            p = jnp.exp2(s_ref[slot, c] - m_new)
            acc_ref[c] = (jnp.exp2(m_old - m_new) * acc_ref[c]
                          + jnp.dot(vtblk, p.astype(BF16), preferred_element_type=F32))
            m_ref[c] = m_new

    scores(0, 0)

    def pair(jj, carry):
        j = 2 * jj
        scores(j + 1, 1)
        consume(j, 0)
        scores(j + 2, 0)
        consume(j + 1, 1)
        return carry

    lax.fori_loop(0, nk // 2 - 1, pair, 0)
    scores(nk - 1, 1)
    consume(nk - 2, 0)
    consume(nk - 1, 1)

    lq = lam_ref[...]
    lam = (jnp.exp(jnp.sum(lq[0:1] * lq[1:2], axis=1, keepdims=True))
           - jnp.exp(jnp.sum(lq[2:3] * lq[3:4], axis=1, keepdims=True)) + LAM_INIT)
    v_rows = slice(0, ATT_V_DIM)
    l_row = slice(ATT_V_DIM, ATT_V_DIM + 1)
    ot = acc_ref[0, v_rows] / acc_ref[0, l_row] - lam * (acc_ref[1, v_rows] / acc_ref[1, l_row])
    ot = ot * lax.rsqrt(jnp.mean(ot * ot, axis=0, keepdims=True) + RMS_EPS) * sg_ref[...] * (1.0 - LAM_INIT)
    o_ref[...] = ot.T.astype(BF16)


def _t5_bucket(rel):
    nb = N_BUCKETS // 2
    max_exact = nb // 2
    ret = jnp.where(rel > 0, nb, 0)
    n = jnp.abs(rel)
    nf = jnp.maximum(n, 1).astype(F32)
    large = max_exact + (jnp.log(nf / max_exact) / math.log(MAX_DISTANCE / max_exact)
                         * (nb - max_exact)).astype(jnp.int32)
    large = jnp.minimum(large, nb - 1)
    return ret + jnp.where(n < max_exact, n, large)


def _attn_bias_tables(rel_bias, blk):
    assert blk >= MAX_DISTANCE
    table = rel_bias.astype(F32) * LOG2E

    def bias_of(rel):
        bucket = _t5_bucket(rel)
        sel = bucket[None] == jnp.arange(N_BUCKETS, dtype=jnp.int32).reshape((N_BUCKETS,) + (1,) * rel.ndim)
        return jnp.stack([jnp.sum(jnp.where(sel, table[:, h].reshape((N_BUCKETS,) + (1,) * rel.ndim), 0.0), axis=0)
                          for h in range(ATT_HEADS)])

    r = jnp.arange(blk, dtype=jnp.int32)
    rel = jnp.stack([dj * blk + r[:, None] - r[None, :] for dj in (-2, -1, 0, 1, 2)])
    tiles = bias_of(rel)
    m = jnp.arange(N_META, dtype=jnp.int32)
    rel_m0 = m[:, None] - (N_META + r[None, :])
    rel_m1 = rel_m0 - blk
    bmeta = bias_of(jnp.stack([rel_m0, rel_m1]))
    return tiles, bmeta


def _attention(qt, k, vt, kmeta, vtmeta, tiles, bmeta, lam, subln_col, *, blk):
    B, S, _ = k.shape
    return pl.pallas_call(
        functools.partial(_attn_kernel, blk=blk),
        grid=(B, ATT_HEADS, S // blk),
        in_specs=[pl.BlockSpec((None, ATT_V_DIM, blk), lambda b, h, i: (b, h, i)),
                  pl.BlockSpec((None, S, ATT_V_DIM), lambda b, h, i: (b, 0, h)),
                  pl.BlockSpec((None, ATT_V_DIM, S), lambda b, h, i: (b, h, 0)),
                  pl.BlockSpec((N_META, ATT_V_DIM), lambda b, h, i: (0, h)),
                  pl.BlockSpec((ATT_V_DIM, N_META), lambda b, h, i: (h, 0)),
                  pl.BlockSpec((None, 5, blk, blk), lambda b, h, i: (h, 0, 0, 0)),
                  pl.BlockSpec((None, 2, N_META, blk), lambda b, h, i: (h, 0, 0, 0)),
                  pl.BlockSpec((4, ATT_HEAD_DIM), lambda b, h, i: (0, 0)),
                  pl.BlockSpec((ATT_V_DIM, 1), lambda b, h, i: (0, 0))],
        out_specs=pl.BlockSpec((None, blk, ATT_V_DIM), lambda b, h, i: (b, i, h)),
        scratch_shapes=[pltpu.VMEM((2, 1, blk), F32), pltpu.VMEM((2, ATT_V_DIM + ONES_ROWS, blk), F32),
                        pltpu.VMEM((2, 2, blk, blk), F32), pltpu.VMEM((2, 2, 1, blk), F32)],
        out_shape=jax.ShapeDtypeStruct((B, S, ATT_WIDTH), BF16),
        compiler_params=_cparams(("parallel", "parallel", "arbitrary"), 56),
        name="diff_attention",
    )(qt, k, vt, kmeta, vtmeta, tiles, bmeta, lam, subln_col)


def _merge_kernel(x_ref, ya_ref, yb_ref, sg_ref, pa_ref, pb_ref, wo_ref, g2_ref, wr2_ref, br_ref,
                  tri_ref, x1_ref, h2_ref, ti_ref, tg_ref, rk_ref, cnt_ref, lg_ref):
    ma = jnp.dot(ya_ref[...], pa_ref[...], preferred_element_type=F32)
    mb = jnp.dot(yb_ref[...], pb_ref[...], preferred_element_type=F32)
    m = sg_ref[:, :D_MODEL].astype(F32) * ma + sg_ref[:, D_MODEL:].astype(F32) * mb
    x1 = x_ref[...] + jnp.dot(m.astype(BF16), wo_ref[...], preferred_element_type=F32)
    x1_ref[...] = x1
    h2 = x1 * lax.rsqrt(jnp.mean(x1 * x1, axis=-1, keepdims=True) + RMS_EPS) * g2_ref[...]
    h2_hi = h2.astype(BF16)
    h2_ref[...] = _pack_rows(h2)
    h2_lo = (h2 - h2_hi.astype(F32)).astype(BF16)
    wr2 = wr2_ref[...]
    lg_ref[0] = jnp.dot(h2_hi, wr2, preferred_element_type=F32)
    lg_ref[1] = jnp.dot(h2_lo, wr2, preferred_element_type=F32)
    hi_t = lg_ref[0].T
    lo_t = lg_ref[1].T
    logits = hi_t[:N_EXPERTS] + hi_t[N_EXPERTS:] + lo_t[:N_EXPERTS] + br_ref[...]
    row = lax.broadcasted_iota(jnp.int32, logits.shape, 0)
    slot = lax.broadcasted_iota(jnp.int32, ti_ref.shape, 0)
    top_i = jnp.zeros(ti_ref.shape, jnp.int32)
    top_v = jnp.zeros(tg_ref.shape, F32)
    work = logits
    picks = []
    for kk in range(TOP_K):
        mx = jnp.max(work, axis=0, keepdims=True)
        idx = jnp.min(jnp.where(work == mx, row, N_EXPERTS), axis=0, keepdims=True)
        pick = row == idx
        picks.append(pick)
        top_i = jnp.where(slot == kk, idx, top_i)
        top_v = jnp.where(slot == kk, mx, top_v)
        work = jnp.where(pick, -jnp.inf, work)
    ex = jnp.exp(top_v - jnp.max(top_v, axis=0, keepdims=True))
    ti_ref[...] = top_i
    tg_ref[...] = ex / jnp.sum(ex, axis=0, keepdims=True)

    @pl.when(pl.program_id(0) == 0)
    def _():
        cnt_ref[...] = jnp.zeros_like(cnt_ref)

    chosen = jnp.where(work == -jnp.inf, 1.0, 0.0)
    before = cnt_ref[...] + jnp.dot(chosen.astype(BF16), tri_ref[...], preferred_element_type=F32)
    rank = jnp.zeros(rk_ref.shape, F32)
    for kk in range(TOP_K):
        rk = jnp.sum(jnp.where(picks[kk], before, 0.0), axis=0, keepdims=True)
        rank = jnp.where(slot == kk, rk, rank)
    rk_ref[...] = rank.astype(jnp.int32)
    cnt_ref[...] += jnp.sum(chosen, axis=1, keepdims=True)


def _merge(x, ya, yb, sg, pa, pb, wo, g2, wr_hi, wr_lo, br, *, tm):
    T, D = x.shape
    tok = lambda w: pl.BlockSpec((tm, w), lambda i: (i, 0))
    tok_t = pl.BlockSpec((TOP_K, tm), lambda i: (0, i))
    tri = jnp.triu(jnp.ones((tm, tm), F32), 1).astype(BF16)
    wr2 = jnp.concatenate([wr_hi, wr_lo], axis=1)
    return pl.pallas_call(
        _merge_kernel,
        grid=(T // tm,),
        in_specs=[tok(D), tok(HY_WIDTH), tok(ATT_WIDTH), tok(COL_G), _const_spec((HY_WIDTH, D)),
                  _const_spec((ATT_WIDTH, D)), _const_spec((D, D)), _const_spec((1, D)),
                  _const_spec((D, 2 * N_EXPERTS)), _const_spec((N_EXPERTS, 1)), _const_spec((tm, tm))],
        out_specs=[tok(D), tok(D // 2), tok_t, tok_t, tok_t, pl.BlockSpec((N_EXPERTS, 1), lambda i: (0, 0))],
        out_shape=[jax.ShapeDtypeStruct((T, D), F32), jax.ShapeDtypeStruct((T, D // 2), jnp.uint32),
                   jax.ShapeDtypeStruct((TOP_K, T), jnp.int32), jax.ShapeDtypeStruct((TOP_K, T), F32),
                   jax.ShapeDtypeStruct((TOP_K, T), jnp.int32), jax.ShapeDtypeStruct((N_EXPERTS, 1), F32)],
        scratch_shapes=[pltpu.VMEM((2, tm, 2 * N_EXPERTS), F32)],
        compiler_params=_cparams(("arbitrary",), 48),
        name="merge_router",
    )(x, ya, yb, sg, pa, pb, wo, g2, wr2, br.reshape(N_EXPERTS, 1), tri)


def _moe_kernel(iblk_ref, iexp_ref, start_ref, count_ref, nitem_ref, xs_ref, w1_ref, b1_ref, w2_ref, b2_ref, ys_ref,
                w1b_ref, w2b_ref):
    i = pl.program_id(0)
    tm = xs_ref.shape[0]

    @pl.when(jnp.logical_and(i < nitem_ref[0],
                             jnp.logical_or(i == 0, iexp_ref[jnp.maximum(i - 1, 0)] != iexp_ref[i])))
    def _():
        rows = 256
        for r in range(0, D_MODEL, rows):
            w1b_ref[r:r + rows, :] = w1_ref[r:r + rows, :].astype(BF16)
        for r in range(0, D_FF, rows):
            w2b_ref[r:r + rows, :] = w2_ref[r:r + rows, :].astype(BF16)

    @pl.when(i < nitem_ref[0])
    def _():
        blk = iblk_ref[i]
        e = iexp_ref[i]
        x = jnp.concatenate(_unpack_rows(xs_ref[...]), axis=1).astype(BF16)
        u = jnp.dot(x, w1b_ref[...], preferred_element_type=F32) + b1_ref[...]
        glu = jnp.minimum(u[:, :D_FF], SWIGLU_LIMIT)
        lin = jnp.clip(u[:, D_FF:], -SWIGLU_LIMIT, SWIGLU_LIMIT)
        a = glu * jax.nn.sigmoid(SWIGLU_ALPHA * glu) * (lin + 1.0)
        y = _pack_rows(jnp.dot(a.astype(BF16), w2b_ref[...], preferred_element_type=F32) + b2_ref[...])
        row = lax.broadcasted_iota(jnp.int32, (tm, 1), 0) + blk * tm
        mine = jnp.logical_and(row >= start_ref[e], row < start_ref[e] + count_ref[e])
        first_visit = jnp.logical_or(i == 0, iblk_ref[jnp.maximum(i - 1, 0)] != blk)

        @pl.when(first_visit)
        def _():
            ys_ref[...] = jnp.where(mine, y, jnp.zeros_like(y))

        @pl.when(jnp.logical_not(first_visit))
        def _():
            ys_ref[...] = jnp.where(mine, y, ys_ref[...])


def _moe_experts(item_blk, item_exp, start, count, n_items, xs, w1, b1, w2, b2, *, tm):
    A, D = xs.shape[0], xs.shape[1] * 2
    grid_spec = pltpu.PrefetchScalarGridSpec(
        num_scalar_prefetch=5,
        grid=(item_blk.shape[0],),
        in_specs=[pl.BlockSpec((tm, D // 2), lambda i, ib, ie, *_: (ib[i], 0)),
                  pl.BlockSpec((None, D, 2 * D_FF), lambda i, ib, ie, *_: (ie[i], 0, 0)),
                  pl.BlockSpec((None, 1, 2 * D_FF), lambda i, ib, ie, *_: (ie[i], 0, 0)),
                  pl.BlockSpec((None, D_FF, D), lambda i, ib, ie, *_: (ie[i], 0, 0)),
                  pl.BlockSpec((None, 1, D), lambda i, ib, ie, *_: (ie[i], 0, 0))],
        out_specs=pl.BlockSpec((tm, D // 2), lambda i, ib, ie, *_: (ib[i], 0)),
        scratch_shapes=[pltpu.VMEM((D, 2 * D_FF), BF16), pltpu.VMEM((D_FF, D), BF16)],
    )
    return pl.pallas_call(
        _moe_kernel,
        grid_spec=grid_spec,
        out_shape=jax.ShapeDtypeStruct((A, D // 2), jnp.uint32),
        compiler_params=_cparams(("arbitrary",), 56),
        name="moe_experts",
    )(item_blk, item_exp, start, count, n_items, xs, w1, b1, w2, b2)


def _combine_kernel(x1_ref, yg_ref, tg_ref, o_ref):
    half = D_MODEL // 2
    acc_lo = x1_ref[:, :half]
    acc_hi = x1_ref[:, half:]
    g = tg_ref[...]
    for kk in range(TOP_K):
        lo, hi = _unpack_rows(yg_ref[kk])
        acc_lo = acc_lo + g[:, kk:kk + 1] * lo
        acc_hi = acc_hi + g[:, kk:kk + 1] * hi
    o_ref[:, :half] = acc_lo
    o_ref[:, half:] = acc_hi


def _combine(x1, yg, tg, *, tm):
    T, D = x1.shape
    return pl.pallas_call(
        _combine_kernel,
        grid=(T // tm,),
        in_specs=[pl.BlockSpec((tm, D), lambda i: (i, 0)), pl.BlockSpec((TOP_K, tm, D // 2), lambda i: (0, i, 0)),
                  pl.BlockSpec((tm, TOP_K), lambda i: (i, 0))],
        out_specs=pl.BlockSpec((tm, D), lambda i: (i, 0)),
        out_shape=jax.ShapeDtypeStruct((T, D), F32),
        compiler_params=_cparams(("parallel",), 48),
        name="moe_combine",
    )(x1, yg, tg)


def _lookup(table, idx):
    n = table.shape[0]
    sel = idx[None] == jnp.arange(n, dtype=jnp.int32).reshape((n,) + (1,) * idx.ndim)
    return jnp.sum(jnp.where(sel, table.reshape((n,) + (1,) * idx.ndim), 0), axis=0)


def _route(top_i, rank, counts, *, tm):
    T = top_i.shape[1]
    A = T * TOP_K
    counts = counts.reshape(N_EXPERTS).astype(jnp.int32)
    start = jnp.cumsum(counts) - counts
    flat_e = top_i.reshape(A // LANES, LANES)
    slot_of = (_lookup(start, flat_e) + rank.reshape(A // LANES, LANES)).reshape(A)
    slot_tok = (jnp.argsort(top_i.T.reshape(A)) // TOP_K).astype(jnp.int32)
    first_blk = start // tm
    n_it = jnp.where(counts > 0, (start + counts - 1) // tm - first_blk + 1, 0)
    it_end = jnp.cumsum(n_it)
    n_items = it_end[-1]
    n_max = A // tm + N_EXPERTS
    i = jnp.minimum(jnp.arange(n_max, dtype=jnp.int32), n_items - 1)
    item_exp = jnp.sum(it_end[None, :] <= i[:, None], axis=1, dtype=jnp.int32)
    item_blk = _lookup(first_blk, item_exp) + i - _lookup(it_end - n_it, item_exp)
    return slot_tok, slot_of, item_blk.astype(jnp.int32), item_exp, start, counts, n_items.reshape(1)


def _trunk(x, meta_proj, w, tabs, *, tm_prep, tm_tok, tm_moe, attn_blk):
    B, S, D = x.shape
    T = B * S
    hy_m, _, k_m, vt_m, _ = meta_proj
    assert tm_prep == HY_PAD
    front = jnp.concatenate([jnp.zeros((HY_PAD - N_META, COL_HY), BF16), hy_m[0]], axis=0)
    hy_ext, qt, k, vt, sg = _prep(x, front, w['g1'], w['w_in'], w['wqt'], w['wvt'], w['bd'], w['qg'], w['kg'],
                                  tm=tm_prep)
    kt = _hyena_filter(w['fw1'], w['fb1'], w['ffr1'], w['fw2'], w['fb2'], w['ffr2'], w['fw3'], s_real=S)
    zt, a, bt = _hy_pre(hy_ext, w['scw'], w['scb'], w['skip'], s_real=S)
    ya = _hy_post(_hy_conv(kt, zt), a, bt)
    tiles, bmeta = tabs
    yb = _attention(qt, k, vt, k_m[0], vt_m[0], tiles, bmeta, w['lam'], w['subln'], blk=attn_blk)
    x1, h2, top_i, top_g, rank, counts = _merge(
        x.reshape(T, D), ya.reshape(T, HY_WIDTH), yb.reshape(T, ATT_WIDTH), sg.reshape(T, COL_G),
        w['pa'], w['pb'], w['wo'], w['g2'], w['wr_hi'], w['wr_lo'], w['br'], tm=tm_tok)
    slot_tok, slot_of, item_blk, item_exp, start, cnt, n_items = _route(top_i, rank, counts, tm=tm_moe)
    xs = _sc_gather_rows(h2, slot_tok)
    ys = _moe_experts(item_blk, item_exp, start, cnt, n_items, xs, w['w1'], w['b1'], w['w2'], w['b2'], tm=tm_moe)
    yg = _sc_gather_rows(ys, slot_of).reshape(TOP_K, T, D // 2)
    return _combine(x1, yg, top_g.T, tm=tm_tok).reshape(B, S, D)


def kernel(x_prompt, x_sample, meta_tokens, rel_bias, norm1_g, w_in, short_conv_w, short_conv_b, filt_w1, filt_b1,
           filt_freq1, filt_w2, filt_b2, filt_freq2, filt_w3, hy_skip, proj_a, q_norm_g, k_norm_g, lambda_q1,
           lambda_k1, lambda_q2, lambda_k2, subln_g, proj_b, w_out, norm2_g, w_router, b_router, w_mlp1, b_mlp1,
           w_mlp2, b_mlp2):
    l = 0
    n_hm = COL_Q // ATT_HEAD_DIM
    w_in_bf = w_in[l].astype(BF16)
    c_q, c_v = COL_HY, COL_HY + COL_Q + COL_K
    wr_hi = w_router[l].astype(BF16)
    w = {
        'g1': norm1_g[l][None], 'w_in': w_in_bf,
        'wqt': w_in_bf[:, c_q:c_q + COL_Q].T, 'wvt': w_in_bf[:, c_v:c_v + COL_V].T,
        'bd': jnp.kron(jnp.eye(n_hm, dtype=F32), jnp.full((ATT_HEAD_DIM, ATT_HEAD_DIM), 1.0 / ATT_HEAD_DIM, F32)
                       ).astype(BF16),
        'qg': (jnp.tile(q_norm_g[l], n_hm) * (ATT_HEAD_DIM ** -0.5 * LOG2E))[:, None],
        'kg': jnp.tile(k_norm_g[l], n_hm)[None],
        'wr_hi': wr_hi, 'wr_lo': (w_router[l] - wr_hi.astype(F32)).astype(BF16),
        'fw1': filt_w1[l], 'fb1': filt_b1[l], 'ffr1': filt_freq1[l], 'fw2': filt_w2[l], 'fb2': filt_b2[l],
        'ffr2': filt_freq2[l], 'fw3': filt_w3[l],
        'scw': short_conv_w[l], 'scb': short_conv_b[l], 'skip': hy_skip[l],
        'lam': jnp.stack([lambda_q1[l], lambda_k1[l], lambda_q2[l], lambda_k2[l]]), 'subln': subln_g[l][:, None],
        'pa': proj_a[l].astype(BF16), 'pb': proj_b[l].astype(BF16), 'wo': w_out[l].astype(BF16),
        'g2': norm2_g[l][None], 'br': b_router[l][None],
        'w1': w_mlp1[l], 'b1': b_mlp1[l][:, None, :], 'w2': w_mlp2[l], 'b2': b_mlp2[l][:, None, :],
    }
    meta_proj = _prep(meta_tokens[None], None, w['g1'], w['w_in'], w['wqt'], w['wvt'], w['bd'], w['qg'], w['kg'],
                      tm=N_META)
    tabs = _attn_bias_tables(rel_bias, TILES['attn_blk'])
    y_prompt = _trunk(x_prompt, meta_proj, w, tabs, **TILES)
    y_sample = _trunk(x_sample, meta_proj, w, tabs, **TILES)
    return (y_prompt, y_sample)
```

```python
import functools
import math

import jax
import jax.numpy as jnp
import numpy as np
from jax import lax
from jax.experimental import pallas as pl
from jax.experimental.pallas import tpu as pltpu
from jax.experimental.pallas import tpu_sc as plsc

F32 = jnp.float32
BF16 = jnp.bfloat16

D_MODEL = 1024
N_META = 16
RMS_EPS = 1e-6
HY_WIDTH = 512
HY_SHORT = 3
FILT_BANDS = 16
FILT_EMB = 1 + 2 * FILT_BANDS
FILT_HIDDEN = 64
DECAY_TARGET = 1e-2
FAST_DECAY_PCT = 0.3
SLOW_DECAY_PCT = 1.5
ATT_HEADS = 4
ATT_HEAD_DIM = 64
ATT_V_DIM = 128
ATT_WIDTH = 512
N_BUCKETS = 32
MAX_DISTANCE = 128
N_EXPERTS = 32
TOP_K = 4
D_FF = 1024
SWIGLU_LIMIT = 7.0
SWIGLU_ALPHA = 1.702
COL_HY = 3 * HY_WIDTH
COL_Q = 512
COL_K = 512
COL_V = 512
COL_G = 2 * D_MODEL
IN_COLS = COL_HY + COL_Q + COL_K + COL_V + COL_G
LAM_INIT = 0.8 - 0.6 * math.exp(-0.3 * 0)
LOG2E = 1.4426950408889634
ONES_ROWS = 16

V7X_VMEM_BYTES = 64 * 1024 * 1024
LANES = 128
CONV_BLK = 256
HY_PAD = 512
TILES = dict(tm_prep=512, tm_tok=512, tm_moe=512, attn_blk=512)


def _cparams(sem, vmem_mb):
    return pltpu.CompilerParams(dimension_semantics=sem, vmem_limit_bytes=vmem_mb * 1024 * 1024)


def _const_spec(shape):
    nd = len(shape)
    return pl.BlockSpec(shape, lambda *_: (0,) * nd, pipeline_mode=pl.Buffered(1))


def _pack_rows(x):
    n = x.shape[1] // 2
    lo = pltpu.bitcast(x[:, :n].astype(BF16).astype(F32), jnp.uint32)
    hi = pltpu.bitcast(x[:, n:].astype(BF16).astype(F32), jnp.uint32)
    return (hi & jnp.uint32(0xFFFF0000)) | (lo >> 16)


def _unpack_rows(w):
    lo = pltpu.bitcast(w << 16, F32)
    hi = pltpu.bitcast(w & jnp.uint32(0xFFFF0000), F32)
    return lo, hi


def _sc_gather_rows(table, idx, *, chunk=64):
    info = plsc.get_sparse_core_info()
    n_workers = info.num_cores * info.num_subcores
    n_rows, width = idx.shape[0], table.shape[1]
    per_worker = n_rows // n_workers
    n_chunks = per_worker // chunk
    assert n_chunks * chunk * n_workers == n_rows and n_chunks % 2 == 0
    mesh = plsc.VectorSubcoreMesh(core_axis_name="c", subcore_axis_name="s")

    @functools.partial(
        pl.kernel, mesh=mesh, out_type=jax.ShapeDtypeStruct((n_rows, width), table.dtype),
        scratch_types=[pltpu.VMEM((2, chunk), jnp.int32), pltpu.VMEM((2, chunk, width), table.dtype),
                       pltpu.SemaphoreType.DMA((2,))])
    def gather(table_hbm, idx_hbm, out_hbm, idx_v, rows_v, sems):
        base = (lax.axis_index("s") * info.num_cores + lax.axis_index("c")) * per_worker

        def rows_copy(buf):
            return pltpu.make_async_copy(table_hbm.at[idx_v.at[buf]], rows_v.at[buf], sems.at[buf])

        def fetch(j, buf):
            off = pl.multiple_of(base + j * chunk, chunk)
            pltpu.sync_copy(idx_hbm.at[pl.ds(off, chunk)], idx_v.at[buf])
            rows_copy(buf).start()

        def drain(j, buf):
            off = pl.multiple_of(base + j * chunk, chunk)
            rows_copy(buf).wait()
            pltpu.sync_copy(rows_v.at[buf], out_hbm.at[pl.ds(off, chunk)])

        fetch(0, 0)

        @pl.loop(0, n_chunks // 2)
        def _(jj):
            j = 2 * jj
            fetch(j + 1, 1)
            drain(j, 0)

            @pl.when(j + 2 < n_chunks)
            def _():
                fetch(j + 2, 0)

            drain(j + 1, 1)

    return gather(table, idx)


def _prep_kernel(x_ref, front_ref, g1_ref, w_ref, wqt_ref, wvt_ref, bd_ref, qg_ref, kg_ref,
                 hy_ref, qt_ref, k_ref, vt_ref, sg_ref, *, lead):
    if lead:
        @pl.when(pl.program_id(1) == 0)
        def _():
            hy_ref[...] = front_ref[...]

        @pl.when(pl.program_id(1) > 0)
        def _():
            _prep_tile(x_ref, g1_ref, w_ref, wqt_ref, wvt_ref, bd_ref, qg_ref, kg_ref,
                       hy_ref, qt_ref, k_ref, vt_ref, sg_ref)
    else:
        _prep_tile(x_ref, g1_ref, w_ref, wqt_ref, wvt_ref, bd_ref, qg_ref, kg_ref,
                   hy_ref, qt_ref, k_ref, vt_ref, sg_ref)


def _prep_tile(x_ref, g1_ref, w_ref, wqt_ref, wvt_ref, bd_ref, qg_ref, kg_ref,
               hy_ref, qt_ref, k_ref, vt_ref, sg_ref):
    x = x_ref[...]
    ms = jnp.mean(x * x, axis=-1, keepdims=True)
    h = (x * lax.rsqrt(ms + RMS_EPS) * g1_ref[...]).astype(BF16)
    nt_dims = (((1,), (1,)), ((), ()))

    def proj(lo, hi):
        return jnp.dot(h, w_ref[:, lo:hi], preferred_element_type=F32)

    def proj_t(wt_ref):
        return lax.dot_general(wt_ref[...], h, nt_dims, preferred_element_type=F32)

    c0, c1, c2, c3 = COL_HY, COL_HY + COL_Q, COL_HY + COL_Q + COL_K, COL_HY + COL_Q + COL_K + COL_V
    hy_ref[...] = proj(0, c0).astype(BF16)
    qt = proj_t(wqt_ref)
    q3 = qt.reshape(COL_Q // ATT_HEAD_DIM, ATT_HEAD_DIM, qt.shape[1])
    q3 = q3 * lax.rsqrt(jnp.mean(q3 * q3, axis=1, keepdims=True) + RMS_EPS)
    qt_ref[...] = (q3.reshape(qt.shape) * qg_ref[...]).astype(BF16)
    kk = proj(c1, c2)
    msk = jnp.dot((kk * kk).astype(BF16), bd_ref[...], preferred_element_type=F32)
    k_ref[...] = (kk * lax.rsqrt(msk + RMS_EPS) * kg_ref[...]).astype(BF16)
    vt_ref[...] = proj_t(wvt_ref).astype(BF16)
    sg_ref[...] = jax.nn.sigmoid(proj(c3, IN_COLS)).astype(BF16)


def _prep(x, front, g1, w_in_bf, wqt, wvt, bd, qg_col, kg, *, tm):
    B, S, D = x.shape
    nt = S // tm
    lead = 0 if front is None else 1
    if front is None:
        front = jnp.zeros((tm, COL_HY), BF16)
    src = lambda i: jnp.maximum(i - lead, 0)
    tok = lambda w: pl.BlockSpec((None, tm, w), lambda b, i: (b, src(i), 0))
    tok_t = lambda w: pl.BlockSpec((None, w, tm), lambda b, i: (b, 0, src(i)))
    return pl.pallas_call(
        functools.partial(_prep_kernel, lead=lead),
        grid=(B, nt + lead),
        in_specs=[tok(D), _const_spec((tm, COL_HY)), _const_spec((1, D)), _const_spec((D, IN_COLS)),
                  _const_spec((COL_Q, D)), _const_spec((COL_V, D)), _const_spec((COL_Q, COL_Q)),
                  _const_spec((COL_Q, 1)), _const_spec((1, COL_K))],
        out_specs=[pl.BlockSpec((None, tm, COL_HY), lambda b, i: (b, i, 0)),
                   tok_t(COL_Q), tok(COL_K), tok_t(COL_V), tok(COL_G)],
        out_shape=[jax.ShapeDtypeStruct((B, S + lead * tm, COL_HY), BF16),
                   jax.ShapeDtypeStruct((B, COL_Q, S), BF16), jax.ShapeDtypeStruct((B, S, COL_K), BF16),
                   jax.ShapeDtypeStruct((B, COL_V, S), BF16), jax.ShapeDtypeStruct((B, S, COL_G), BF16)],
        compiler_params=_cparams(("parallel", "arbitrary"), 48),
        name="prep",
    )(x, front, g1, w_in_bf, wqt, wvt, bd, qg_col, kg)


def _filt_kernel(w1t_ref, w1c_ref, w1s_ref, b1_ref, fr1_ref, w2t_ref, b2_ref, fr2_ref, w3f_ref, w3b_ref,
                 bands_ref, deltas_ref, kt_ref, hid_ref, *, n_seq, s_real):
    lk = kt_ref.shape[1]
    hi = lax.Precision.HIGHEST
    u = lax.broadcasted_iota(jnp.int32, (1, lk), 1)
    d = u - s_real
    pos = jnp.abs(d).astype(F32)
    t = pos / float(max(n_seq - 1, 1))

    @pl.when(pl.program_id(0) == 0)
    def _():
        w = 2.0 * math.pi * pos / float(n_seq)
        ang = bands_ref[...] * w
        pre = (w1t_ref[...] * t
               + jnp.dot(w1c_ref[...], jnp.cos(ang), precision=hi, preferred_element_type=F32)
               - jnp.dot(w1s_ref[...], jnp.sin(ang), precision=hi, preferred_element_type=F32)
               + b1_ref[...])
        h1 = jnp.sin(fr1_ref[...] * pre)
        h2 = jnp.sin(fr2_ref[...] * (jnp.dot(w2t_ref[...], h1, precision=hi, preferred_element_type=F32)
                                     + b2_ref[...]))
        hid_ref[...] = h2

    h2 = hid_ref[...]
    hf = jnp.dot(w3f_ref[...], h2, precision=hi, preferred_element_type=F32)
    hb = jnp.dot(w3b_ref[...], h2, precision=hi, preferred_element_type=F32)
    decay = jnp.exp(-t * deltas_ref[...])
    valid = pos <= float(n_seq - 1)
    kun = jnp.where(valid, jnp.where(d >= 0, hf, hb) * decay, 0.0)
    nrm = lax.rsqrt(jnp.sum(kun * kun, axis=1, keepdims=True) + RMS_EPS)
    kt_ref[...] = kun * nrm


def _hyena_filter(fw1, fb1, ffr1, fw2, fb2, ffr2, fw3, *, s_real):
    n_seq = s_real + N_META
    lk = 2 * s_real + CONV_BLK
    cf = 64
    col = lambda a: a.reshape(-1, 1)
    w1t = fw1.T
    bands = jnp.linspace(1e-4, FILT_BANDS - 1, FILT_BANDS, dtype=F32).reshape(-1, 1)
    max_decay = math.log(DECAY_TARGET) / FAST_DECAY_PCT
    min_decay = math.log(DECAY_TARGET) / SLOW_DECAY_PCT
    deltas = jnp.abs(jnp.linspace(min_decay, max_decay, HY_WIDTH, dtype=F32)).reshape(-1, 1)
    w3t = fw3.T
    h = FILT_HIDDEN
    return pl.pallas_call(
        functools.partial(_filt_kernel, n_seq=n_seq, s_real=s_real),
        grid=(HY_WIDTH // cf,),
        in_specs=[_const_spec((h, 1)), _const_spec((h, FILT_BANDS)), _const_spec((h, FILT_BANDS)),
                  _const_spec((h, 1)), _const_spec((h, 1)), _const_spec((h, h)), _const_spec((h, 1)),
                  _const_spec((h, 1)),
                  pl.BlockSpec((cf, h), lambda i: (i, 0)), pl.BlockSpec((cf, h), lambda i: (i, 0)),
                  _const_spec((FILT_BANDS, 1)), pl.BlockSpec((cf, 1), lambda i: (i, 0))],
        out_specs=pl.BlockSpec((cf, lk), lambda i: (i, 0)),
        out_shape=jax.ShapeDtypeStruct((HY_WIDTH, lk), F32),
        scratch_shapes=[pltpu.VMEM((h, lk), F32)],
        compiler_params=_cparams(("arbitrary",), 48),
        name="hyena_filter",
    )(w1t[:, 0:1], w1t[:, 1:1 + FILT_BANDS], w1t[:, 1 + FILT_BANDS:], col(fb1), col(ffr1), fw2.T, col(fb2),
      col(ffr2), w3t[:HY_WIDTH], w3t[HY_WIDTH:], bands, deltas)


def _hy_pre_kernel(x0_ref, x1_ref, v_ref, p0_ref, p1_ref, pv_ref, n0_ref, n1_ref, nv_ref,
                   scw_ref, scb_ref, skip_ref, zt_ref, a_ref, bt_ref, zs_ref):
    jt = pl.program_id(1)
    last = pl.num_programs(1) - 1
    nb_batch = x0_ref.shape[0]
    row = lax.broadcasted_iota(jnp.int32, (CONV_BLK, LANES), 0)
    keep_next = (jt < last).astype(F32)
    z_keep = jnp.logical_or(jt > 0, row >= CONV_BLK - N_META)

    def short_conv(cur_ref, prev_ref, next_ref, comp, b):
        cur = cur_ref[b].astype(F32)
        prev = prev_ref[b, 15:16, :].astype(F32)
        nxt = next_ref[b, 0:1, :].astype(F32) * keep_next
        up = jnp.where(row == 0, prev, pltpu.roll(cur, 1, 0))
        dn = jnp.where(row == CONV_BLK - 1, nxt, pltpu.roll(cur, CONV_BLK - 1, 0))
        w = scw_ref[comp]
        return up * w[0:1] + cur * w[1:2] + dn * w[2:3] + scb_ref[comp]

    def body(b, carry):
        x0c = short_conv(x0_ref, p0_ref, n0_ref, 0, b)
        x1c = short_conv(x1_ref, p1_ref, n1_ref, 1, b)
        vc = short_conv(v_ref, pv_ref, nv_ref, 2, b)
        z = jnp.where(z_keep, x1c * vc, 0.0)
        zs_ref[b] = z.T
        a_ref[b] = x0c.astype(BF16)
        bt_ref[b] = (x0c * z * skip_ref[...]).astype(BF16)
        return carry

    lax.fori_loop(0, nb_batch, body, 0)
    zt_ref[...] = pltpu.einshape("bcl->cbl", zs_ref[...]).astype(BF16)


def _hy_pre(hy_ext, scw, scb, skip, *, s_real):
    B = hy_ext.shape[0]
    nb = s_real // CONV_BLK
    ncb = HY_WIDTH // LANES
    sub = 16
    r16 = CONV_BLK // sub
    last16 = (s_real + HY_PAD) // sub - 1
    cur = lambda comp: pl.BlockSpec((B, CONV_BLK, LANES), lambda cb, jt: (0, jt + 1, comp * ncb + cb))
    prv = lambda comp: pl.BlockSpec((B, sub, LANES), lambda cb, jt: (0, r16 * (jt + 1) - 1, comp * ncb + cb))
    nxt = lambda comp: pl.BlockSpec(
        (B, sub, LANES), lambda cb, jt: (0, jnp.minimum(r16 * (jt + 2), last16), comp * ncb + cb))
    nat = pl.BlockSpec((B, CONV_BLK, LANES), lambda cb, jt: (0, jnp.maximum(jt - 1, 0), cb))
    scw_r = scw.reshape(HY_SHORT, 3, ncb, LANES).transpose(2, 1, 0, 3)
    scb_r = scb.reshape(3, ncb, 1, LANES).transpose(1, 0, 2, 3)
    skip_r = skip.reshape(ncb, 1, LANES)
    return pl.pallas_call(
        _hy_pre_kernel,
        grid=(ncb, nb + 1),
        in_specs=[cur(0), cur(1), cur(2), prv(0), prv(1), prv(2), nxt(0), nxt(1), nxt(2),
                  pl.BlockSpec((None, 3, HY_SHORT, LANES), lambda cb, jt: (cb, 0, 0, 0)),
                  pl.BlockSpec((None, 3, 1, LANES), lambda cb, jt: (cb, 0, 0, 0)),
                  pl.BlockSpec((None, 1, LANES), lambda cb, jt: (cb, 0, 0))],
        out_specs=[pl.BlockSpec((LANES, None, B, CONV_BLK), lambda cb, jt: (cb, jt, 0, 0)), nat, nat],
        out_shape=[jax.ShapeDtypeStruct((HY_WIDTH, nb + 1, B, CONV_BLK), BF16),
                   jax.ShapeDtypeStruct((B, s_real, HY_WIDTH), BF16),
                   jax.ShapeDtypeStruct((B, s_real, HY_WIDTH), BF16)],
        scratch_shapes=[pltpu.VMEM((B, LANES, CONV_BLK), F32)],
        compiler_params=_cparams(("parallel", "arbitrary"), 48),
        name="hyena_pre",
    )(hy_ext, hy_ext, hy_ext, hy_ext, hy_ext, hy_ext, hy_ext, hy_ext, hy_ext, scw_r, scb_r, skip_r)


def _hy_conv_kernel(kt_ref, z_ref, y_ref, g_ref, *, nb):
    cbk = z_ref.shape[0]
    B = z_ref.shape[2]
    s_real = nb * CONV_BLK
    nq = 2 * s_real // LANES + 1
    upper = (lax.broadcasted_iota(jnp.int32, (LANES, LANES), 1)
             >= lax.broadcasted_iota(jnp.int32, (LANES, LANES), 0))

    def channel(c, carry):
        kt_row = kt_ref[pl.ds(c, 1), :]
        for q in range(nq):
            off = 2 * s_real - LANES * q
            skew = lambda lo: pltpu.roll(jnp.broadcast_to(kt_row[:, lo:lo + LANES], (LANES, LANES)),
                                         0, 1, stride=1, stride_axis=0)
            blk_g = jnp.where(upper, skew(off + LANES), skew(off))
            g_ref[LANES * q:LANES * (q + 1), :] = blk_g.astype(BF16)
        y_ref[c] = jnp.zeros(y_ref.shape[1:], F32)
        for delta in range(-(nb - 1), nb + 1):
            base = CONV_BLK * (nb - delta)
            tile = jnp.concatenate([g_ref[base + LANES:base + LANES + CONV_BLK, :],
                                    g_ref[base:base + CONV_BLK, :]], axis=1)
            jb_lo = max(0, 1 - delta)
            n = min(nb, nb - delta) - jb_lo + 1
            o_lo = jb_lo + delta - 1
            lhs = z_ref[c, jb_lo:jb_lo + n].reshape(n * B, CONV_BLK)
            y_ref[c, o_lo:o_lo + n] += jnp.dot(lhs, tile, preferred_element_type=F32).reshape(n, B, CONV_BLK)
        return carry

    lax.fori_loop(0, cbk, channel, 0)


def _hy_conv(kt, zt, *, cbk=8):
    C, nb1, B, _ = zt.shape
    nb = nb1 - 1
    lk = kt.shape[1]
    return pl.pallas_call(
        functools.partial(_hy_conv_kernel, nb=nb),
        grid=(C // cbk,),
        in_specs=[pl.BlockSpec((cbk, lk), lambda i: (i, 0)),
                  pl.BlockSpec((cbk, nb1, B, CONV_BLK), lambda i: (i, 0, 0, 0))],
        out_specs=pl.BlockSpec((cbk, nb, B, CONV_BLK), lambda i: (i, 0, 0, 0)),
        out_shape=jax.ShapeDtypeStruct((C, nb, B, CONV_BLK), F32),
        scratch_shapes=[pltpu.VMEM((2 * nb * CONV_BLK + LANES, LANES), BF16)],
        compiler_params=_cparams(("parallel",), 48),
        name="hyena_conv",
    )(kt, zt)


def _hy_post_kernel(yt_ref, a_ref, bt_ref, o_ref, ys_ref):
    ys_ref[...] = pltpu.einshape("cbl->bcl", yt_ref[...])

    def body(b, carry):
        y = ys_ref[b].T
        o_ref[b] = (a_ref[b].astype(F32) * y + bt_ref[b].astype(F32)).astype(BF16)
        return carry

    lax.fori_loop(0, a_ref.shape[0], body, 0)


def _hy_post(yt, a, bt):
    C, nb, B, _ = yt.shape
    nat = pl.BlockSpec((B, CONV_BLK, LANES), lambda cb, ib: (0, ib, cb))
    return pl.pallas_call(
        _hy_post_kernel,
        grid=(C // LANES, nb),
        in_specs=[pl.BlockSpec((LANES, None, B, CONV_BLK), lambda cb, ib: (cb, ib, 0, 0)), nat, nat],
        out_specs=nat,
        out_shape=jax.ShapeDtypeStruct(a.shape, BF16),
        scratch_shapes=[pltpu.VMEM((B, LANES, CONV_BLK), F32)],
        compiler_params=_cparams(("parallel", "parallel"), 48),
        name="hyena_post",
    )(yt, a, bt)


def _attn_kernel(qt_ref, k_ref, vt_ref, km_ref, vtm_ref, bias_ref, bmeta_ref, lam_ref, sg_ref,
                 o_ref, m_ref, acc_ref, s_ref, bm_ref, *, blk):
    i = pl.program_id(2)
    nk = k_ref.shape[0] // blk
    assert nk % 2 == 0
    qt = qt_ref[...]
    rowi = lax.broadcasted_iota(jnp.int32, qt.shape, 0)
    zero = jnp.zeros_like(qt)
    qmaps = (jnp.where(rowi < ATT_HEAD_DIM, qt, zero), jnp.where(rowi >= ATT_HEAD_DIM, qt, zero))

    def with_ones(vt):
        return jnp.concatenate([vt, jnp.ones((ONES_ROWS, vt.shape[1]), BF16)], axis=0)

    kmeta, vtmeta, bias_m = km_ref[...], with_ones(vtm_ref[...]), bmeta_ref[jnp.minimum(i, 1)]
    for c in range(2):
        s = jnp.dot(kmeta, qmaps[c], preferred_element_type=F32) + bias_m
        m0 = jnp.max(s, axis=0, keepdims=True)
        m_ref[c] = m0
        acc_ref[c] = jnp.dot(vtmeta, jnp.exp2(s - m0).astype(BF16), preferred_element_type=F32)

    def scores(j, slot):
        kblk = k_ref[pl.ds(pl.multiple_of(j * blk, blk), blk), :]
        bias = bias_ref[jnp.clip(j - i, -2, 2) + 2]
        for c in range(2):
            s = jnp.dot(kblk, qmaps[c], preferred_element_type=F32) + bias
            s_ref[slot, c] = s
            bm_ref[slot, c] = jnp.max(s, axis=0, keepdims=True)

    def consume(j, slot):
        vtblk = with_ones(vt_ref[:, pl.ds(pl.multiple_of(j * blk, blk), blk)])
        for c in range(2):
            m_old = m_ref[c]
            m_new = jnp.maximum(m_old, bm_ref[slot, c])
            p = jnp.exp2(s_ref[slot, c] - m_new)
            acc_ref[c] = (jnp.exp2(m_old - m_new) * acc_ref[c]
                          + jnp.dot(vtblk, p.astype(BF16), preferred_element_type=F32))
            m_ref[c] = m_new

    scores(0, 0)

    def pair(jj, carry):
        j = 2 * jj
        scores(j + 1, 1)
        consume(j, 0)
        scores(j + 2, 0)
        consume(j + 1, 1)
        return carry

    lax.fori_loop(0, nk // 2 - 1, pair, 0)
    scores(nk - 1, 1)
    consume(nk - 2, 0)
    consume(nk - 1, 1)

    lq = lam_ref[...]
    lam = (jnp.exp(jnp.sum(lq[0:1] * lq[1:2], axis=1, keepdims=True))
           - jnp.exp(jnp.sum(lq[2:3] * lq[3:4], axis=1, keepdims=True)) + LAM_INIT)
    v_rows = slice(0, ATT_V_DIM)
    l_row = slice(ATT_V_DIM, ATT_V_DIM + 1)
    ot = acc_ref[0, v_rows] / acc_ref[0, l_row] - lam * (acc_ref[1, v_rows] / acc_ref[1, l_row])
    ot = ot * lax.rsqrt(jnp.mean(ot * ot, axis=0, keepdims=True) + RMS_EPS) * sg_ref[...] * (1.0 - LAM_INIT)
    o_ref[...] = ot.T.astype(BF16)


def _t5_bucket(rel):
    nb = N_BUCKETS // 2
    max_exact = nb // 2
    ret = jnp.where(rel > 0, nb, 0)
    n = jnp.abs(rel)
    nf = jnp.maximum(n, 1).astype(F32)
    large = max_exact + (jnp.log(nf / max_exact) / math.log(MAX_DISTANCE / max_exact)
                         * (nb - max_exact)).astype(jnp.int32)
    large = jnp.minimum(large, nb - 1)
    return ret + jnp.where(n < max_exact, n, large)


def _attn_bias_tables(rel_bias, blk):
    assert blk >= MAX_DISTANCE
    table = rel_bias.astype(F32) * LOG2E

    def bias_of(rel):
        bucket = _t5_bucket(rel)
        sel = bucket[None] == jnp.arange(N_BUCKETS, dtype=jnp.int32).reshape((N_BUCKETS,) + (1,) * rel.ndim)
        return jnp.stack([jnp.sum(jnp.where(sel, table[:, h].reshape((N_BUCKETS,) + (1,) * rel.ndim), 0.0), axis=0)
                          for h in range(ATT_HEADS)])

    r = jnp.arange(blk, dtype=jnp.int32)
    rel = jnp.stack([dj * blk + r[:, None] - r[None, :] for dj in (-2, -1, 0, 1, 2)])
    tiles = bias_of(rel)
    m = jnp.arange(N_META, dtype=jnp.int32)
    rel_m0 = m[:, None] - (N_META + r[None, :])
    rel_m1 = rel_m0 - blk
    bmeta = bias_of(jnp.stack([rel_m0, rel_m1]))
    return tiles, bmeta


def _attention(qt, k, vt, kmeta, vtmeta, tiles, bmeta, lam, subln_col, *, blk):
    B, S, _ = k.shape
    return pl.pallas_call(
        functools.partial(_attn_kernel, blk=blk),
        grid=(B, ATT_HEADS, S // blk),
        in_specs=[pl.BlockSpec((None, ATT_V_DIM, blk), lambda b, h, i: (b, h, i)),
                  pl.BlockSpec((None, S, ATT_V_DIM), lambda b, h, i: (b, 0, h)),
                  pl.BlockSpec((None, ATT_V_DIM, S), lambda b, h, i: (b, h, 0)),
                  pl.BlockSpec((N_META, ATT_V_DIM), lambda b, h, i: (0, h)),
                  pl.BlockSpec((ATT_V_DIM, N_META), lambda b, h, i: (h, 0)),
                  pl.BlockSpec((None, 5, blk, blk), lambda b, h, i: (h, 0, 0, 0)),
                  pl.BlockSpec((None, 2, N_META, blk), lambda b, h, i: (h, 0, 0, 0)),
                  pl.BlockSpec((4, ATT_HEAD_DIM), lambda b, h, i: (0, 0)),
                  pl.BlockSpec((ATT_V_DIM, 1), lambda b, h, i: (0, 0))],
        out_specs=pl.BlockSpec((None, blk, ATT_V_DIM), lambda b, h, i: (b, i, h)),
        scratch_shapes=[pltpu.VMEM((2, 1, blk), F32), pltpu.VMEM((2, ATT_V_DIM + ONES_ROWS, blk), F32),
                        pltpu.VMEM((2, 2, blk, blk), F32), pltpu.VMEM((2, 2, 1, blk), F32)],
        out_shape=jax.ShapeDtypeStruct((B, S, ATT_WIDTH), BF16),
        compiler_params=_cparams(("parallel", "parallel", "arbitrary"), 56),
        name="diff_attention",
    )(qt, k, vt, kmeta, vtmeta, tiles, bmeta, lam, subln_col)


def _merge_kernel(x_ref, ya_ref, yb_ref, sg_ref, pa_ref, pb_ref, wo_ref, g2_ref, wr2_ref, br_ref,
                  tri_ref, x1_ref, h2_ref, ti_ref, tg_ref, rk_ref, cnt_ref, lg_ref):
    ma = jnp.dot(ya_ref[...], pa_ref[...], preferred_element_type=F32)
    mb = jnp.dot(yb_ref[...], pb_ref[...], preferred_element_type=F32)
    m = sg_ref[:, :D_MODEL].astype(F32) * ma + sg_ref[:, D_MODEL:].astype(F32) * mb
    x1 = x_ref[...] + jnp.dot(m.astype(BF16), wo_ref[...], preferred_element_type=F32)
    x1_ref[...] = x1
    h2 = x1 * lax.rsqrt(jnp.mean(x1 * x1, axis=-1, keepdims=True) + RMS_EPS) * g2_ref[...]
    h2_hi = h2.astype(BF16)
    h2_ref[...] = _pack_rows(h2)
    h2_lo = (h2 - h2_hi.astype(F32)).astype(BF16)
    wr2 = wr2_ref[...]
    lg_ref[0] = jnp.dot(h2_hi, wr2, preferred_element_type=F32)
    lg_ref[1] = jnp.dot(h2_lo, wr2, preferred_element_type=F32)
    hi_t = lg_ref[0].T
    lo_t = lg_ref[1].T
    logits = hi_t[:N_EXPERTS] + hi_t[N_EXPERTS:] + lo_t[:N_EXPERTS] + br_ref[...]
    row = lax.broadcasted_iota(jnp.int32, logits.shape, 0)
    slot = lax.broadcasted_iota(jnp.int32, ti_ref.shape, 0)
    top_i = jnp.zeros(ti_ref.shape, jnp.int32)
    top_v = jnp.zeros(tg_ref.shape, F32)
    work = logits
    picks = []
    for kk in range(TOP_K):
        mx = jnp.max(work, axis=0, keepdims=True)
        idx = jnp.min(jnp.where(work == mx, row, N_EXPERTS), axis=0, keepdims=True)
        pick = row == idx
        picks.append(pick)
        top_i = jnp.where(slot == kk, idx, top_i)
        top_v = jnp.where(slot == kk, mx, top_v)
        work = jnp.where(pick, -jnp.inf, work)
    ex = jnp.exp(top_v - jnp.max(top_v, axis=0, keepdims=True))
    ti_ref[...] = top_i
    tg_ref[...] = ex / jnp.sum(ex, axis=0, keepdims=True)

    @pl.when(pl.program_id(0) == 0)
    def _():
        cnt_ref[...] = jnp.zeros_like(cnt_ref)

    chosen = jnp.where(work == -jnp.inf, 1.0, 0.0)
    before = cnt_ref[...] + jnp.dot(chosen.astype(BF16), tri_ref[...], preferred_element_type=F32)
    rank = jnp.zeros(rk_ref.shape, F32)
    for kk in range(TOP_K):
        rk = jnp.sum(jnp.where(picks[kk], before, 0.0), axis=0, keepdims=True)
        rank = jnp.where(slot == kk, rk, rank)
    rk_ref[...] = rank.astype(jnp.int32)
    cnt_ref[...] += jnp.sum(chosen, axis=1, keepdims=True)


def _merge(x, ya, yb, sg, pa, pb, wo, g2, wr_hi, wr_lo, br, *, tm):
    T, D = x.shape
    tok = lambda w: pl.BlockSpec((tm, w), lambda i: (i, 0))
    tok_t = pl.BlockSpec((TOP_K, tm), lambda i: (0, i))
    tri = jnp.triu(jnp.ones((tm, tm), F32), 1).astype(BF16)
    wr2 = jnp.concatenate([wr_hi, wr_lo], axis=1)
    return pl.pallas_call(
        _merge_kernel,
        grid=(T // tm,),
        in_specs=[tok(D), tok(HY_WIDTH), tok(ATT_WIDTH), tok(COL_G), _const_spec((HY_WIDTH, D)),
                  _const_spec((ATT_WIDTH, D)), _const_spec((D, D)), _const_spec((1, D)),
                  _const_spec((D, 2 * N_EXPERTS)), _const_spec((N_EXPERTS, 1)), _const_spec((tm, tm))],
        out_specs=[tok(D), tok(D // 2), tok_t, tok_t, tok_t, pl.BlockSpec((N_EXPERTS, 1), lambda i: (0, 0))],
        out_shape=[jax.ShapeDtypeStruct((T, D), F32), jax.ShapeDtypeStruct((T, D // 2), jnp.uint32),
                   jax.ShapeDtypeStruct((TOP_K, T), jnp.int32), jax.ShapeDtypeStruct((TOP_K, T), F32),
                   jax.ShapeDtypeStruct((TOP_K, T), jnp.int32), jax.ShapeDtypeStruct((N_EXPERTS, 1), F32)],
        scratch_shapes=[pltpu.VMEM((2, tm, 2 * N_EXPERTS), F32)],
        compiler_params=_cparams(("arbitrary",), 48),
        name="merge_router",
    )(x, ya, yb, sg, pa, pb, wo, g2, wr2, br.reshape(N_EXPERTS, 1), tri)


def _moe_kernel(iblk_ref, iexp_ref, start_ref, count_ref, nitem_ref, xs_ref, w1_ref, b1_ref, w2_ref, b2_ref, ys_ref,
                w1b_ref, w2b_ref):
    i = pl.program_id(0)
    tm = xs_ref.shape[0]

    @pl.when(jnp.logical_and(i < nitem_ref[0],
                             jnp.logical_or(i == 0, iexp_ref[jnp.maximum(i - 1, 0)] != iexp_ref[i])))
    def _():
        rows = 256
        for r in range(0, D_MODEL, rows):
            w1b_ref[r:r + rows, :] = w1_ref[r:r + rows, :].astype(BF16)
        for r in range(0, D_FF, rows):
            w2b_ref[r:r + rows, :] = w2_ref[r:r + rows, :].astype(BF16)

    @pl.when(i < nitem_ref[0])
    def _():
        blk = iblk_ref[i]
        e = iexp_ref[i]
        x = jnp.concatenate(_unpack_rows(xs_ref[...]), axis=1).astype(BF16)
        u = jnp.dot(x, w1b_ref[...], preferred_element_type=F32) + b1_ref[...]
        glu = jnp.minimum(u[:, :D_FF], SWIGLU_LIMIT)
        lin = jnp.clip(u[:, D_FF:], -SWIGLU_LIMIT, SWIGLU_LIMIT)
        a = glu * jax.nn.sigmoid(SWIGLU_ALPHA * glu) * (lin + 1.0)
        y = _pack_rows(jnp.dot(a.astype(BF16), w2b_ref[...], preferred_element_type=F32) + b2_ref[...])
        row = lax.broadcasted_iota(jnp.int32, (tm, 1), 0) + blk * tm
        mine = jnp.logical_and(row >= start_ref[e], row < start_ref[e] + count_ref[e])
        first_visit = jnp.logical_or(i == 0, iblk_ref[jnp.maximum(i - 1, 0)] != blk)

        @pl.when(first_visit)
        def _():
            ys_ref[...] = jnp.where(mine, y, jnp.zeros_like(y))

        @pl.when(jnp.logical_not(first_visit))
        def _():
            ys_ref[...] = jnp.where(mine, y, ys_ref[...])


def _moe_experts(item_blk, item_exp, start, count, n_items, xs, w1, b1, w2, b2, *, tm):
    A, D = xs.shape[0], xs.shape[1] * 2
    grid_spec = pltpu.PrefetchScalarGridSpec(
        num_scalar_prefetch=5,
        grid=(item_blk.shape[0],),
        in_specs=[pl.BlockSpec((tm, D // 2), lambda i, ib, ie, *_: (ib[i], 0)),
                  pl.BlockSpec((None, D, 2 * D_FF), lambda i, ib, ie, *_: (ie[i], 0, 0)),
                  pl.BlockSpec((None, 1, 2 * D_FF), lambda i, ib, ie, *_: (ie[i], 0, 0)),
                  pl.BlockSpec((None, D_FF, D), lambda i, ib, ie, *_: (ie[i], 0, 0)),
                  pl.BlockSpec((None, 1, D), lambda i, ib, ie, *_: (ie[i], 0, 0))],
        out_specs=pl.BlockSpec((tm, D // 2), lambda i, ib, ie, *_: (ib[i], 0)),
        scratch_shapes=[pltpu.VMEM((D, 2 * D_FF), BF16), pltpu.VMEM((D_FF, D), BF16)],
    )
    return pl.pallas_call(
        _moe_kernel,
        grid_spec=grid_spec,
        out_shape=jax.ShapeDtypeStruct((A, D // 2), jnp.uint32),
        compiler_params=_cparams(("arbitrary",), 56),
        name="moe_experts",
    )(item_blk, item_exp, start, count, n_items, xs, w1, b1, w2, b2)


def _combine_kernel(x1_ref, yg_ref, tg_ref, o_ref):
    half = D_MODEL // 2
    acc_lo = x1_ref[:, :half]
    acc_hi = x1_ref[:, half:]
    g = tg_ref[...]
    for kk in range(TOP_K):
        lo, hi = _unpack_rows(yg_ref[kk])
        acc_lo = acc_lo + g[:, kk:kk + 1] * lo
        acc_hi = acc_hi + g[:, kk:kk + 1] * hi
    o_ref[:, :half] = acc_lo
    o_ref[:, half:] = acc_hi


def _combine(x1, yg, tg, *, tm):
    T, D = x1.shape
    return pl.pallas_call(
        _combine_kernel,
        grid=(T // tm,),
        in_specs=[pl.BlockSpec((tm, D), lambda i: (i, 0)), pl.BlockSpec((TOP_K, tm, D // 2), lambda i: (0, i, 0)),
                  pl.BlockSpec((tm, TOP_K), lambda i: (i, 0))],
        out_specs=pl.BlockSpec((tm, D), lambda i: (i, 0)),
        out_shape=jax.ShapeDtypeStruct((T, D), F32),
        compiler_params=_cparams(("parallel",), 48),
        name="moe_combine",
    )(x1, yg, tg)


def _lookup(table, idx):
    n = table.shape[0]
    sel = idx[None] == jnp.arange(n, dtype=jnp.int32).reshape((n,) + (1,) * idx.ndim)
    return jnp.sum(jnp.where(sel, table.reshape((n,) + (1,) * idx.ndim), 0), axis=0)


def _route(top_i, rank, counts, *, tm):
    T = top_i.shape[1]
    A = T * TOP_K
    counts = counts.reshape(N_EXPERTS).astype(jnp.int32)
    start = jnp.cumsum(counts) - counts
    flat_e = top_i.reshape(A // LANES, LANES)
    slot_of = (_lookup(start, flat_e) + rank.reshape(A // LANES, LANES)).reshape(A)
    slot_tok = (jnp.argsort(top_i.T.reshape(A)) // TOP_K).astype(jnp.int32)
    first_blk = start // tm
    n_it = jnp.where(counts > 0, (start + counts - 1) // tm - first_blk + 1, 0)
    it_end = jnp.cumsum(n_it)
    n_items = it_end[-1]
    n_max = A // tm + N_EXPERTS
    i = jnp.minimum(jnp.arange(n_max, dtype=jnp.int32), n_items - 1)
    item_exp = jnp.sum(it_end[None, :] <= i[:, None], axis=1, dtype=jnp.int32)
    item_blk = _lookup(first_blk, item_exp) + i - _lookup(it_end - n_it, item_exp)
    return slot_tok, slot_of, item_blk.astype(jnp.int32), item_exp, start, counts, n_items.reshape(1)


def _trunk(x, meta_proj, w, tabs, *, tm_prep, tm_tok, tm_moe, attn_blk):
    B, S, D = x.shape
    T = B * S
    hy_m, _, k_m, vt_m, _ = meta_proj
    assert tm_prep == HY_PAD
    front = jnp.concatenate([jnp.zeros((HY_PAD - N_META, COL_HY), BF16), hy_m[0]], axis=0)
    hy_ext, qt, k, vt, sg = _prep(x, front, w['g1'], w['w_in'], w['wqt'], w['wvt'], w['bd'], w['qg'], w['kg'],
                                  tm=tm_prep)
    kt = _hyena_filter(w['fw1'], w['fb1'], w['ffr1'], w['fw2'], w['fb2'], w['ffr2'], w['fw3'], s_real=S)
    zt, a, bt = _hy_pre(hy_ext, w['scw'], w['scb'], w['skip'], s_real=S)
    ya = _hy_post(_hy_conv(kt, zt), a, bt)
    tiles, bmeta = tabs
    yb = _attention(qt, k, vt, k_m[0], vt_m[0], tiles, bmeta, w['lam'], w['subln'], blk=attn_blk)
    x1, h2, top_i, top_g, rank, counts = _merge(
        x.reshape(T, D), ya.reshape(T, HY_WIDTH), yb.reshape(T, ATT_WIDTH), sg.reshape(T, COL_G),
        w['pa'], w['pb'], w['wo'], w['g2'], w['wr_hi'], w['wr_lo'], w['br'], tm=tm_tok)
    slot_tok, slot_of, item_blk, item_exp, start, cnt, n_items = _route(top_i, rank, counts, tm=tm_moe)
    xs = _sc_gather_rows(h2, slot_tok)
    ys = _moe_experts(item_blk, item_exp, start, cnt, n_items, xs, w['w1'], w['b1'], w['w2'], w['b2'], tm=tm_moe)
    yg = _sc_gather_rows(ys, slot_of).reshape(TOP_K, T, D // 2)
    return _combine(x1, yg, top_g.T, tm=tm_tok).reshape(B, S, D)


def kernel(x_prompt, x_sample, meta_tokens, rel_bias, norm1_g, w_in, short_conv_w, short_conv_b, filt_w1, filt_b1,
           filt_freq1, filt_w2, filt_b2, filt_freq2, filt_w3, hy_skip, proj_a, q_norm_g, k_norm_g, lambda_q1,
           lambda_k1, lambda_q2, lambda_k2, subln_g, proj_b, w_out, norm2_g, w_router, b_router, w_mlp1, b_mlp1,
           w_mlp2, b_mlp2):
    l = 0
    n_hm = COL_Q // ATT_HEAD_DIM
    w_in_bf = w_in[l].astype(BF16)
    c_q, c_v = COL_HY, COL_HY + COL_Q + COL_K
    wr_hi = w_router[l].astype(BF16)
    w = {
        'g1': norm1_g[l][None], 'w_in': w_in_bf,
        'wqt': w_in_bf[:, c_q:c_q + COL_Q].T, 'wvt': w_in_bf[:, c_v:c_v + COL_V].T,
        'bd': jnp.kron(jnp.eye(n_hm, dtype=F32), jnp.full((ATT_HEAD_DIM, ATT_HEAD_DIM), 1.0 / ATT_HEAD_DIM, F32)
                       ).astype(BF16),
        'qg': (jnp.tile(q_norm_g[l], n_hm) * (ATT_HEAD_DIM ** -0.5 * LOG2E))[:, None],
        'kg': jnp.tile(k_norm_g[l], n_hm)[None],
        'wr_hi': wr_hi, 'wr_lo': (w_router[l] - wr_hi.astype(F32)).astype(BF16),
        'fw1': filt_w1[l], 'fb1': filt_b1[l], 'ffr1': filt_freq1[l], 'fw2': filt_w2[l], 'fb2': filt_b2[l],
        'ffr2': filt_freq2[l], 'fw3': filt_w3[l],
        'scw': short_conv_w[l], 'scb': short_conv_b[l], 'skip': hy_skip[l],
        'lam': jnp.stack([lambda_q1[l], lambda_k1[l], lambda_q2[l], lambda_k2[l]]), 'subln': subln_g[l][:, None],
        'pa': proj_a[l].astype(BF16), 'pb': proj_b[l].astype(BF16), 'wo': w_out[l].astype(BF16),
        'g2': norm2_g[l][None], 'br': b_router[l][None],
        'w1': w_mlp1[l], 'b1': b_mlp1[l][:, None, :], 'w2': w_mlp2[l], 'b2': b_mlp2[l][:, None, :],
    }
    meta_proj = _prep(meta_tokens[None], None, w['g1'], w['w_in'], w['wqt'], w['wvt'], w['bd'], w['qg'], w['kg'],
                      tm=N_META)
    tabs = _attn_bias_tables(rel_bias, TILES['attn_blk'])
    y_prompt = _trunk(x_prompt, meta_proj, w, tabs, **TILES)
    y_sample = _trunk(x_sample, meta_proj, w, tabs, **TILES)
    return (y_prompt, y_sample)
```

```python
import functools
import math

import jax
import jax.numpy as jnp
import numpy as np
from jax import lax
from jax.experimental import pallas as pl
from jax.experimental.pallas import tpu as pltpu
from jax.experimental.pallas import tpu_sc as plsc

F32 = jnp.float32
BF16 = jnp.bfloat16

D_MODEL = 1024
N_META = 16
RMS_EPS = 1e-6
HY_WIDTH = 512
HY_SHORT = 3
FILT_BANDS = 16
FILT_EMB = 1 + 2 * FILT_BANDS
FILT_HIDDEN = 64
DECAY_TARGET = 1e-2
FAST_DECAY_PCT = 0.3
SLOW_DECAY_PCT = 1.5
ATT_HEADS = 4
ATT_HEAD_DIM = 64
ATT_V_DIM = 128
ATT_WIDTH = 512
N_BUCKETS = 32
MAX_DISTANCE = 128
N_EXPERTS = 32
TOP_K = 4
D_FF = 1024
SWIGLU_LIMIT = 7.0
SWIGLU_ALPHA = 1.702
COL_HY = 3 * HY_WIDTH
COL_Q = 512
COL_K = 512
COL_V = 512
COL_G = 2 * D_MODEL
IN_COLS = COL_HY + COL_Q + COL_K + COL_V + COL_G
LAM_INIT = 0.8 - 0.6 * math.exp(-0.3 * 0)
LOG2E = 1.4426950408889634
ONES_ROWS = 16

V7X_VMEM_BYTES = 64 * 1024 * 1024
LANES = 128
CONV_BLK = 256
HY_PAD = 512
TILES = dict(tm_prep=512, tm_tok=512, tm_moe=512, attn_blk=512)


def _cparams(sem, vmem_mb):
    return pltpu.CompilerParams(dimension_semantics=sem, vmem_limit_bytes=vmem_mb * 1024 * 1024)


def _const_spec(shape):
    nd = len(shape)
    return pl.BlockSpec(shape, lambda *_: (0,) * nd, pipeline_mode=pl.Buffered(1))


def _pack_rows(x):
    n = x.shape[1] // 2
    lo = pltpu.bitcast(x[:, :n].astype(BF16).astype(F32), jnp.uint32)
    hi = pltpu.bitcast(x[:, n:].astype(BF16).astype(F32), jnp.uint32)
    return (hi & jnp.uint32(0xFFFF0000)) | (lo >> 16)


def _unpack_rows(w):
    lo = pltpu.bitcast(w << 16, F32)
    hi = pltpu.bitcast(w & jnp.uint32(0xFFFF0000), F32)
    return lo, hi


def _sc_invert_slots(slot_of, n_tok, *, chunk=16384):
    info = plsc.get_sparse_core_info()
    n_workers = info.num_cores * info.num_subcores
    lanes = info.num_lanes
    n_picks = slot_of.shape[0]
    per_worker = n_picks // n_workers
    assert n_tok % chunk == 0 and n_picks % chunk == 0 and chunk % lanes == 0 and per_worker * n_workers == n_picks
    mesh = plsc.VectorSubcoreMesh(core_axis_name="c", subcore_axis_name="s")

    @functools.partial(
        pl.kernel, mesh=mesh, out_type=jax.ShapeDtypeStruct((n_picks,), jnp.int32),
        scratch_types=[pltpu.VMEM((chunk,), jnp.int32), pltpu.VMEM((per_worker,), jnp.int32)],
        compiler_params=pltpu.CompilerParams(needs_layout_passes=False))
    def invert(slot_hbm, out_hbm, picks_v, tok_v):
        base = (lax.axis_index("s") * info.num_cores + lax.axis_index("c")) * per_worker
        lane = lax.iota(jnp.int32, lanes)

        @pl.loop(0, n_picks // chunk)
        def _(ci):
            a0 = pl.multiple_of(ci * chunk, chunk)
            pltpu.sync_copy(slot_hbm.at[pl.ds(a0, chunk)], picks_v)
            t0 = lax.rem(a0, n_tok)

            @pl.loop(0, chunk // lanes)
            def _(i):
                row = picks_v[pl.ds(i * lanes, lanes)] - base
                mine = jnp.logical_and(row >= 0, row < per_worker)
                plsc.store_scatter(tok_v, [jnp.where(mine, row, 0)], t0 + i * lanes + lane, mask=mine)

        pltpu.sync_copy(tok_v, out_hbm.at[pl.ds(pl.multiple_of(base, per_worker), per_worker)])

    return invert(slot_of)


def _sc_gather_rows(table, idx, *, chunk=64):
    info = plsc.get_sparse_core_info()
    n_workers = info.num_cores * info.num_subcores
    n_rows, width = idx.shape[0], table.shape[1]
    per_worker = n_rows // n_workers
    n_chunks = per_worker // chunk
    assert n_chunks * chunk * n_workers == n_rows and n_chunks % 2 == 0
    mesh = plsc.VectorSubcoreMesh(core_axis_name="c", subcore_axis_name="s")

    @functools.partial(
        pl.kernel, mesh=mesh, out_type=jax.ShapeDtypeStruct((n_rows, width), table.dtype),
        scratch_types=[pltpu.VMEM((2, chunk), jnp.int32), pltpu.VMEM((2, chunk, width), table.dtype),
                       pltpu.SemaphoreType.DMA((2,))])
    def gather(table_hbm, idx_hbm, out_hbm, idx_v, rows_v, sems):
        base = (lax.axis_index("s") * info.num_cores + lax.axis_index("c")) * per_worker

        def rows_copy(buf):
            return pltpu.make_async_copy(table_hbm.at[idx_v.at[buf]], rows_v.at[buf], sems.at[buf])

        def fetch(j, buf):
            off = pl.multiple_of(base + j * chunk, chunk)
            pltpu.sync_copy(idx_hbm.at[pl.ds(off, chunk)], idx_v.at[buf])
            rows_copy(buf).start()

        def drain(j, buf):
            off = pl.multiple_of(base + j * chunk, chunk)
            rows_copy(buf).wait()
            pltpu.sync_copy(rows_v.at[buf], out_hbm.at[pl.ds(off, chunk)])

        fetch(0, 0)

        @pl.loop(0, n_chunks // 2)
        def _(jj):
            j = 2 * jj
            fetch(j + 1, 1)
            drain(j, 0)

            @pl.when(j + 2 < n_chunks)
            def _():
                fetch(j + 2, 0)

            drain(j + 1, 1)

    return gather(table, idx)


def _prep_kernel(x_ref, front_ref, g1_ref, w_ref, wqt_ref, wvt_ref, bd_ref, qg_ref, kg_ref,
                 hy_ref, qt_ref, k_ref, vt_ref, sg_ref, *, lead):
    if lead:
        @pl.when(pl.program_id(1) == 0)
        def _():
            hy_ref[...] = front_ref[...]

        @pl.when(pl.program_id(1) > 0)
        def _():
            _prep_tile(x_ref, g1_ref, w_ref, wqt_ref, wvt_ref, bd_ref, qg_ref, kg_ref,
                       hy_ref, qt_ref, k_ref, vt_ref, sg_ref)
    else:
        _prep_tile(x_ref, g1_ref, w_ref, wqt_ref, wvt_ref, bd_ref, qg_ref, kg_ref,
                   hy_ref, qt_ref, k_ref, vt_ref, sg_ref)


def _prep_tile(x_ref, g1_ref, w_ref, wqt_ref, wvt_ref, bd_ref, qg_ref, kg_ref,
               hy_ref, qt_ref, k_ref, vt_ref, sg_ref):
    x = x_ref[...]
    ms = jnp.mean(x * x, axis=-1, keepdims=True)
    h = (x * lax.rsqrt(ms + RMS_EPS) * g1_ref[...]).astype(BF16)
    nt_dims = (((1,), (1,)), ((), ()))

    def proj(lo, hi):
        return jnp.dot(h, w_ref[:, lo:hi], preferred_element_type=F32)

    def proj_t(wt_ref):
        return lax.dot_general(wt_ref[...], h, nt_dims, preferred_element_type=F32)

    c0, c1, c2, c3 = COL_HY, COL_HY + COL_Q, COL_HY + COL_Q + COL_K, COL_HY + COL_Q + COL_K + COL_V
    hy_ref[...] = proj(0, c0).astype(BF16)
    qt = proj_t(wqt_ref)
    q3 = qt.reshape(COL_Q // ATT_HEAD_DIM, ATT_HEAD_DIM, qt.shape[1])
    q3 = q3 * lax.rsqrt(jnp.mean(q3 * q3, axis=1, keepdims=True) + RMS_EPS)
    qt_ref[...] = (q3.reshape(qt.shape) * qg_ref[...]).astype(BF16)
    kk = proj(c1, c2)
    msk = jnp.dot((kk * kk).astype(BF16), bd_ref[...], preferred_element_type=F32)
    k_ref[...] = (kk * lax.rsqrt(msk + RMS_EPS) * kg_ref[...]).astype(BF16)
    vt_ref[...] = proj_t(wvt_ref).astype(BF16)
    sg_ref[...] = jax.nn.sigmoid(proj(c3, IN_COLS)).astype(BF16)


def _prep(x, front, g1, w_in_bf, wqt, wvt, bd, qg_col, kg, *, tm):
    B, S, D = x.shape
    nt = S // tm
    lead = 0 if front is None else 1
    if front is None:
        front = jnp.zeros((tm, COL_HY), BF16)
    src = lambda i: jnp.maximum(i - lead, 0)
    tok = lambda w: pl.BlockSpec((None, tm, w), lambda b, i: (b, src(i), 0))
    tok_t = lambda w: pl.BlockSpec((None, w, tm), lambda b, i: (b, 0, src(i)))
    return pl.pallas_call(
        functools.partial(_prep_kernel, lead=lead),
        grid=(B, nt + lead),
        in_specs=[tok(D), _const_spec((tm, COL_HY)), _const_spec((1, D)), _const_spec((D, IN_COLS)),
                  _const_spec((COL_Q, D)), _const_spec((COL_V, D)), _const_spec((COL_Q, COL_Q)),
                  _const_spec((COL_Q, 1)), _const_spec((1, COL_K))],
        out_specs=[pl.BlockSpec((None, tm, COL_HY), lambda b, i: (b, i, 0)),
                   tok_t(COL_Q), tok(COL_K), tok_t(COL_V), tok(COL_G)],
        out_shape=[jax.ShapeDtypeStruct((B, S + lead * tm, COL_HY), BF16),
                   jax.ShapeDtypeStruct((B, COL_Q, S), BF16), jax.ShapeDtypeStruct((B, S, COL_K), BF16),
                   jax.ShapeDtypeStruct((B, COL_V, S), BF16), jax.ShapeDtypeStruct((B, S, COL_G), BF16)],
        compiler_params=_cparams(("parallel", "arbitrary"), 48),
        name="prep",
    )(x, front, g1, w_in_bf, wqt, wvt, bd, qg_col, kg)


def _filt_kernel(w1t_ref, w1c_ref, w1s_ref, b1_ref, fr1_ref, w2t_ref, b2_ref, fr2_ref, w3f_ref, w3b_ref,
                 bands_ref, deltas_ref, kt_ref, hid_ref, *, n_seq, s_real):
    lk = kt_ref.shape[1]
    hi = lax.Precision.HIGHEST
    u = lax.broadcasted_iota(jnp.int32, (1, lk), 1)
    d = u - s_real
    pos = jnp.abs(d).astype(F32)
    t = pos / float(max(n_seq - 1, 1))

    @pl.when(pl.program_id(0) == 0)
    def _():
        w = 2.0 * math.pi * pos / float(n_seq)
        ang = bands_ref[...] * w
        pre = (w1t_ref[...] * t
               + jnp.dot(w1c_ref[...], jnp.cos(ang), precision=hi, preferred_element_type=F32)
               - jnp.dot(w1s_ref[...], jnp.sin(ang), precision=hi, preferred_element_type=F32)
               + b1_ref[...])
        h1 = jnp.sin(fr1_ref[...] * pre)
        h2 = jnp.sin(fr2_ref[...] * (jnp.dot(w2t_ref[...], h1, precision=hi, preferred_element_type=F32)
                                     + b2_ref[...]))
        hid_ref[...] = h2

    h2 = hid_ref[...]
    hf = jnp.dot(w3f_ref[...], h2, precision=hi, preferred_element_type=F32)
    hb = jnp.dot(w3b_ref[...], h2, precision=hi, preferred_element_type=F32)
    decay = jnp.exp(-t * deltas_ref[...])
    valid = pos <= float(n_seq - 1)
    kun = jnp.where(valid, jnp.where(d >= 0, hf, hb) * decay, 0.0)
    nrm = lax.rsqrt(jnp.sum(kun * kun, axis=1, keepdims=True) + RMS_EPS)
    kt_ref[...] = kun * nrm


def _hyena_filter(fw1, fb1, ffr1, fw2, fb2, ffr2, fw3, *, s_real):
    n_seq = s_real + N_META
    lk = 2 * s_real + CONV_BLK
    cf = 64
    col = lambda a: a.reshape(-1, 1)
    w1t = fw1.T
    bands = jnp.linspace(1e-4, FILT_BANDS - 1, FILT_BANDS, dtype=F32).reshape(-1, 1)
    max_decay = math.log(DECAY_TARGET) / FAST_DECAY_PCT
    min_decay = math.log(DECAY_TARGET) / SLOW_DECAY_PCT
    deltas = jnp.abs(jnp.linspace(min_decay, max_decay, HY_WIDTH, dtype=F32)).reshape(-1, 1)
    w3t = fw3.T
    h = FILT_HIDDEN
    return pl.pallas_call(
        functools.partial(_filt_kernel, n_seq=n_seq, s_real=s_real),
        grid=(HY_WIDTH // cf,),
        in_specs=[_const_spec((h, 1)), _const_spec((h, FILT_BANDS)), _const_spec((h, FILT_BANDS)),
                  _const_spec((h, 1)), _const_spec((h, 1)), _const_spec((h, h)), _const_spec((h, 1)),
                  _const_spec((h, 1)),
                  pl.BlockSpec((cf, h), lambda i: (i, 0)), pl.BlockSpec((cf, h), lambda i: (i, 0)),
                  _const_spec((FILT_BANDS, 1)), pl.BlockSpec((cf, 1), lambda i: (i, 0))],
        out_specs=pl.BlockSpec((cf, lk), lambda i: (i, 0)),
        out_shape=jax.ShapeDtypeStruct((HY_WIDTH, lk), F32),
        scratch_shapes=[pltpu.VMEM((h, lk), F32)],
        compiler_params=_cparams(("arbitrary",), 48),
        name="hyena_filter",
    )(w1t[:, 0:1], w1t[:, 1:1 + FILT_BANDS], w1t[:, 1 + FILT_BANDS:], col(fb1), col(ffr1), fw2.T, col(fb2),
      col(ffr2), w3t[:HY_WIDTH], w3t[HY_WIDTH:], bands, deltas)


def _hy_pre_kernel(x0_ref, x1_ref, v_ref, p0_ref, p1_ref, pv_ref, n0_ref, n1_ref, nv_ref,
                   scw_ref, scb_ref, skip_ref, zt_ref, a_ref, bt_ref, zs_ref):
    jt = pl.program_id(1)
    last = pl.num_programs(1) - 1
    nb_batch = x0_ref.shape[0]
    row = lax.broadcasted_iota(jnp.int32, (CONV_BLK, LANES), 0)
    keep_next = (jt < last).astype(F32)
    z_keep = jnp.logical_or(jt > 0, row >= CONV_BLK - N_META)

    def short_conv(cur_ref, prev_ref, next_ref, comp, b):
        cur = cur_ref[b].astype(F32)
        prev = prev_ref[b, 15:16, :].astype(F32)
        nxt = next_ref[b, 0:1, :].astype(F32) * keep_next
        up = jnp.where(row == 0, prev, pltpu.roll(cur, 1, 0))
        dn = jnp.where(row == CONV_BLK - 1, nxt, pltpu.roll(cur, CONV_BLK - 1, 0))
        w = scw_ref[comp]
        return up * w[0:1] + cur * w[1:2] + dn * w[2:3] + scb_ref[comp]

    def body(b, carry):
        x0c = short_conv(x0_ref, p0_ref, n0_ref, 0, b)
        x1c = short_conv(x1_ref, p1_ref, n1_ref, 1, b)
        vc = short_conv(v_ref, pv_ref, nv_ref, 2, b)
        z = jnp.where(z_keep, x1c * vc, 0.0)
        zs_ref[b] = z.T
        a_ref[b] = x0c.astype(BF16)
        bt_ref[b] = (x0c * z * skip_ref[...]).astype(BF16)
        return carry

    lax.fori_loop(0, nb_batch, body, 0)
    zt_ref[...] = pltpu.einshape("bcl->cbl", zs_ref[...]).astype(BF16)


def _hy_pre(hy_ext, scw, scb, skip, *, s_real):
    B = hy_ext.shape[0]
    nb = s_real // CONV_BLK
    ncb = HY_WIDTH // LANES
    sub = 16
    r16 = CONV_BLK // sub
    last16 = (s_real + HY_PAD) // sub - 1
    cur = lambda comp: pl.BlockSpec((B, CONV_BLK, LANES), lambda cb, jt: (0, jt + 1, comp * ncb + cb))
    prv = lambda comp: pl.BlockSpec((B, sub, LANES), lambda cb, jt: (0, r16 * (jt + 1) - 1, comp * ncb + cb))
    nxt = lambda comp: pl.BlockSpec(
        (B, sub, LANES), lambda cb, jt: (0, jnp.minimum(r16 * (jt + 2), last16), comp * ncb + cb))
    nat = pl.BlockSpec((B, CONV_BLK, LANES), lambda cb, jt: (0, jnp.maximum(jt - 1, 0), cb))
    scw_r = scw.reshape(HY_SHORT, 3, ncb, LANES).transpose(2, 1, 0, 3)
    scb_r = scb.reshape(3, ncb, 1, LANES).transpose(1, 0, 2, 3)
    skip_r = skip.reshape(ncb, 1, LANES)
    return pl.pallas_call(
        _hy_pre_kernel,
        grid=(ncb, nb + 1),
        in_specs=[cur(0), cur(1), cur(2), prv(0), prv(1), prv(2), nxt(0), nxt(1), nxt(2),
                  pl.BlockSpec((None, 3, HY_SHORT, LANES), lambda cb, jt: (cb, 0, 0, 0)),
                  pl.BlockSpec((None, 3, 1, LANES), lambda cb, jt: (cb, 0, 0, 0)),
                  pl.BlockSpec((None, 1, LANES), lambda cb, jt: (cb, 0, 0))],
        out_specs=[pl.BlockSpec((LANES, None, B, CONV_BLK), lambda cb, jt: (cb, jt, 0, 0)), nat, nat],
        out_shape=[jax.ShapeDtypeStruct((HY_WIDTH, nb + 1, B, CONV_BLK), BF16),
                   jax.ShapeDtypeStruct((B, s_real, HY_WIDTH), BF16),
                   jax.ShapeDtypeStruct((B, s_real, HY_WIDTH), BF16)],
        scratch_shapes=[pltpu.VMEM((B, LANES, CONV_BLK), F32)],
        compiler_params=_cparams(("parallel", "arbitrary"), 48),
        name="hyena_pre",
    )(hy_ext, hy_ext, hy_ext, hy_ext, hy_ext, hy_ext, hy_ext, hy_ext, hy_ext, scw_r, scb_r, skip_r)


def _hy_conv_kernel(kt_ref, z_ref, y_ref, g_ref, *, nb):
    cbk = z_ref.shape[0]
    B = z_ref.shape[2]
    s_real = nb * CONV_BLK
    nq = 2 * s_real // LANES + 1
    upper = (lax.broadcasted_iota(jnp.int32, (LANES, LANES), 1)
             >= lax.broadcasted_iota(jnp.int32, (LANES, LANES), 0))

    def channel(c, carry):
        kt_row = kt_ref[pl.ds(c, 1), :]
        for q in range(nq):
            off = 2 * s_real - LANES * q
            skew = lambda lo: pltpu.roll(jnp.broadcast_to(kt_row[:, lo:lo + LANES], (LANES, LANES)),
                                         0, 1, stride=1, stride_axis=0)
            blk_g = jnp.where(upper, skew(off + LANES), skew(off))
            g_ref[LANES * q:LANES * (q + 1), :] = blk_g.astype(BF16)
        y_ref[c] = jnp.zeros(y_ref.shape[1:], F32)
        for delta in range(-(nb - 1), nb + 1):
            base = CONV_BLK * (nb - delta)
            tile = jnp.concatenate([g_ref[base + LANES:base + LANES + CONV_BLK, :],
                                    g_ref[base:base + CONV_BLK, :]], axis=1)
            jb_lo = max(0, 1 - delta)
            n = min(nb, nb - delta) - jb_lo + 1
            o_lo = jb_lo + delta - 1
            lhs = z_ref[c, jb_lo:jb_lo + n].reshape(n * B, CONV_BLK)
            y_ref[c, o_lo:o_lo + n] += jnp.dot(lhs, tile, preferred_element_type=F32).reshape(n, B, CONV_BLK)
        return carry

    lax.fori_loop(0, cbk, channel, 0)


def _hy_conv(kt, zt, *, cbk=8):
    C, nb1, B, _ = zt.shape
    nb = nb1 - 1
    lk = kt.shape[1]
    return pl.pallas_call(
        functools.partial(_hy_conv_kernel, nb=nb),
        grid=(C // cbk,),
        in_specs=[pl.BlockSpec((cbk, lk), lambda i: (i, 0)),
                  pl.BlockSpec((cbk, nb1, B, CONV_BLK), lambda i: (i, 0, 0, 0))],
        out_specs=pl.BlockSpec((cbk, nb, B, CONV_BLK), lambda i: (i, 0, 0, 0)),
        out_shape=jax.ShapeDtypeStruct((C, nb, B, CONV_BLK), F32),
        scratch_shapes=[pltpu.VMEM((2 * nb * CONV_BLK + LANES, LANES), BF16)],
        compiler_params=_cparams(("parallel",), 48),
        name="hyena_conv",
    )(kt, zt)


def _hy_post_kernel(yt_ref, a_ref, bt_ref, o_ref, ys_ref):
    ys_ref[...] = pltpu.einshape("cbl->bcl", yt_ref[...])

    def body(b, carry):
        y = ys_ref[b].T
        o_ref[b] = (a_ref[b].astype(F32) * y + bt_ref[b].astype(F32)).astype(BF16)
        return carry

    lax.fori_loop(0, a_ref.shape[0], body, 0)


def _hy_post(yt, a, bt):
    C, nb, B, _ = yt.shape
    nat = pl.BlockSpec((B, CONV_BLK, LANES), lambda cb, ib: (0, ib, cb))
    return pl.pallas_call(
        _hy_post_kernel,
        grid=(C // LANES, nb),
        in_specs=[pl.BlockSpec((LANES, None, B, CONV_BLK), lambda cb, ib: (cb, ib, 0, 0)), nat, nat],
        out_specs=nat,
        out_shape=jax.ShapeDtypeStruct(a.shape, BF16),
        scratch_shapes=[pltpu.VMEM((B, LANES, CONV_BLK), F32)],
        compiler_params=_cparams(("parallel", "parallel"), 48),
        name="hyena_post",
    )(yt, a, bt)


def _attn_kernel(qt_ref, k_ref, vt_ref, km_ref, vtm_ref, bias_ref, bmeta_ref, lam_ref, sg_ref,
                 o_ref, m_ref, acc_ref, s_ref, bm_ref, *, blk):
    i = pl.program_id(2)
    nk = k_ref.shape[0] // blk
    assert nk % 2 == 0
    qt = qt_ref[...]
    rowi = lax.broadcasted_iota(jnp.int32, qt.shape, 0)
    zero = jnp.zeros_like(qt)
    qmaps = (jnp.where(rowi < ATT_HEAD_DIM, qt, zero), jnp.where(rowi >= ATT_HEAD_DIM, qt, zero))

    def with_ones(vt):
        return jnp.concatenate([vt, jnp.ones((ONES_ROWS, vt.shape[1]), BF16)], axis=0)

    kmeta, vtmeta, bias_m = km_ref[...], with_ones(vtm_ref[...]), bmeta_ref[jnp.minimum(i, 1)]
    for c in range(2):
        s = jnp.dot(kmeta, qmaps[c], preferred_element_type=F32) + bias_m
        m0 = jnp.max(s, axis=0, keepdims=True)
        m_ref[c] = m0
        acc_ref[c] = jnp.dot(vtmeta, jnp.exp2(s - m0).astype(BF16), preferred_element_type=F32)

    def scores(j, slot):
        kblk = k_ref[pl.ds(pl.multiple_of(j * blk, blk), blk), :]
        bias = bias_ref[jnp.clip(j - i, -2, 2) + 2]
        for c in range(2):
            s = jnp.dot(kblk, qmaps[c], preferred_element_type=F32) + bias
            s_ref[slot, c] = s
            bm_ref[slot, c] = jnp.max(s, axis=0, keepdims=True)

    def consume(j, slot):
        vtblk = with_ones(vt_ref[:, pl.ds(pl.multiple_of(j * blk, blk), blk)])
        for c in range(2):
            m_old = m_ref[c]
            m_new = jnp.maximum(m_old, bm_ref[slot, c])
            p = jnp.exp2(s_ref[slot, c] - m_new)
            acc_ref[c] = (jnp.exp2(m_old - m_new) * acc_ref[c]
                          + jnp.dot(vtblk, p.astype(BF16), preferred_element_type=F32))
            m_ref[c] = m_new

    scores(0, 0)

    def pair(jj, carry):
        j = 2 * jj
        scores(j + 1, 1)
        consume(j, 0)
        scores(j + 2, 0)
        consume(j + 1, 1)
        return carry

    lax.fori_loop(0, nk // 2 - 1, pair, 0)
    scores(nk - 1, 1)
    consume(nk - 2, 0)
    consume(nk - 1, 1)

    lq = lam_ref[...]
    lam = (jnp.exp(jnp.sum(lq[0:1] * lq[1:2], axis=1, keepdims=True))
           - jnp.exp(jnp.sum(lq[2:3] * lq[3:4], axis=1, keepdims=True)) + LAM_INIT)
    v_rows = slice(0, ATT_V_DIM)
    l_row = slice(ATT_V_DIM, ATT_V_DIM + 1)
    ot = acc_ref[0, v_rows] / acc_ref[0, l_row] - lam * (acc_ref[1, v_rows] / acc_ref[1, l_row])
    ot = ot * lax.rsqrt(jnp.mean(ot * ot, axis=0, keepdims=True) + RMS_EPS) * sg_ref[...] * (1.0 - LAM_INIT)
    o_ref[...] = ot.T.astype(BF16)


def _t5_bucket(rel):
    nb = N_BUCKETS // 2
    max_exact = nb // 2
    ret = jnp.where(rel > 0, nb, 0)
    n = jnp.abs(rel)
    nf = jnp.maximum(n, 1).astype(F32)
    large = max_exact + (jnp.log(nf / max_exact) / math.log(MAX_DISTANCE / max_exact)
                         * (nb - max_exact)).astype(jnp.int32)
    large = jnp.minimum(large, nb - 1)
    return ret + jnp.where(n < max_exact, n, large)


def _attn_bias_tables(rel_bias, blk):
    assert blk >= MAX_DISTANCE
    table = rel_bias.astype(F32) * LOG2E

    def bias_of(rel):
        bucket = _t5_bucket(rel)
        sel = bucket[None] == jnp.arange(N_BUCKETS, dtype=jnp.int32).reshape((N_BUCKETS,) + (1,) * rel.ndim)
        return jnp.stack([jnp.sum(jnp.where(sel, table[:, h].reshape((N_BUCKETS,) + (1,) * rel.ndim), 0.0), axis=0)
                          for h in range(ATT_HEADS)])

    r = jnp.arange(blk, dtype=jnp.int32)
    rel = jnp.stack([dj * blk + r[:, None] - r[None, :] for dj in (-2, -1, 0, 1, 2)])
    tiles = bias_of(rel)
    m = jnp.arange(N_META, dtype=jnp.int32)
    rel_m0 = m[:, None] - (N_META + r[None, :])
    rel_m1 = rel_m0 - blk
    bmeta = bias_of(jnp.stack([rel_m0, rel_m1]))
    return tiles, bmeta


def _attention(qt, k, vt, kmeta, vtmeta, tiles, bmeta, lam, subln_col, *, blk):
    B, S, _ = k.shape
    return pl.pallas_call(
        functools.partial(_attn_kernel, blk=blk),
        grid=(B, ATT_HEADS, S // blk),
        in_specs=[pl.BlockSpec((None, ATT_V_DIM, blk), lambda b, h, i: (b, h, i)),
                  pl.BlockSpec((None, S, ATT_V_DIM), lambda b, h, i: (b, 0, h)),
                  pl.BlockSpec((None, ATT_V_DIM, S), lambda b, h, i: (b, h, 0)),
                  pl.BlockSpec((N_META, ATT_V_DIM), lambda b, h, i: (0, h)),
                  pl.BlockSpec((ATT_V_DIM, N_META), lambda b, h, i: (h, 0)),
                  pl.BlockSpec((None, 5, blk, blk), lambda b, h, i: (h, 0, 0, 0)),
                  pl.BlockSpec((None, 2, N_META, blk), lambda b, h, i: (h, 0, 0, 0)),
                  pl.BlockSpec((4, ATT_HEAD_DIM), lambda b, h, i: (0, 0)),
                  pl.BlockSpec((ATT_V_DIM, 1), lambda b, h, i: (0, 0))],
        out_specs=pl.BlockSpec((None, blk, ATT_V_DIM), lambda b, h, i: (b, i, h)),
        scratch_shapes=[pltpu.VMEM((2, 1, blk), F32), pltpu.VMEM((2, ATT_V_DIM + ONES_ROWS, blk), F32),
                        pltpu.VMEM((2, 2, blk, blk), F32), pltpu.VMEM((2, 2, 1, blk), F32)],
        out_shape=jax.ShapeDtypeStruct((B, S, ATT_WIDTH), BF16),
        compiler_params=_cparams(("parallel", "parallel", "arbitrary"), 56),
        name="diff_attention",
    )(qt, k, vt, kmeta, vtmeta, tiles, bmeta, lam, subln_col)


def _merge_kernel(x_ref, ya_ref, yb_ref, sg_ref, pa_ref, pb_ref, wo_ref, g2_ref, wr2_ref, br_ref,
                  tri_ref, x1_ref, h2_ref, ti_ref, tg_ref, rk_ref, cnt_ref, lg_ref):
    ma = jnp.dot(ya_ref[...], pa_ref[...], preferred_element_type=F32)
    mb = jnp.dot(yb_ref[...], pb_ref[...], preferred_element_type=F32)
    m = sg_ref[:, :D_MODEL].astype(F32) * ma + sg_ref[:, D_MODEL:].astype(F32) * mb
    x1 = x_ref[...] + jnp.dot(m.astype(BF16), wo_ref[...], preferred_element_type=F32)
    x1_ref[...] = x1
    h2 = x1 * lax.rsqrt(jnp.mean(x1 * x1, axis=-1, keepdims=True) + RMS_EPS) * g2_ref[...]
    h2_hi = h2.astype(BF16)
    h2_ref[...] = _pack_rows(h2)
    h2_lo = (h2 - h2_hi.astype(F32)).astype(BF16)
    wr2 = wr2_ref[...]
    lg_ref[0] = jnp.dot(h2_hi, wr2, preferred_element_type=F32)
    lg_ref[1] = jnp.dot(h2_lo, wr2, preferred_element_type=F32)
    hi_t = lg_ref[0].T
    lo_t = lg_ref[1].T
    logits = hi_t[:N_EXPERTS] + hi_t[N_EXPERTS:] + lo_t[:N_EXPERTS] + br_ref[...]
    row = lax.broadcasted_iota(jnp.int32, logits.shape, 0)
    slot = lax.broadcasted_iota(jnp.int32, ti_ref.shape, 0)
    top_i = jnp.zeros(ti_ref.shape, jnp.int32)
    top_v = jnp.zeros(tg_ref.shape, F32)
    work = logits
    picks = []
    for kk in range(TOP_K):
        mx = jnp.max(work, axis=0, keepdims=True)
        idx = jnp.min(jnp.where(work == mx, row, N_EXPERTS), axis=0, keepdims=True)
        pick = row == idx
        picks.append(pick)
        top_i = jnp.where(slot == kk, idx, top_i)
        top_v = jnp.where(slot == kk, mx, top_v)
        work = jnp.where(pick, -jnp.inf, work)
    ex = jnp.exp(top_v - jnp.max(top_v, axis=0, keepdims=True))
    ti_ref[...] = top_i
    tg_ref[...] = ex / jnp.sum(ex, axis=0, keepdims=True)

    @pl.when(pl.program_id(0) == 0)
    def _():
        cnt_ref[...] = jnp.zeros_like(cnt_ref)

    chosen = jnp.where(work == -jnp.inf, 1.0, 0.0)
    before = cnt_ref[...] + jnp.dot(chosen.astype(BF16), tri_ref[...], preferred_element_type=F32)
    rank = jnp.zeros(rk_ref.shape, F32)
    for kk in range(TOP_K):
        rk = jnp.sum(jnp.where(picks[kk], before, 0.0), axis=0, keepdims=True)
        rank = jnp.where(slot == kk, rk, rank)
    rk_ref[...] = rank.astype(jnp.int32)
    cnt_ref[...] += jnp.sum(chosen, axis=1, keepdims=True)


def _merge(x, ya, yb, sg, pa, pb, wo, g2, wr_hi, wr_lo, br, *, tm):
    T, D = x.shape
    tok = lambda w: pl.BlockSpec((tm, w), lambda i: (i, 0))
    tok_t = pl.BlockSpec((TOP_K, tm), lambda i: (0, i))
    tri = jnp.triu(jnp.ones((tm, tm), F32), 1).astype(BF16)
    wr2 = jnp.concatenate([wr_hi, wr_lo], axis=1)
    return pl.pallas_call(
        _merge_kernel,
        grid=(T // tm,),
        in_specs=[tok(D), tok(HY_WIDTH), tok(ATT_WIDTH), tok(COL_G), _const_spec((HY_WIDTH, D)),
                  _const_spec((ATT_WIDTH, D)), _const_spec((D, D)), _const_spec((1, D)),
                  _const_spec((D, 2 * N_EXPERTS)), _const_spec((N_EXPERTS, 1)), _const_spec((tm, tm))],
        out_specs=[tok(D), tok(D // 2), tok_t, tok_t, tok_t, pl.BlockSpec((N_EXPERTS, 1), lambda i: (0, 0))],
        out_shape=[jax.ShapeDtypeStruct((T, D), F32), jax.ShapeDtypeStruct((T, D // 2), jnp.uint32),
                   jax.ShapeDtypeStruct((TOP_K, T), jnp.int32), jax.ShapeDtypeStruct((TOP_K, T), F32),
                   jax.ShapeDtypeStruct((TOP_K, T), jnp.int32), jax.ShapeDtypeStruct((N_EXPERTS, 1), F32)],
        scratch_shapes=[pltpu.VMEM((2, tm, 2 * N_EXPERTS), F32)],
        compiler_params=_cparams(("arbitrary",), 48),
        name="merge_router",
    )(x, ya, yb, sg, pa, pb, wo, g2, wr2, br.reshape(N_EXPERTS, 1), tri)


def _moe_kernel(iblk_ref, iexp_ref, start_ref, count_ref, nitem_ref, xs_ref, w1_ref, b1_ref, w2_ref, b2_ref, ys_ref,
                w1b_ref, w2b_ref):
    i = pl.program_id(0)
    tm = xs_ref.shape[0]

    @pl.when(jnp.logical_and(i < nitem_ref[0],
                             jnp.logical_or(i == 0, iexp_ref[jnp.maximum(i - 1, 0)] != iexp_ref[i])))
    def _():
        rows = 256
        for r in range(0, D_MODEL, rows):
            w1b_ref[r:r + rows, :] = w1_ref[r:r + rows, :].astype(BF16)
        for r in range(0, D_FF, rows):
            w2b_ref[r:r + rows, :] = w2_ref[r:r + rows, :].astype(BF16)

    @pl.when(i < nitem_ref[0])
    def _():
        blk = iblk_ref[i]
        e = iexp_ref[i]
        x = jnp.concatenate(_unpack_rows(xs_ref[...]), axis=1).astype(BF16)
        u = jnp.dot(x, w1b_ref[...], preferred_element_type=F32) + b1_ref[...]
        glu = jnp.minimum(u[:, :D_FF], SWIGLU_LIMIT)
        lin = jnp.clip(u[:, D_FF:], -SWIGLU_LIMIT, SWIGLU_LIMIT)
        a = glu * jax.nn.sigmoid(SWIGLU_ALPHA * glu) * (lin + 1.0)
        y = _pack_rows(jnp.dot(a.astype(BF16), w2b_ref[...], preferred_element_type=F32) + b2_ref[...])
        row = lax.broadcasted_iota(jnp.int32, (tm, 1), 0) + blk * tm
        mine = jnp.logical_and(row >= start_ref[e], row < start_ref[e] + count_ref[e])
        first_visit = jnp.logical_or(i == 0, iblk_ref[jnp.maximum(i - 1, 0)] != blk)

        @pl.when(first_visit)
        def _():
            ys_ref[...] = jnp.where(mine, y, jnp.zeros_like(y))

        @pl.when(jnp.logical_not(first_visit))
        def _():
            ys_ref[...] = jnp.where(mine, y, ys_ref[...])


def _moe_experts(item_blk, item_exp, start, count, n_items, xs, w1, b1, w2, b2, *, tm):
    A, D = xs.shape[0], xs.shape[1] * 2
    grid_spec = pltpu.PrefetchScalarGridSpec(
        num_scalar_prefetch=5,
        grid=(item_blk.shape[0],),
        in_specs=[pl.BlockSpec((tm, D // 2), lambda i, ib, ie, *_: (ib[i], 0)),
                  pl.BlockSpec((None, D, 2 * D_FF), lambda i, ib, ie, *_: (ie[i], 0, 0)),
                  pl.BlockSpec((None, 1, 2 * D_FF), lambda i, ib, ie, *_: (ie[i], 0, 0)),
                  pl.BlockSpec((None, D_FF, D), lambda i, ib, ie, *_: (ie[i], 0, 0)),
                  pl.BlockSpec((None, 1, D), lambda i, ib, ie, *_: (ie[i], 0, 0))],
        out_specs=pl.BlockSpec((tm, D // 2), lambda i, ib, ie, *_: (ib[i], 0)),
        scratch_shapes=[pltpu.VMEM((D, 2 * D_FF), BF16), pltpu.VMEM((D_FF, D), BF16)],
    )
    return pl.pallas_call(
        _moe_kernel,
        grid_spec=grid_spec,
        out_shape=jax.ShapeDtypeStruct((A, D // 2), jnp.uint32),
        compiler_params=_cparams(("arbitrary",), 56),
        name="moe_experts",
    )(item_blk, item_exp, start, count, n_items, xs, w1, b1, w2, b2)


def _combine_kernel(x1_ref, yg_ref, tg_ref, o_ref):
    half = D_MODEL // 2
    acc_lo = x1_ref[:, :half]
    acc_hi = x1_ref[:, half:]
    g = tg_ref[...]
    for kk in range(TOP_K):
        lo, hi = _unpack_rows(yg_ref[kk])
        acc_lo = acc_lo + g[:, kk:kk + 1] * lo
        acc_hi = acc_hi + g[:, kk:kk + 1] * hi
    o_ref[:, :half] = acc_lo
    o_ref[:, half:] = acc_hi


def _combine(x1, yg, tg, *, tm):
    T, D = x1.shape
    return pl.pallas_call(
        _combine_kernel,
        grid=(T // tm,),
        in_specs=[pl.BlockSpec((tm, D), lambda i: (i, 0)), pl.BlockSpec((TOP_K, tm, D // 2), lambda i: (0, i, 0)),
                  pl.BlockSpec((tm, TOP_K), lambda i: (i, 0))],
        out_specs=pl.BlockSpec((tm, D), lambda i: (i, 0)),
        out_shape=jax.ShapeDtypeStruct((T, D), F32),
        compiler_params=_cparams(("parallel",), 48),
        name="moe_combine",
    )(x1, yg, tg)


def _lookup(table, idx):
    n = table.shape[0]
    sel = idx[None] == jnp.arange(n, dtype=jnp.int32).reshape((n,) + (1,) * idx.ndim)
    return jnp.sum(jnp.where(sel, table.reshape((n,) + (1,) * idx.ndim), 0), axis=0)


def _route(top_i, rank, counts, *, tm):
    T = top_i.shape[1]
    A = T * TOP_K
    counts = counts.reshape(N_EXPERTS).astype(jnp.int32)
    start = jnp.cumsum(counts) - counts
    flat_e = top_i.reshape(A // LANES, LANES)
    slot_of = (_lookup(start, flat_e) + rank.reshape(A // LANES, LANES)).reshape(A)
    slot_tok = jnp.clip(_sc_invert_slots(slot_of, T), 0, T - 1)
    first_blk = start // tm
    n_it = jnp.where(counts > 0, (start + counts - 1) // tm - first_blk + 1, 0)
    it_end = jnp.cumsum(n_it)
    n_items = it_end[-1]
    n_max = A // tm + N_EXPERTS
    i = jnp.minimum(jnp.arange(n_max, dtype=jnp.int32), n_items - 1)
    item_exp = jnp.sum(it_end[None, :] <= i[:, None], axis=1, dtype=jnp.int32)
    item_blk = _lookup(first_blk, item_exp) + i - _lookup(it_end - n_it, item_exp)
    return slot_tok, slot_of, item_blk.astype(jnp.int32), item_exp, start, counts, n_items.reshape(1)


def _trunk(x, meta_proj, w, tabs, *, tm_prep, tm_tok, tm_moe, attn_blk):
    B, S, D = x.shape
    T = B * S
    hy_m, _, k_m, vt_m, _ = meta_proj
    assert tm_prep == HY_PAD
    front = jnp.concatenate([jnp.zeros((HY_PAD - N_META, COL_HY), BF16), hy_m[0]], axis=0)
    hy_ext, qt, k, vt, sg = _prep(x, front, w['g1'], w['w_in'], w['wqt'], w['wvt'], w['bd'], w['qg'], w['kg'],
                                  tm=tm_prep)
    kt = _hyena_filter(w['fw1'], w['fb1'], w['ffr1'], w['fw2'], w['fb2'], w['ffr2'], w['fw3'], s_real=S)
    zt, a, bt = _hy_pre(hy_ext, w['scw'], w['scb'], w['skip'], s_real=S)
    ya = _hy_post(_hy_conv(kt, zt), a, bt)
    tiles, bmeta = tabs
    yb = _attention(qt, k, vt, k_m[0], vt_m[0], tiles, bmeta, w['lam'], w['subln'], blk=attn_blk)
    x1, h2, top_i, top_g, rank, counts = _merge(
        x.reshape(T, D), ya.reshape(T, HY_WIDTH), yb.reshape(T, ATT_WIDTH), sg.reshape(T, COL_G),
        w['pa'], w['pb'], w['wo'], w['g2'], w['wr_hi'], w['wr_lo'], w['br'], tm=tm_tok)
    slot_tok, slot_of, item_blk, item_exp, start, cnt, n_items = _route(top_i, rank, counts, tm=tm_moe)
    xs = _sc_gather_rows(h2, slot_tok)
    ys = _moe_experts(item_blk, item_exp, start, cnt, n_items, xs, w['w1'], w['b1'], w['w2'], w['b2'], tm=tm_moe)
    yg = _sc_gather_rows(ys, slot_of).reshape(TOP_K, T, D // 2)
    return _combine(x1, yg, top_g.T, tm=tm_tok).reshape(B, S, D)


def kernel(x_prompt, x_sample, meta_tokens, rel_bias, norm1_g, w_in, short_conv_w, short_conv_b, filt_w1, filt_b1,
           filt_freq1, filt_w2, filt_b2, filt_freq2, filt_w3, hy_skip, proj_a, q_norm_g, k_norm_g, lambda_q1,
           lambda_k1, lambda_q2, lambda_k2, subln_g, proj_b, w_out, norm2_g, w_router, b_router, w_mlp1, b_mlp1,
           w_mlp2, b_mlp2):
    l = 0
    n_hm = COL_Q // ATT_HEAD_DIM
    w_in_bf = w_in[l].astype(BF16)
    c_q, c_v = COL_HY, COL_HY + COL_Q + COL_K
    wr_hi = w_router[l].astype(BF16)
    w = {
        'g1': norm1_g[l][None], 'w_in': w_in_bf,
        'wqt': w_in_bf[:, c_q:c_q + COL_Q].T, 'wvt': w_in_bf[:, c_v:c_v + COL_V].T,
        'bd': jnp.kron(jnp.eye(n_hm, dtype=F32), jnp.full((ATT_HEAD_DIM, ATT_HEAD_DIM), 1.0 / ATT_HEAD_DIM, F32)
                       ).astype(BF16),
        'qg': (jnp.tile(q_norm_g[l], n_hm) * (ATT_HEAD_DIM ** -0.5 * LOG2E))[:, None],
        'kg': jnp.tile(k_norm_g[l], n_hm)[None],
        'wr_hi': wr_hi, 'wr_lo': (w_router[l] - wr_hi.astype(F32)).astype(BF16),
        'fw1': filt_w1[l], 'fb1': filt_b1[l], 'ffr1': filt_freq1[l], 'fw2': filt_w2[l], 'fb2': filt_b2[l],
        'ffr2': filt_freq2[l], 'fw3': filt_w3[l],
        'scw': short_conv_w[l], 'scb': short_conv_b[l], 'skip': hy_skip[l],
        'lam': jnp.stack([lambda_q1[l], lambda_k1[l], lambda_q2[l], lambda_k2[l]]), 'subln': subln_g[l][:, None],
        'pa': proj_a[l].astype(BF16), 'pb': proj_b[l].astype(BF16), 'wo': w_out[l].astype(BF16),
        'g2': norm2_g[l][None], 'br': b_router[l][None],
        'w1': w_mlp1[l], 'b1': b_mlp1[l][:, None, :], 'w2': w_mlp2[l], 'b2': b_mlp2[l][:, None, :],
    }
    meta_proj = _prep(meta_tokens[None], None, w['g1'], w['w_in'], w['wqt'], w['wvt'], w['bd'], w['qg'], w['kg'],
                      tm=N_META)
    tabs = _attn_bias_tables(rel_bias, TILES['attn_blk'])
    y_prompt = _trunk(x_prompt, meta_proj, w, tabs, **TILES)
    y_sample = _trunk(x_sample, meta_proj, w, tabs, **TILES)
    return (y_prompt, y_sample)
```

```python
import functools
import math

import jax
import jax.numpy as jnp
import numpy as np
from jax import lax
from jax.experimental import pallas as pl
from jax.experimental.pallas import tpu as pltpu
from jax.experimental.pallas import tpu_sc as plsc

F32 = jnp.float32
BF16 = jnp.bfloat16

D_MODEL = 1024
N_META = 16
RMS_EPS = 1e-6
HY_WIDTH = 512
HY_SHORT = 3
FILT_BANDS = 16
FILT_EMB = 1 + 2 * FILT_BANDS
FILT_HIDDEN = 64
DECAY_TARGET = 1e-2
FAST_DECAY_PCT = 0.3
SLOW_DECAY_PCT = 1.5
ATT_HEADS = 4
ATT_HEAD_DIM = 64
ATT_V_DIM = 128
ATT_WIDTH = 512
N_BUCKETS = 32
MAX_DISTANCE = 128
N_EXPERTS = 32
TOP_K = 4
D_FF = 1024
SWIGLU_LIMIT = 7.0
SWIGLU_ALPHA = 1.702
COL_HY = 3 * HY_WIDTH
COL_Q = 512
COL_K = 512
COL_V = 512
COL_G = 2 * D_MODEL
IN_COLS = COL_HY + COL_Q + COL_K + COL_V + COL_G
LAM_INIT = 0.8 - 0.6 * math.exp(-0.3 * 0)
LOG2E = 1.4426950408889634
ONES_ROWS = 16

V7X_VMEM_BYTES = 64 * 1024 * 1024
LANES = 128
CONV_BLK = 256
HY_PAD = 512
TILES = dict(tm_prep=512, tm_tok=512, tm_moe=512, attn_blk=512)


def _cparams(sem, vmem_mb):
    return pltpu.CompilerParams(dimension_semantics=sem, vmem_limit_bytes=vmem_mb * 1024 * 1024)


def _const_spec(shape):
    nd = len(shape)
    return pl.BlockSpec(shape, lambda *_: (0,) * nd, pipeline_mode=pl.Buffered(1))


def _pack_rows(x):
    n = x.shape[1] // 2
    lo = pltpu.bitcast(x[:, :n].astype(BF16).astype(F32), jnp.uint32)
    hi = pltpu.bitcast(x[:, n:].astype(BF16).astype(F32), jnp.uint32)
    return (hi & jnp.uint32(0xFFFF0000)) | (lo >> 16)


def _unpack_rows(w):
    lo = pltpu.bitcast(w << 16, F32)
    hi = pltpu.bitcast(w & jnp.uint32(0xFFFF0000), F32)
    return lo, hi


SC_ROWS = 64


def _sc_workers():
    info = plsc.get_sparse_core_info()
    return info, info.num_cores * info.num_subcores


def _sc_stream_rows(table_hbm, out_hbm, rows_v, sems, base, n_chunks, load_index, index_ref):
    def rows_copy(j, buf):
        return pltpu.make_async_copy(table_hbm.at[index_ref(j, buf)], rows_v.at[buf], sems.at[buf])

    def fetch(j, buf):
        load_index(j, buf)
        rows_copy(j, buf).start()

    def drain(j, buf):
        rows_copy(j, buf).wait()
        pltpu.sync_copy(rows_v.at[buf], out_hbm.at[pl.ds(pl.multiple_of(base + j * SC_ROWS, SC_ROWS), SC_ROWS)])

    fetch(0, 0)

    @pl.loop(0, n_chunks // 2)
    def _(jj):
        j = 2 * jj
        fetch(j + 1, 1)
        drain(j, 0)

        @pl.when(j + 2 < n_chunks)
        def _():
            fetch(j + 2, 0)

        drain(j + 1, 1)


def _sc_sorted_rows(table, slot_of, *, scan=16384):
    info, n_workers = _sc_workers()
    lanes = info.num_lanes
    n_tok, width = table.shape
    n_picks = slot_of.shape[0]
    per_worker = n_picks // n_workers
    n_chunks = per_worker // SC_ROWS
    assert n_tok % scan == 0 and n_picks % scan == 0 and scan % lanes == 0
    assert n_chunks * SC_ROWS * n_workers == n_picks and n_chunks % 2 == 0
    mesh = plsc.VectorSubcoreMesh(core_axis_name="c", subcore_axis_name="s")

    @functools.partial(
        pl.kernel, mesh=mesh, out_type=jax.ShapeDtypeStruct((n_picks, width), table.dtype),
        scratch_types=[pltpu.VMEM((scan,), jnp.int32), pltpu.VMEM((per_worker,), jnp.int32),
                       pltpu.VMEM((2, SC_ROWS, width), table.dtype), pltpu.SemaphoreType.DMA((2,))],
        compiler_params=pltpu.CompilerParams(needs_layout_passes=False))
    def sorted_rows(table_hbm, slot_hbm, out_hbm, picks_v, tok_v, rows_v, sems):
        base = (lax.axis_index("s") * info.num_cores + lax.axis_index("c")) * per_worker
        lane = lax.iota(jnp.int32, lanes)

        @pl.loop(0, per_worker // lanes)
        def _(i):
            tok_v[pl.ds(i * lanes, lanes)] = jnp.zeros((lanes,), jnp.int32)

        @pl.loop(0, n_picks // scan)
        def _(ci):
            a0 = pl.multiple_of(ci * scan, scan)
            pltpu.sync_copy(slot_hbm.at[pl.ds(a0, scan)], picks_v)
            t0 = lax.rem(a0, n_tok)

            @pl.loop(0, scan // lanes)
            def _(i):
                row = picks_v[pl.ds(i * lanes, lanes)] - base
                mine = jnp.logical_and(row >= 0, row < per_worker)
                plsc.store_scatter(tok_v, [jnp.where(mine, row, 0)], t0 + i * lanes + lane, mask=mine)

        _sc_stream_rows(table_hbm, out_hbm, rows_v, sems, base, n_chunks,
                        load_index=lambda j, buf: None,
                        index_ref=lambda j, buf: tok_v.at[pl.ds(pl.multiple_of(j * SC_ROWS, SC_ROWS), SC_ROWS)])

    return sorted_rows(table, slot_of)


def _sc_gather_rows(table, idx):
    info, n_workers = _sc_workers()
    n_rows, width = idx.shape[0], table.shape[1]
    per_worker = n_rows // n_workers
    n_chunks = per_worker // SC_ROWS
    assert n_chunks * SC_ROWS * n_workers == n_rows and n_chunks % 2 == 0
    mesh = plsc.VectorSubcoreMesh(core_axis_name="c", subcore_axis_name="s")

    @functools.partial(
        pl.kernel, mesh=mesh, out_type=jax.ShapeDtypeStruct((n_rows, width), table.dtype),
        scratch_types=[pltpu.VMEM((2, SC_ROWS), jnp.int32), pltpu.VMEM((2, SC_ROWS, width), table.dtype),
                       pltpu.SemaphoreType.DMA((2,))])
    def gather(table_hbm, idx_hbm, out_hbm, idx_v, rows_v, sems):
        base = (lax.axis_index("s") * info.num_cores + lax.axis_index("c")) * per_worker

        def load_index(j, buf):
            off = pl.multiple_of(base + j * SC_ROWS, SC_ROWS)
            pltpu.sync_copy(idx_hbm.at[pl.ds(off, SC_ROWS)], idx_v.at[buf])

        _sc_stream_rows(table_hbm, out_hbm, rows_v, sems, base, n_chunks,
                        load_index=load_index, index_ref=lambda j, buf: idx_v.at[buf])

    return gather(table, idx)


def _prep_kernel(x_ref, front_ref, g1_ref, w_ref, wqt_ref, wvt_ref, bd_ref, qg_ref, kg_ref,
                 hy_ref, qt_ref, k_ref, vt_ref, sg_ref, *, lead):
    if lead:
        @pl.when(pl.program_id(1) == 0)
        def _():
            hy_ref[...] = front_ref[...]

        @pl.when(pl.program_id(1) > 0)
        def _():
            _prep_tile(x_ref, g1_ref, w_ref, wqt_ref, wvt_ref, bd_ref, qg_ref, kg_ref,
                       hy_ref, qt_ref, k_ref, vt_ref, sg_ref)
    else:
        _prep_tile(x_ref, g1_ref, w_ref, wqt_ref, wvt_ref, bd_ref, qg_ref, kg_ref,
                   hy_ref, qt_ref, k_ref, vt_ref, sg_ref)


def _prep_tile(x_ref, g1_ref, w_ref, wqt_ref, wvt_ref, bd_ref, qg_ref, kg_ref,
               hy_ref, qt_ref, k_ref, vt_ref, sg_ref):
    x = x_ref[...]
    ms = jnp.mean(x * x, axis=-1, keepdims=True)
    h = (x * lax.rsqrt(ms + RMS_EPS) * g1_ref[...]).astype(BF16)
    nt_dims = (((1,), (1,)), ((), ()))

    def proj(lo, hi):
        return jnp.dot(h, w_ref[:, lo:hi], preferred_element_type=F32)

    def proj_t(wt_ref):
        return lax.dot_general(wt_ref[...], h, nt_dims, preferred_element_type=F32)

    c0, c1, c2, c3 = COL_HY, COL_HY + COL_Q, COL_HY + COL_Q + COL_K, COL_HY + COL_Q + COL_K + COL_V
    hy_ref[...] = proj(0, c0).astype(BF16)
    qt = proj_t(wqt_ref)
    q3 = qt.reshape(COL_Q // ATT_HEAD_DIM, ATT_HEAD_DIM, qt.shape[1])
    q3 = q3 * lax.rsqrt(jnp.mean(q3 * q3, axis=1, keepdims=True) + RMS_EPS)
    qt_ref[...] = (q3.reshape(qt.shape) * qg_ref[...]).astype(BF16)
    kk = proj(c1, c2)
    msk = jnp.dot((kk * kk).astype(BF16), bd_ref[...], preferred_element_type=F32)
    k_ref[...] = (kk * lax.rsqrt(msk + RMS_EPS) * kg_ref[...]).astype(BF16)
    vt_ref[...] = proj_t(wvt_ref).astype(BF16)
    sg_ref[...] = jax.nn.sigmoid(proj(c3, IN_COLS)).astype(BF16)


def _prep(x, front, g1, w_in_bf, wqt, wvt, bd, qg_col, kg, *, tm):
    B, S, D = x.shape
    nt = S // tm
    lead = 0 if front is None else 1
    if front is None:
        front = jnp.zeros((tm, COL_HY), BF16)
    src = lambda i: jnp.maximum(i - lead, 0)
    tok = lambda w: pl.BlockSpec((None, tm, w), lambda b, i: (b, src(i), 0))
    tok_t = lambda w: pl.BlockSpec((None, w, tm), lambda b, i: (b, 0, src(i)))
    return pl.pallas_call(
        functools.partial(_prep_kernel, lead=lead),
        grid=(B, nt + lead),
        in_specs=[tok(D), _const_spec((tm, COL_HY)), _const_spec((1, D)), _const_spec((D, IN_COLS)),
                  _const_spec((COL_Q, D)), _const_spec((COL_V, D)), _const_spec((COL_Q, COL_Q)),
                  _const_spec((COL_Q, 1)), _const_spec((1, COL_K))],
        out_specs=[pl.BlockSpec((None, tm, COL_HY), lambda b, i: (b, i, 0)),
                   tok_t(COL_Q), tok(COL_K), tok_t(COL_V), tok(COL_G)],
        out_shape=[jax.ShapeDtypeStruct((B, S + lead * tm, COL_HY), BF16),
                   jax.ShapeDtypeStruct((B, COL_Q, S), BF16), jax.ShapeDtypeStruct((B, S, COL_K), BF16),
                   jax.ShapeDtypeStruct((B, COL_V, S), BF16), jax.ShapeDtypeStruct((B, S, COL_G), BF16)],
        compiler_params=_cparams(("parallel", "arbitrary"), 48),
        name="prep",
    )(x, front, g1, w_in_bf, wqt, wvt, bd, qg_col, kg)


def _filt_kernel(w1t_ref, w1c_ref, w1s_ref, b1_ref, fr1_ref, w2t_ref, b2_ref, fr2_ref, w3f_ref, w3b_ref,
                 bands_ref, deltas_ref, kt_ref, hid_ref, *, n_seq, s_real):
    lk = kt_ref.shape[1]
    hi = lax.Precision.HIGHEST
    u = lax.broadcasted_iota(jnp.int32, (1, lk), 1)
    d = u - s_real
    pos = jnp.abs(d).astype(F32)
    t = pos / float(max(n_seq - 1, 1))

    @pl.when(pl.program_id(0) == 0)
    def _():
        w = 2.0 * math.pi * pos / float(n_seq)
        ang = bands_ref[...] * w
        pre = (w1t_ref[...] * t
               + jnp.dot(w1c_ref[...], jnp.cos(ang), precision=hi, preferred_element_type=F32)
               - jnp.dot(w1s_ref[...], jnp.sin(ang), precision=hi, preferred_element_type=F32)
               + b1_ref[...])
        h1 = jnp.sin(fr1_ref[...] * pre)
        h2 = jnp.sin(fr2_ref[...] * (jnp.dot(w2t_ref[...], h1, precision=hi, preferred_element_type=F32)
                                     + b2_ref[...]))
        hid_ref[...] = h2

    h2 = hid_ref[...]
    hf = jnp.dot(w3f_ref[...], h2, precision=hi, preferred_element_type=F32)
    hb = jnp.dot(w3b_ref[...], h2, precision=hi, preferred_element_type=F32)
    decay = jnp.exp(-t * deltas_ref[...])
    valid = pos <= float(n_seq - 1)
    kun = jnp.where(valid, jnp.where(d >= 0, hf, hb) * decay, 0.0)
    nrm = lax.rsqrt(jnp.sum(kun * kun, axis=1, keepdims=True) + RMS_EPS)
    kt_ref[...] = kun * nrm


def _hyena_filter(fw1, fb1, ffr1, fw2, fb2, ffr2, fw3, *, s_real):
    n_seq = s_real + N_META
    lk = 2 * s_real + CONV_BLK
    cf = 64
    col = lambda a: a.reshape(-1, 1)
    w1t = fw1.T
    bands = jnp.linspace(1e-4, FILT_BANDS - 1, FILT_BANDS, dtype=F32).reshape(-1, 1)
    max_decay = math.log(DECAY_TARGET) / FAST_DECAY_PCT
    min_decay = math.log(DECAY_TARGET) / SLOW_DECAY_PCT
    deltas = jnp.abs(jnp.linspace(min_decay, max_decay, HY_WIDTH, dtype=F32)).reshape(-1, 1)
    w3t = fw3.T
    h = FILT_HIDDEN
    return pl.pallas_call(
        functools.partial(_filt_kernel, n_seq=n_seq, s_real=s_real),
        grid=(HY_WIDTH // cf,),
        in_specs=[_const_spec((h, 1)), _const_spec((h, FILT_BANDS)), _const_spec((h, FILT_BANDS)),
                  _const_spec((h, 1)), _const_spec((h, 1)), _const_spec((h, h)), _const_spec((h, 1)),
                  _const_spec((h, 1)),
                  pl.BlockSpec((cf, h), lambda i: (i, 0)), pl.BlockSpec((cf, h), lambda i: (i, 0)),
                  _const_spec((FILT_BANDS, 1)), pl.BlockSpec((cf, 1), lambda i: (i, 0))],
        out_specs=pl.BlockSpec((cf, lk), lambda i: (i, 0)),
        out_shape=jax.ShapeDtypeStruct((HY_WIDTH, lk), F32),
        scratch_shapes=[pltpu.VMEM((h, lk), F32)],
        compiler_params=_cparams(("arbitrary",), 48),
        name="hyena_filter",
    )(w1t[:, 0:1], w1t[:, 1:1 + FILT_BANDS], w1t[:, 1 + FILT_BANDS:], col(fb1), col(ffr1), fw2.T, col(fb2),
      col(ffr2), w3t[:HY_WIDTH], w3t[HY_WIDTH:], bands, deltas)


def _hy_pre_kernel(x0_ref, x1_ref, v_ref, p0_ref, p1_ref, pv_ref, n0_ref, n1_ref, nv_ref,
                   scw_ref, scb_ref, skip_ref, zt_ref, a_ref, bt_ref, zs_ref):
    jt = pl.program_id(1)
    last = pl.num_programs(1) - 1
    nb_batch = x0_ref.shape[0]
    row = lax.broadcasted_iota(jnp.int32, (CONV_BLK, LANES), 0)
    keep_next = (jt < last).astype(F32)
    z_keep = jnp.logical_or(jt > 0, row >= CONV_BLK - N_META)

    def short_conv(cur_ref, prev_ref, next_ref, comp, b):
        cur = cur_ref[b].astype(F32)
        prev = prev_ref[b, 15:16, :].astype(F32)
        nxt = next_ref[b, 0:1, :].astype(F32) * keep_next
        up = jnp.where(row == 0, prev, pltpu.roll(cur, 1, 0))
        dn = jnp.where(row == CONV_BLK - 1, nxt, pltpu.roll(cur, CONV_BLK - 1, 0))
        w = scw_ref[comp]
        return up * w[0:1] + cur * w[1:2] + dn * w[2:3] + scb_ref[comp]

    def body(b, carry):
        x0c = short_conv(x0_ref, p0_ref, n0_ref, 0, b)
        x1c = short_conv(x1_ref, p1_ref, n1_ref, 1, b)
        vc = short_conv(v_ref, pv_ref, nv_ref, 2, b)
        z = jnp.where(z_keep, x1c * vc, 0.0)
        zs_ref[b] = z.T
        a_ref[b] = x0c.astype(BF16)
        bt_ref[b] = (x0c * z * skip_ref[...]).astype(BF16)
        return carry

    lax.fori_loop(0, nb_batch, body, 0)
    zt_ref[...] = pltpu.einshape("bcl->cbl", zs_ref[...]).astype(BF16)


def _hy_pre(hy_ext, scw, scb, skip, *, s_real):
    B = hy_ext.shape[0]
    nb = s_real // CONV_BLK
    ncb = HY_WIDTH // LANES
    sub = 16
    r16 = CONV_BLK // sub
    last16 = (s_real + HY_PAD) // sub - 1
    cur = lambda comp: pl.BlockSpec((B, CONV_BLK, LANES), lambda cb, jt: (0, jt + 1, comp * ncb + cb))
    prv = lambda comp: pl.BlockSpec((B, sub, LANES), lambda cb, jt: (0, r16 * (jt + 1) - 1, comp * ncb + cb))
    nxt = lambda comp: pl.BlockSpec(
        (B, sub, LANES), lambda cb, jt: (0, jnp.minimum(r16 * (jt + 2), last16), comp * ncb + cb))
    nat = pl.BlockSpec((B, CONV_BLK, LANES), lambda cb, jt: (0, jnp.maximum(jt - 1, 0), cb))
    scw_r = scw.reshape(HY_SHORT, 3, ncb, LANES).transpose(2, 1, 0, 3)
    scb_r = scb.reshape(3, ncb, 1, LANES).transpose(1, 0, 2, 3)
    skip_r = skip.reshape(ncb, 1, LANES)
    return pl.pallas_call(
        _hy_pre_kernel,
        grid=(ncb, nb + 1),
        in_specs=[cur(0), cur(1), cur(2), prv(0), prv(1), prv(2), nxt(0), nxt(1), nxt(2),
                  pl.BlockSpec((None, 3, HY_SHORT, LANES), lambda cb, jt: (cb, 0, 0, 0)),
                  pl.BlockSpec((None, 3, 1, LANES), lambda cb, jt: (cb, 0, 0, 0)),
                  pl.BlockSpec((None, 1, LANES), lambda cb, jt: (cb, 0, 0))],
        out_specs=[pl.BlockSpec((LANES, None, B, CONV_BLK), lambda cb, jt: (cb, jt, 0, 0)), nat, nat],
        out_shape=[jax.ShapeDtypeStruct((HY_WIDTH, nb + 1, B, CONV_BLK), BF16),
                   jax.ShapeDtypeStruct((B, s_real, HY_WIDTH), BF16),
                   jax.ShapeDtypeStruct((B, s_real, HY_WIDTH), BF16)],
        scratch_shapes=[pltpu.VMEM((B, LANES, CONV_BLK), F32)],
        compiler_params=_cparams(("parallel", "arbitrary"), 48),
        name="hyena_pre",
    )(hy_ext, hy_ext, hy_ext, hy_ext, hy_ext, hy_ext, hy_ext, hy_ext, hy_ext, scw_r, scb_r, skip_r)


def _hy_conv_kernel(kt_ref, z_ref, y_ref, g_ref, *, nb):
    cbk = z_ref.shape[0]
    B = z_ref.shape[2]
    s_real = nb * CONV_BLK
    nq = 2 * s_real // LANES + 1
    upper = (lax.broadcasted_iota(jnp.int32, (LANES, LANES), 1)
             >= lax.broadcasted_iota(jnp.int32, (LANES, LANES), 0))

    def channel(c, carry):
        kt_row = kt_ref[pl.ds(c, 1), :]
        for q in range(nq):
            off = 2 * s_real - LANES * q
            skew = lambda lo: pltpu.roll(jnp.broadcast_to(kt_row[:, lo:lo + LANES], (LANES, LANES)),
                                         0, 1, stride=1, stride_axis=0)
            blk_g = jnp.where(upper, skew(off + LANES), skew(off))
            g_ref[LANES * q:LANES * (q + 1), :] = blk_g.astype(BF16)
        y_ref[c] = jnp.zeros(y_ref.shape[1:], F32)
        for delta in range(-(nb - 1), nb + 1):
            base = CONV_BLK * (nb - delta)
            tile = jnp.concatenate([g_ref[base + LANES:base + LANES + CONV_BLK, :],
                                    g_ref[base:base + CONV_BLK, :]], axis=1)
            jb_lo = max(0, 1 - delta)
            n = min(nb, nb - delta) - jb_lo + 1
            o_lo = jb_lo + delta - 1
            lhs = z_ref[c, jb_lo:jb_lo + n].reshape(n * B, CONV_BLK)
            y_ref[c, o_lo:o_lo + n] += jnp.dot(lhs, tile, preferred_element_type=F32).reshape(n, B, CONV_BLK)
        return carry

    lax.fori_loop(0, cbk, channel, 0)


def _hy_conv(kt, zt, *, cbk=8):
    C, nb1, B, _ = zt.shape
    nb = nb1 - 1
    lk = kt.shape[1]
    return pl.pallas_call(
        functools.partial(_hy_conv_kernel, nb=nb),
        grid=(C // cbk,),
        in_specs=[pl.BlockSpec((cbk, lk), lambda i: (i, 0)),
                  pl.BlockSpec((cbk, nb1, B, CONV_BLK), lambda i: (i, 0, 0, 0))],
        out_specs=pl.BlockSpec((cbk, nb, B, CONV_BLK), lambda i: (i, 0, 0, 0)),
        out_shape=jax.ShapeDtypeStruct((C, nb, B, CONV_BLK), F32),
        scratch_shapes=[pltpu.VMEM((2 * nb * CONV_BLK + LANES, LANES), BF16)],
        compiler_params=_cparams(("parallel",), 48),
        name="hyena_conv",
    )(kt, zt)


def _hy_post_kernel(yt_ref, a_ref, bt_ref, o_ref, ys_ref):
    ys_ref[...] = pltpu.einshape("cbl->bcl", yt_ref[...])

    def body(b, carry):
        y = ys_ref[b].T
        o_ref[b] = (a_ref[b].astype(F32) * y + bt_ref[b].astype(F32)).astype(BF16)
        return carry

    lax.fori_loop(0, a_ref.shape[0], body, 0)


def _hy_post(yt, a, bt):
    C, nb, B, _ = yt.shape
    nat = pl.BlockSpec((B, CONV_BLK, LANES), lambda cb, ib: (0, ib, cb))
    return pl.pallas_call(
        _hy_post_kernel,
        grid=(C // LANES, nb),
        in_specs=[pl.BlockSpec((LANES, None, B, CONV_BLK), lambda cb, ib: (cb, ib, 0, 0)), nat, nat],
        out_specs=nat,
        out_shape=jax.ShapeDtypeStruct(a.shape, BF16),
        scratch_shapes=[pltpu.VMEM((B, LANES, CONV_BLK), F32)],
        compiler_params=_cparams(("parallel", "parallel"), 48),
        name="hyena_post",
    )(yt, a, bt)


def _attn_kernel(qt_ref, k_ref, vt_ref, km_ref, vtm_ref, bias_ref, bmeta_ref, lam_ref, sg_ref,
                 o_ref, m_ref, acc_ref, s_ref, bm_ref, *, blk):
    i = pl.program_id(2)
    nk = k_ref.shape[0] // blk
    assert nk % 2 == 0
    qt = qt_ref[...]
    rowi = lax.broadcasted_iota(jnp.int32, qt.shape, 0)
    zero = jnp.zeros_like(qt)
    qmaps = (jnp.where(rowi < ATT_HEAD_DIM, qt, zero), jnp.where(rowi >= ATT_HEAD_DIM, qt, zero))

    def with_ones(vt):
        return jnp.concatenate([vt, jnp.ones((ONES_ROWS, vt.shape[1]), BF16)], axis=0)

    kmeta, vtmeta, bias_m = km_ref[...], with_ones(vtm_ref[...]), bmeta_ref[jnp.minimum(i, 1)]
    for c in range(2):
        s = jnp.dot(kmeta, qmaps[c], preferred_element_type=F32) + bias_m
        m0 = jnp.max(s, axis=0, keepdims=True)
        m_ref[c] = m0
        acc_ref[c] = jnp.dot(vtmeta, jnp.exp2(s - m0).astype(BF16), preferred_element_type=F32)

    def scores(j, slot):
        kblk = k_ref[pl.ds(pl.multiple_of(j * blk, blk), blk), :]
        bias = bias_ref[jnp.clip(j - i, -2, 2) + 2]
        for c in range(2):
            s = jnp.dot(kblk, qmaps[c], preferred_element_type=F32) + bias
            s_ref[slot, c] = s
            bm_ref[slot, c] = jnp.max(s, axis=0, keepdims=True)

    def consume(j, slot):
        vtblk = with_ones(vt_ref[:, pl.ds(pl.multiple_of(j * blk, blk), blk)])
        for c in range(2):
            m_old = m_ref[c]
            m_new = jnp.maximum(m_old, bm_ref[slot, c])
            p = jnp.exp2(s_ref[slot, c] - m_new)
            acc_ref[c] = (jnp.exp2(m_old - m_new) * acc_ref[c]
                          + jnp.dot(vtblk, p.astype(BF16), preferred_element_type=F32))
            m_ref[c] = m_new

    scores(0, 0)

    def pair(jj, carry):
        j = 2 * jj
        scores(j + 1, 1)
        consume(j, 0)
        scores(j + 2, 0)
        consume(j + 1, 1)
        return carry

    lax.fori_loop(0, nk // 2 - 1, pair, 0)
    scores(nk - 1, 1)
    consume(nk - 2, 0)
    consume(nk - 1, 1)

    lq = lam_ref[...]
    lam = (jnp.exp(jnp.sum(lq[0:1] * lq[1:2], axis=1, keepdims=True))
           - jnp.exp(jnp.sum(lq[2:3] * lq[3:4], axis=1, keepdims=True)) + LAM_INIT)
    v_rows = slice(0, ATT_V_DIM)
    l_row = slice(ATT_V_DIM, ATT_V_DIM + 1)
    ot = acc_ref[0, v_rows] / acc_ref[0, l_row] - lam * (acc_ref[1, v_rows] / acc_ref[1, l_row])
    ot = ot * lax.rsqrt(jnp.mean(ot * ot, axis=0, keepdims=True) + RMS_EPS) * sg_ref[...] * (1.0 - LAM_INIT)
    o_ref[...] = ot.T.astype(BF16)


def _t5_bucket(rel):
    nb = N_BUCKETS // 2
    max_exact = nb // 2
    ret = jnp.where(rel > 0, nb, 0)
    n = jnp.abs(rel)
    nf = jnp.maximum(n, 1).astype(F32)
    large = max_exact + (jnp.log(nf / max_exact) / math.log(MAX_DISTANCE / max_exact)
                         * (nb - max_exact)).astype(jnp.int32)
    large = jnp.minimum(large, nb - 1)
    return ret + jnp.where(n < max_exact, n, large)


def _attn_bias_tables(rel_bias, blk):
    assert blk >= MAX_DISTANCE
    table = rel_bias.astype(F32) * LOG2E

    def bias_of(rel):
        bucket = _t5_bucket(rel)
        sel = bucket[None] == jnp.arange(N_BUCKETS, dtype=jnp.int32).reshape((N_BUCKETS,) + (1,) * rel.ndim)
        return jnp.stack([jnp.sum(jnp.where(sel, table[:, h].reshape((N_BUCKETS,) + (1,) * rel.ndim), 0.0), axis=0)
                          for h in range(ATT_HEADS)])

    r = jnp.arange(blk, dtype=jnp.int32)
    rel = jnp.stack([dj * blk + r[:, None] - r[None, :] for dj in (-2, -1, 0, 1, 2)])
    tiles = bias_of(rel)
    m = jnp.arange(N_META, dtype=jnp.int32)
    rel_m0 = m[:, None] - (N_META + r[None, :])
    rel_m1 = rel_m0 - blk
    bmeta = bias_of(jnp.stack([rel_m0, rel_m1]))
    return tiles, bmeta


def _attention(qt, k, vt, kmeta, vtmeta, tiles, bmeta, lam, subln_col, *, blk):
    B, S, _ = k.shape
    return pl.pallas_call(
        functools.partial(_attn_kernel, blk=blk),
        grid=(B, ATT_HEADS, S // blk),
        in_specs=[pl.BlockSpec((None, ATT_V_DIM, blk), lambda b, h, i: (b, h, i)),
                  pl.BlockSpec((None, S, ATT_V_DIM), lambda b, h, i: (b, 0, h)),
                  pl.BlockSpec((None, ATT_V_DIM, S), lambda b, h, i: (b, h, 0)),
                  pl.BlockSpec((N_META, ATT_V_DIM), lambda b, h, i: (0, h)),
                  pl.BlockSpec((ATT_V_DIM, N_META), lambda b, h, i: (h, 0)),
                  pl.BlockSpec((None, 5, blk, blk), lambda b, h, i: (h, 0, 0, 0)),
                  pl.BlockSpec((None, 2, N_META, blk), lambda b, h, i: (h, 0, 0, 0)),
                  pl.BlockSpec((4, ATT_HEAD_DIM), lambda b, h, i: (0, 0)),
                  pl.BlockSpec((ATT_V_DIM, 1), lambda b, h, i: (0, 0))],
        out_specs=pl.BlockSpec((None, blk, ATT_V_DIM), lambda b, h, i: (b, i, h)),
        scratch_shapes=[pltpu.VMEM((2, 1, blk), F32), pltpu.VMEM((2, ATT_V_DIM + ONES_ROWS, blk), F32),
                        pltpu.VMEM((2, 2, blk, blk), F32), pltpu.VMEM((2, 2, 1, blk), F32)],
        out_shape=jax.ShapeDtypeStruct((B, S, ATT_WIDTH), BF16),
        compiler_params=_cparams(("parallel", "parallel", "arbitrary"), 56),
        name="diff_attention",
    )(qt, k, vt, kmeta, vtmeta, tiles, bmeta, lam, subln_col)


def _merge_kernel(x_ref, ya_ref, yb_ref, sg_ref, pa_ref, pb_ref, wo_ref, g2_ref, wr2_ref, br_ref,
                  tri_ref, x1_ref, h2_ref, ti_ref, tg_ref, rk_ref, cnt_ref, lg_ref):
    ma = jnp.dot(ya_ref[...], pa_ref[...], preferred_element_type=F32)
    mb = jnp.dot(yb_ref[...], pb_ref[...], preferred_element_type=F32)
    m = sg_ref[:, :D_MODEL].astype(F32) * ma + sg_ref[:, D_MODEL:].astype(F32) * mb
    x1 = x_ref[...] + jnp.dot(m.astype(BF16), wo_ref[...], preferred_element_type=F32)
    x1_ref[...] = x1
    h2 = x1 * lax.rsqrt(jnp.mean(x1 * x1, axis=-1, keepdims=True) + RMS_EPS) * g2_ref[...]
    h2_hi = h2.astype(BF16)
    h2_ref[...] = _pack_rows(h2)
    h2_lo = (h2 - h2_hi.astype(F32)).astype(BF16)
    wr2 = wr2_ref[...]
    lg_ref[0] = jnp.dot(h2_hi, wr2, preferred_element_type=F32)
    lg_ref[1] = jnp.dot(h2_lo, wr2, preferred_element_type=F32)
    hi_t = lg_ref[0].T
    lo_t = lg_ref[1].T
    logits = hi_t[:N_EXPERTS] + hi_t[N_EXPERTS:] + lo_t[:N_EXPERTS] + br_ref[...]
    row = lax.broadcasted_iota(jnp.int32, logits.shape, 0)
    slot = lax.broadcasted_iota(jnp.int32, ti_ref.shape, 0)
    top_i = jnp.zeros(ti_ref.shape, jnp.int32)
    top_v = jnp.zeros(tg_ref.shape, F32)
    work = logits
    picks = []
    for kk in range(TOP_K):
        mx = jnp.max(work, axis=0, keepdims=True)
        idx = jnp.min(jnp.where(work == mx, row, N_EXPERTS), axis=0, keepdims=True)
        pick = row == idx
        picks.append(pick)
        top_i = jnp.where(slot == kk, idx, top_i)
        top_v = jnp.where(slot == kk, mx, top_v)
        work = jnp.where(pick, -jnp.inf, work)
    ex = jnp.exp(top_v - jnp.max(top_v, axis=0, keepdims=True))
    ti_ref[...] = top_i
    tg_ref[...] = ex / jnp.sum(ex, axis=0, keepdims=True)

    @pl.when(pl.program_id(0) == 0)
    def _():
        cnt_ref[...] = jnp.zeros_like(cnt_ref)

    chosen = jnp.where(work == -jnp.inf, 1.0, 0.0)
    before = cnt_ref[...] + jnp.dot(chosen.astype(BF16), tri_ref[...], preferred_element_type=F32)
    rank = jnp.zeros(rk_ref.shape, F32)
    for kk in range(TOP_K):
        rk = jnp.sum(jnp.where(picks[kk], before, 0.0), axis=0, keepdims=True)
        rank = jnp.where(slot == kk, rk, rank)
    rk_ref[...] = rank.astype(jnp.int32)
    cnt_ref[...] += jnp.sum(chosen, axis=1, keepdims=True)


def _merge(x, ya, yb, sg, pa, pb, wo, g2, wr_hi, wr_lo, br, *, tm):
    T, D = x.shape
    tok = lambda w: pl.BlockSpec((tm, w), lambda i: (i, 0))
    tok_t = pl.BlockSpec((TOP_K, tm), lambda i: (0, i))
    tri = jnp.triu(jnp.ones((tm, tm), F32), 1).astype(BF16)
    wr2 = jnp.concatenate([wr_hi, wr_lo], axis=1)
    return pl.pallas_call(
        _merge_kernel,
        grid=(T // tm,),
        in_specs=[tok(D), tok(HY_WIDTH), tok(ATT_WIDTH), tok(COL_G), _const_spec((HY_WIDTH, D)),
                  _const_spec((ATT_WIDTH, D)), _const_spec((D, D)), _const_spec((1, D)),
                  _const_spec((D, 2 * N_EXPERTS)), _const_spec((N_EXPERTS, 1)), _const_spec((tm, tm))],
        out_specs=[tok(D), tok(D // 2), tok_t, tok_t, tok_t, pl.BlockSpec((N_EXPERTS, 1), lambda i: (0, 0))],
        out_shape=[jax.ShapeDtypeStruct((T, D), F32), jax.ShapeDtypeStruct((T, D // 2), jnp.uint32),
                   jax.ShapeDtypeStruct((TOP_K, T), jnp.int32), jax.ShapeDtypeStruct((TOP_K, T), F32),
                   jax.ShapeDtypeStruct((TOP_K, T), jnp.int32), jax.ShapeDtypeStruct((N_EXPERTS, 1), F32)],
        scratch_shapes=[pltpu.VMEM((2, tm, 2 * N_EXPERTS), F32)],
        compiler_params=_cparams(("arbitrary",), 48),
        name="merge_router",
    )(x, ya, yb, sg, pa, pb, wo, g2, wr2, br.reshape(N_EXPERTS, 1), tri)


def _moe_kernel(iblk_ref, iexp_ref, start_ref, count_ref, nitem_ref, xs_ref, w1_ref, b1_ref, w2_ref, b2_ref, ys_ref,
                w1b_ref, w2b_ref):
    i = pl.program_id(0)
    tm = xs_ref.shape[0]

    @pl.when(jnp.logical_and(i < nitem_ref[0],
                             jnp.logical_or(i == 0, iexp_ref[jnp.maximum(i - 1, 0)] != iexp_ref[i])))
    def _():
        rows = 256
        for r in range(0, D_MODEL, rows):
            w1b_ref[r:r + rows, :] = w1_ref[r:r + rows, :].astype(BF16)
        for r in range(0, D_FF, rows):
            w2b_ref[r:r + rows, :] = w2_ref[r:r + rows, :].astype(BF16)

    @pl.when(i < nitem_ref[0])
    def _():
        blk = iblk_ref[i]
        e = iexp_ref[i]
        x = jnp.concatenate(_unpack_rows(xs_ref[...]), axis=1).astype(BF16)
        u = jnp.dot(x, w1b_ref[...], preferred_element_type=F32) + b1_ref[...]
        glu = jnp.minimum(u[:, :D_FF], SWIGLU_LIMIT)
        lin = jnp.clip(u[:, D_FF:], -SWIGLU_LIMIT, SWIGLU_LIMIT)
        a = glu * jax.nn.sigmoid(SWIGLU_ALPHA * glu) * (lin + 1.0)
        y = _pack_rows(jnp.dot(a.astype(BF16), w2b_ref[...], preferred_element_type=F32) + b2_ref[...])
        row = lax.broadcasted_iota(jnp.int32, (tm, 1), 0) + blk * tm
        mine = jnp.logical_and(row >= start_ref[e], row < start_ref[e] + count_ref[e])
        first_visit = jnp.logical_or(i == 0, iblk_ref[jnp.maximum(i - 1, 0)] != blk)

        @pl.when(first_visit)
        def _():
            ys_ref[...] = jnp.where(mine, y, jnp.zeros_like(y))

        @pl.when(jnp.logical_not(first_visit))
        def _():
            ys_ref[...] = jnp.where(mine, y, ys_ref[...])


def _moe_experts(item_blk, item_exp, start, count, n_items, xs, w1, b1, w2, b2, *, tm):
    A, D = xs.shape[0], xs.shape[1] * 2
    grid_spec = pltpu.PrefetchScalarGridSpec(
        num_scalar_prefetch=5,
        grid=(item_blk.shape[0],),
        in_specs=[pl.BlockSpec((tm, D // 2), lambda i, ib, ie, *_: (ib[i], 0)),
                  pl.BlockSpec((None, D, 2 * D_FF), lambda i, ib, ie, *_: (ie[i], 0, 0)),
                  pl.BlockSpec((None, 1, 2 * D_FF), lambda i, ib, ie, *_: (ie[i], 0, 0)),
                  pl.BlockSpec((None, D_FF, D), lambda i, ib, ie, *_: (ie[i], 0, 0)),
                  pl.BlockSpec((None, 1, D), lambda i, ib, ie, *_: (ie[i], 0, 0))],
        out_specs=pl.BlockSpec((tm, D // 2), lambda i, ib, ie, *_: (ib[i], 0)),
        scratch_shapes=[pltpu.VMEM((D, 2 * D_FF), BF16), pltpu.VMEM((D_FF, D), BF16)],
    )
    return pl.pallas_call(
        _moe_kernel,
        grid_spec=grid_spec,
        out_shape=jax.ShapeDtypeStruct((A, D // 2), jnp.uint32),
        compiler_params=_cparams(("arbitrary",), 56),
        name="moe_experts",
    )(item_blk, item_exp, start, count, n_items, xs, w1, b1, w2, b2)


def _combine_kernel(x1_ref, yg_ref, tg_ref, o_ref):
    half = D_MODEL // 2
    acc_lo = x1_ref[:, :half]
    acc_hi = x1_ref[:, half:]
    g = tg_ref[...]
    for kk in range(TOP_K):
        lo, hi = _unpack_rows(yg_ref[kk])
        acc_lo = acc_lo + g[:, kk:kk + 1] * lo
        acc_hi = acc_hi + g[:, kk:kk + 1] * hi
    o_ref[:, :half] = acc_lo
    o_ref[:, half:] = acc_hi


def _combine(x1, yg, tg, *, tm):
    T, D = x1.shape
    return pl.pallas_call(
        _combine_kernel,
        grid=(T // tm,),
        in_specs=[pl.BlockSpec((tm, D), lambda i: (i, 0)), pl.BlockSpec((TOP_K, tm, D // 2), lambda i: (0, i, 0)),
                  pl.BlockSpec((tm, TOP_K), lambda i: (i, 0))],
        out_specs=pl.BlockSpec((tm, D), lambda i: (i, 0)),
        out_shape=jax.ShapeDtypeStruct((T, D), F32),
        compiler_params=_cparams(("parallel",), 48),
        name="moe_combine",
    )(x1, yg, tg)


def _lookup(table, idx):
    n = table.shape[0]
    sel = idx[None] == jnp.arange(n, dtype=jnp.int32).reshape((n,) + (1,) * idx.ndim)
    return jnp.sum(jnp.where(sel, table.reshape((n,) + (1,) * idx.ndim), 0), axis=0)


def _route(top_i, rank, counts, *, tm):
    T = top_i.shape[1]
    A = T * TOP_K
    counts = counts.reshape(N_EXPERTS).astype(jnp.int32)
    start = jnp.cumsum(counts) - counts
    flat_e = top_i.reshape(A // LANES, LANES)
    slot_of = (_lookup(start, flat_e) + rank.reshape(A // LANES, LANES)).reshape(A)
    first_blk = start // tm
    n_it = jnp.where(counts > 0, (start + counts - 1) // tm - first_blk + 1, 0)
    it_end = jnp.cumsum(n_it)
    n_items = it_end[-1]
    n_max = A // tm + N_EXPERTS
    i = jnp.minimum(jnp.arange(n_max, dtype=jnp.int32), n_items - 1)
    item_exp = jnp.sum(it_end[None, :] <= i[:, None], axis=1, dtype=jnp.int32)
    item_blk = _lookup(first_blk, item_exp) + i - _lookup(it_end - n_it, item_exp)
    return slot_of, item_blk.astype(jnp.int32), item_exp, start, counts, n_items.reshape(1)


def _trunk(x, meta_proj, w, tabs, *, tm_prep, tm_tok, tm_moe, attn_blk):
    B, S, D = x.shape
    T = B * S
    hy_m, _, k_m, vt_m, _ = meta_proj
    assert tm_prep == HY_PAD
    front = jnp.concatenate([jnp.zeros((HY_PAD - N_META, COL_HY), BF16), hy_m[0]], axis=0)
    hy_ext, qt, k, vt, sg = _prep(x, front, w['g1'], w['w_in'], w['wqt'], w['wvt'], w['bd'], w['qg'], w['kg'],
                                  tm=tm_prep)
    kt = _hyena_filter(w['fw1'], w['fb1'], w['ffr1'], w['fw2'], w['fb2'], w['ffr2'], w['fw3'], s_real=S)
    zt, a, bt = _hy_pre(hy_ext, w['scw'], w['scb'], w['skip'], s_real=S)
    ya = _hy_post(_hy_conv(kt, zt), a, bt)
    tiles, bmeta = tabs
    yb = _attention(qt, k, vt, k_m[0], vt_m[0], tiles, bmeta, w['lam'], w['subln'], blk=attn_blk)
    x1, h2, top_i, top_g, rank, counts = _merge(
        x.reshape(T, D), ya.reshape(T, HY_WIDTH), yb.reshape(T, ATT_WIDTH), sg.reshape(T, COL_G),
        w['pa'], w['pb'], w['wo'], w['g2'], w['wr_hi'], w['wr_lo'], w['br'], tm=tm_tok)
    slot_of, item_blk, item_exp, start, cnt, n_items = _route(top_i, rank, counts, tm=tm_moe)
    xs = _sc_sorted_rows(h2, slot_of)
    ys = _moe_experts(item_blk, item_exp, start, cnt, n_items, xs, w['w1'], w['b1'], w['w2'], w['b2'], tm=tm_moe)
    yg = _sc_gather_rows(ys, slot_of).reshape(TOP_K, T, D // 2)
    return _combine(x1, yg, top_g.T, tm=tm_tok).reshape(B, S, D)


def kernel(x_prompt, x_sample, meta_tokens, rel_bias, norm1_g, w_in, short_conv_w, short_conv_b, filt_w1, filt_b1,
           filt_freq1, filt_w2, filt_b2, filt_freq2, filt_w3, hy_skip, proj_a, q_norm_g, k_norm_g, lambda_q1,
           lambda_k1, lambda_q2, lambda_k2, subln_g, proj_b, w_out, norm2_g, w_router, b_router, w_mlp1, b_mlp1,
           w_mlp2, b_mlp2):
    l = 0
    n_hm = COL_Q // ATT_HEAD_DIM
    w_in_bf = w_in[l].astype(BF16)
    c_q, c_v = COL_HY, COL_HY + COL_Q + COL_K
    wr_hi = w_router[l].astype(BF16)
    w = {
        'g1': norm1_g[l][None], 'w_in': w_in_bf,
        'wqt': w_in_bf[:, c_q:c_q + COL_Q].T, 'wvt': w_in_bf[:, c_v:c_v + COL_V].T,
        'bd': jnp.kron(jnp.eye(n_hm, dtype=F32), jnp.full((ATT_HEAD_DIM, ATT_HEAD_DIM), 1.0 / ATT_HEAD_DIM, F32)
                       ).astype(BF16),
        'qg': (jnp.tile(q_norm_g[l], n_hm) * (ATT_HEAD_DIM ** -0.5 * LOG2E))[:, None],
        'kg': jnp.tile(k_norm_g[l], n_hm)[None],
        'wr_hi': wr_hi, 'wr_lo': (w_router[l] - wr_hi.astype(F32)).astype(BF16),
        'fw1': filt_w1[l], 'fb1': filt_b1[l], 'ffr1': filt_freq1[l], 'fw2': filt_w2[l], 'fb2': filt_b2[l],
        'ffr2': filt_freq2[l], 'fw3': filt_w3[l],
        'scw': short_conv_w[l], 'scb': short_conv_b[l], 'skip': hy_skip[l],
        'lam': jnp.stack([lambda_q1[l], lambda_k1[l], lambda_q2[l], lambda_k2[l]]), 'subln': subln_g[l][:, None],
        'pa': proj_a[l].astype(BF16), 'pb': proj_b[l].astype(BF16), 'wo': w_out[l].astype(BF16),
        'g2': norm2_g[l][None], 'br': b_router[l][None],
        'w1': w_mlp1[l], 'b1': b_mlp1[l][:, None, :], 'w2': w_mlp2[l], 'b2': b_mlp2[l][:, None, :],
    }
    meta_proj = _prep(meta_tokens[None], None, w['g1'], w['w_in'], w['wqt'], w['wvt'], w['bd'], w['qg'], w['kg'],
                      tm=N_META)
    tabs = _attn_bias_tables(rel_bias, TILES['attn_blk'])
    y_prompt = _trunk(x_prompt, meta_proj, w, tabs, **TILES)
    y_sample = _trunk(x_sample, meta_proj, w, tabs, **TILES)
    return (y_prompt, y_sample)
```

```python
import functools
import math

import jax
import jax.numpy as jnp
import numpy as np
from jax import lax
from jax.experimental import pallas as pl
from jax.experimental.pallas import tpu as pltpu
from jax.experimental.pallas import tpu_sc as plsc

F32 = jnp.float32
BF16 = jnp.bfloat16

D_MODEL = 1024
N_META = 16
RMS_EPS = 1e-6
HY_WIDTH = 512
HY_SHORT = 3
FILT_BANDS = 16
FILT_EMB = 1 + 2 * FILT_BANDS
FILT_HIDDEN = 64
DECAY_TARGET = 1e-2
FAST_DECAY_PCT = 0.3
SLOW_DECAY_PCT = 1.5
ATT_HEADS = 4
ATT_HEAD_DIM = 64
ATT_V_DIM = 128
ATT_WIDTH = 512
N_BUCKETS = 32
MAX_DISTANCE = 128
N_EXPERTS = 32
TOP_K = 4
D_FF = 1024
SWIGLU_LIMIT = 7.0
SWIGLU_ALPHA = 1.702
COL_HY = 3 * HY_WIDTH
COL_Q = 512
COL_K = 512
COL_V = 512
COL_G = 2 * D_MODEL
IN_COLS = COL_HY + COL_Q + COL_K + COL_V + COL_G
LAM_INIT = 0.8 - 0.6 * math.exp(-0.3 * 0)
LOG2E = 1.4426950408889634
ONES_ROWS = 16

V7X_VMEM_BYTES = 64 * 1024 * 1024
LANES = 128
CONV_BLK = 256
HY_PAD = 512
TILES = dict(tm_prep=512, tm_tok=512, tm_moe=512, attn_blk=512)


def _cparams(sem, vmem_mb):
    return pltpu.CompilerParams(dimension_semantics=sem, vmem_limit_bytes=vmem_mb * 1024 * 1024)


def _const_spec(shape):
    nd = len(shape)
    return pl.BlockSpec(shape, lambda *_: (0,) * nd, pipeline_mode=pl.Buffered(1))


def _pack_rows(x):
    n = x.shape[1] // 2
    lo = pltpu.bitcast(x[:, :n].astype(BF16).astype(F32), jnp.uint32)
    hi = pltpu.bitcast(x[:, n:].astype(BF16).astype(F32), jnp.uint32)
    return (hi & jnp.uint32(0xFFFF0000)) | (lo >> 16)


def _unpack_rows(w):
    lo = pltpu.bitcast(w << 16, F32)
    hi = pltpu.bitcast(w & jnp.uint32(0xFFFF0000), F32)
    return lo, hi


SC_ROWS = 64


def _sc_workers():
    info = plsc.get_sparse_core_info()
    return info, info.num_cores * info.num_subcores


def _sc_stream_rows(table_hbm, out_hbm, rows_v, sems, base, n_chunks, load_index, index_ref):
    def rows_copy(j, buf):
        return pltpu.make_async_copy(table_hbm.at[index_ref(j, buf)], rows_v.at[buf], sems.at[buf])

    def fetch(j, buf):
        load_index(j, buf)
        rows_copy(j, buf).start()

    def drain(j, buf):
        rows_copy(j, buf).wait()
        pltpu.sync_copy(rows_v.at[buf], out_hbm.at[pl.ds(pl.multiple_of(base + j * SC_ROWS, SC_ROWS), SC_ROWS)])

    fetch(0, 0)

    @pl.loop(0, n_chunks // 2)
    def _(jj):
        j = 2 * jj
        fetch(j + 1, 1)
        drain(j, 0)

        @pl.when(j + 2 < n_chunks)
        def _():
            fetch(j + 2, 0)

        drain(j + 1, 1)


def _sc_sorted_rows(table, slot_of, *, scan=16384):
    info, n_workers = _sc_workers()
    lanes = info.num_lanes
    n_tok, width = table.shape
    n_picks = slot_of.shape[0]
    per_worker = n_picks // n_workers
    n_chunks = per_worker // SC_ROWS
    assert n_tok % scan == 0 and n_picks % scan == 0 and scan % lanes == 0
    assert n_chunks * SC_ROWS * n_workers == n_picks and n_chunks % 2 == 0
    mesh = plsc.VectorSubcoreMesh(core_axis_name="c", subcore_axis_name="s")

    @functools.partial(
        pl.kernel, mesh=mesh, out_type=jax.ShapeDtypeStruct((n_picks, width), table.dtype),
        scratch_types=[pltpu.VMEM((scan,), jnp.int32), pltpu.VMEM((per_worker,), jnp.int32),
                       pltpu.VMEM((2, SC_ROWS, width), table.dtype), pltpu.SemaphoreType.DMA((2,))],
        compiler_params=pltpu.CompilerParams(needs_layout_passes=False))
    def sorted_rows(table_hbm, slot_hbm, out_hbm, picks_v, tok_v, rows_v, sems):
        base = (lax.axis_index("s") * info.num_cores + lax.axis_index("c")) * per_worker
        lane = lax.iota(jnp.int32, lanes)

        @pl.loop(0, per_worker // lanes)
        def _(i):
            tok_v[pl.ds(i * lanes, lanes)] = jnp.zeros((lanes,), jnp.int32)

        @pl.loop(0, n_picks // scan)
        def _(ci):
            a0 = pl.multiple_of(ci * scan, scan)
            pltpu.sync_copy(slot_hbm.at[pl.ds(a0, scan)], picks_v)
            t0 = lax.rem(a0, n_tok)

            @pl.loop(0, scan // lanes)
            def _(i):
                row = picks_v[pl.ds(i * lanes, lanes)] - base
                mine = jnp.logical_and(row >= 0, row < per_worker)
                plsc.store_scatter(tok_v, [jnp.where(mine, row, 0)], t0 + i * lanes + lane, mask=mine)

        _sc_stream_rows(table_hbm, out_hbm, rows_v, sems, base, n_chunks,
                        load_index=lambda j, buf: None,
                        index_ref=lambda j, buf: tok_v.at[pl.ds(pl.multiple_of(j * SC_ROWS, SC_ROWS), SC_ROWS)])

    return sorted_rows(table, slot_of)


def _sc_gather_rows(table, idx):
    info, n_workers = _sc_workers()
    n_rows, width = idx.shape[0], table.shape[1]
    per_worker = n_rows // n_workers
    n_chunks = per_worker // SC_ROWS
    assert n_chunks * SC_ROWS * n_workers == n_rows and n_chunks % 2 == 0
    mesh = plsc.VectorSubcoreMesh(core_axis_name="c", subcore_axis_name="s")

    @functools.partial(
        pl.kernel, mesh=mesh, out_type=jax.ShapeDtypeStruct((n_rows, width), table.dtype),
        scratch_types=[pltpu.VMEM((2, SC_ROWS), jnp.int32), pltpu.VMEM((2, SC_ROWS, width), table.dtype),
                       pltpu.SemaphoreType.DMA((2,))])
    def gather(table_hbm, idx_hbm, out_hbm, idx_v, rows_v, sems):
        base = (lax.axis_index("s") * info.num_cores + lax.axis_index("c")) * per_worker

        def load_index(j, buf):
            off = pl.multiple_of(base + j * SC_ROWS, SC_ROWS)
            pltpu.sync_copy(idx_hbm.at[pl.ds(off, SC_ROWS)], idx_v.at[buf])

        _sc_stream_rows(table_hbm, out_hbm, rows_v, sems, base, n_chunks,
                        load_index=load_index, index_ref=lambda j, buf: idx_v.at[buf])

    return gather(table, idx)


def _prep_kernel(x_ref, front_ref, g1_ref, w_ref, wqt_ref, wvt_ref, bd_ref, qg_ref, kg_ref,
                 hy_ref, qt_ref, k_ref, vt_ref, sg_ref, *, lead):
    if lead:
        @pl.when(pl.program_id(1) == 0)
        def _():
            hy_ref[...] = front_ref[...]

        @pl.when(pl.program_id(1) > 0)
        def _():
            _prep_tile(x_ref, g1_ref, w_ref, wqt_ref, wvt_ref, bd_ref, qg_ref, kg_ref,
                       hy_ref, qt_ref, k_ref, vt_ref, sg_ref)
    else:
        _prep_tile(x_ref, g1_ref, w_ref, wqt_ref, wvt_ref, bd_ref, qg_ref, kg_ref,
                   hy_ref, qt_ref, k_ref, vt_ref, sg_ref)


def _prep_tile(x_ref, g1_ref, w_ref, wqt_ref, wvt_ref, bd_ref, qg_ref, kg_ref,
               hy_ref, qt_ref, k_ref, vt_ref, sg_ref):
    x = x_ref[...]
    ms = jnp.mean(x * x, axis=-1, keepdims=True)
    h = (x * lax.rsqrt(ms + RMS_EPS) * g1_ref[...]).astype(BF16)
    nt_dims = (((1,), (1,)), ((), ()))

    def proj(lo, hi):
        return jnp.dot(h, w_ref[:, lo:hi], preferred_element_type=F32)

    def proj_t(wt_ref):
        return lax.dot_general(wt_ref[...], h, nt_dims, preferred_element_type=F32)

    c0, c1, c2, c3 = COL_HY, COL_HY + COL_Q, COL_HY + COL_Q + COL_K, COL_HY + COL_Q + COL_K + COL_V
    hy_ref[...] = proj(0, c0).astype(BF16)
    qt = proj_t(wqt_ref)
    q3 = qt.reshape(COL_Q // ATT_HEAD_DIM, ATT_HEAD_DIM, qt.shape[1])
    q3 = q3 * lax.rsqrt(jnp.mean(q3 * q3, axis=1, keepdims=True) + RMS_EPS)
    qt_ref[...] = (q3.reshape(qt.shape) * qg_ref[...]).astype(BF16)
    kk = proj(c1, c2)
    msk = jnp.dot((kk * kk).astype(BF16), bd_ref[...], preferred_element_type=F32)
    k_ref[...] = (kk * lax.rsqrt(msk + RMS_EPS) * kg_ref[...]).astype(BF16)
    vt_ref[...] = proj_t(wvt_ref).astype(BF16)
    sg_ref[...] = jax.nn.sigmoid(proj(c3, IN_COLS)).astype(BF16)


def _prep(x, front, g1, w_in_bf, wqt, wvt, bd, qg_col, kg, *, tm):
    B, S, D = x.shape
    nt = S // tm
    lead = 0 if front is None else 1
    if front is None:
        front = jnp.zeros((tm, COL_HY), BF16)
    src = lambda i: jnp.maximum(i - lead, 0)
    tok = lambda w: pl.BlockSpec((None, tm, w), lambda b, i: (b, src(i), 0))
    tok_t = lambda w: pl.BlockSpec((None, w, tm), lambda b, i: (b, 0, src(i)))
    return pl.pallas_call(
        functools.partial(_prep_kernel, lead=lead),
        grid=(B, nt + lead),
        in_specs=[tok(D), _const_spec((tm, COL_HY)), _const_spec((1, D)), _const_spec((D, IN_COLS)),
                  _const_spec((COL_Q, D)), _const_spec((COL_V, D)), _const_spec((COL_Q, COL_Q)),
                  _const_spec((COL_Q, 1)), _const_spec((1, COL_K))],
        out_specs=[pl.BlockSpec((None, tm, COL_HY), lambda b, i: (b, i, 0)),
                   tok_t(COL_Q), tok(COL_K), tok_t(COL_V), tok(COL_G)],
        out_shape=[jax.ShapeDtypeStruct((B, S + lead * tm, COL_HY), BF16),
                   jax.ShapeDtypeStruct((B, COL_Q, S), BF16), jax.ShapeDtypeStruct((B, S, COL_K), BF16),
                   jax.ShapeDtypeStruct((B, COL_V, S), BF16), jax.ShapeDtypeStruct((B, S, COL_G), BF16)],
        compiler_params=_cparams(("parallel", "arbitrary"), 48),
        name="prep",
    )(x, front, g1, w_in_bf, wqt, wvt, bd, qg_col, kg)


def _filt_kernel(w1t_ref, w1c_ref, w1s_ref, b1_ref, fr1_ref, w2t_ref, b2_ref, fr2_ref, w3f_ref, w3b_ref,
                 bands_ref, deltas_ref, kt_ref, hid_ref, *, n_seq, s_real):
    lk = kt_ref.shape[1]
    hi = lax.Precision.HIGHEST
    u = lax.broadcasted_iota(jnp.int32, (1, lk), 1)
    d = u - s_real
    pos = jnp.abs(d).astype(F32)
    t = pos / float(max(n_seq - 1, 1))

    @pl.when(pl.program_id(0) == 0)
    def _():
        w = 2.0 * math.pi * pos / float(n_seq)
        ang = bands_ref[...] * w
        pre = (w1t_ref[...] * t
               + jnp.dot(w1c_ref[...], jnp.cos(ang), precision=hi, preferred_element_type=F32)
               - jnp.dot(w1s_ref[...], jnp.sin(ang), precision=hi, preferred_element_type=F32)
               + b1_ref[...])
        h1 = jnp.sin(fr1_ref[...] * pre)
        h2 = jnp.sin(fr2_ref[...] * (jnp.dot(w2t_ref[...], h1, precision=hi, preferred_element_type=F32)
                                     + b2_ref[...]))
        hid_ref[...] = h2

    h2 = hid_ref[...]
    hf = jnp.dot(w3f_ref[...], h2, precision=hi, preferred_element_type=F32)
    hb = jnp.dot(w3b_ref[...], h2, precision=hi, preferred_element_type=F32)
    decay = jnp.exp(-t * deltas_ref[...])
    valid = pos <= float(n_seq - 1)
    kun = jnp.where(valid, jnp.where(d >= 0, hf, hb) * decay, 0.0)
    nrm = lax.rsqrt(jnp.sum(kun * kun, axis=1, keepdims=True) + RMS_EPS)
    kt_ref[...] = kun * nrm


def _hyena_filter(fw1, fb1, ffr1, fw2, fb2, ffr2, fw3, *, s_real):
    n_seq = s_real + N_META
    lk = 2 * s_real + CONV_BLK
    cf = 64
    col = lambda a: a.reshape(-1, 1)
    w1t = fw1.T
    bands = jnp.linspace(1e-4, FILT_BANDS - 1, FILT_BANDS, dtype=F32).reshape(-1, 1)
    max_decay = math.log(DECAY_TARGET) / FAST_DECAY_PCT
    min_decay = math.log(DECAY_TARGET) / SLOW_DECAY_PCT
    deltas = jnp.abs(jnp.linspace(min_decay, max_decay, HY_WIDTH, dtype=F32)).reshape(-1, 1)
    w3t = fw3.T
    h = FILT_HIDDEN
    return pl.pallas_call(
        functools.partial(_filt_kernel, n_seq=n_seq, s_real=s_real),
        grid=(HY_WIDTH // cf,),
        in_specs=[_const_spec((h, 1)), _const_spec((h, FILT_BANDS)), _const_spec((h, FILT_BANDS)),
                  _const_spec((h, 1)), _const_spec((h, 1)), _const_spec((h, h)), _const_spec((h, 1)),
                  _const_spec((h, 1)),
                  pl.BlockSpec((cf, h), lambda i: (i, 0)), pl.BlockSpec((cf, h), lambda i: (i, 0)),
                  _const_spec((FILT_BANDS, 1)), pl.BlockSpec((cf, 1), lambda i: (i, 0))],
        out_specs=pl.BlockSpec((cf, lk), lambda i: (i, 0)),
        out_shape=jax.ShapeDtypeStruct((HY_WIDTH, lk), F32),
        scratch_shapes=[pltpu.VMEM((h, lk), F32)],
        compiler_params=_cparams(("arbitrary",), 48),
        name="hyena_filter",
    )(w1t[:, 0:1], w1t[:, 1:1 + FILT_BANDS], w1t[:, 1 + FILT_BANDS:], col(fb1), col(ffr1), fw2.T, col(fb2),
      col(ffr2), w3t[:HY_WIDTH], w3t[HY_WIDTH:], bands, deltas)


def _hy_pre_kernel(x0_ref, x1_ref, v_ref, p0_ref, p1_ref, pv_ref, n0_ref, n1_ref, nv_ref,
                   scw_ref, scb_ref, skip_ref, zt_ref, a_ref, bt_ref, zs_ref):
    jt = pl.program_id(1)
    last = pl.num_programs(1) - 1
    nb_batch = x0_ref.shape[0]
    row = lax.broadcasted_iota(jnp.int32, (CONV_BLK, LANES), 0)
    keep_next = (jt < last).astype(F32)
    z_keep = jnp.logical_or(jt > 0, row >= CONV_BLK - N_META)

    def short_conv(cur_ref, prev_ref, next_ref, comp, b):
        cur = cur_ref[b].astype(F32)
        prev = prev_ref[b, 15:16, :].astype(F32)
        nxt = next_ref[b, 0:1, :].astype(F32) * keep_next
        up = jnp.where(row == 0, prev, pltpu.roll(cur, 1, 0))
        dn = jnp.where(row == CONV_BLK - 1, nxt, pltpu.roll(cur, CONV_BLK - 1, 0))
        w = scw_ref[comp]
        return up * w[0:1] + cur * w[1:2] + dn * w[2:3] + scb_ref[comp]

    def body(b, carry):
        x0c = short_conv(x0_ref, p0_ref, n0_ref, 0, b)
        x1c = short_conv(x1_ref, p1_ref, n1_ref, 1, b)
        vc = short_conv(v_ref, pv_ref, nv_ref, 2, b)
        z = jnp.where(z_keep, x1c * vc, 0.0)
        zs_ref[b] = z.T
        a_ref[b] = x0c.astype(BF16)
        bt_ref[b] = (x0c * z * skip_ref[...]).astype(BF16)
        return carry

    lax.fori_loop(0, nb_batch, body, 0)
    zt_ref[...] = pltpu.einshape("bcl->cbl", zs_ref[...]).astype(BF16)


def _hy_pre(hy_ext, scw, scb, skip, *, s_real):
    B = hy_ext.shape[0]
    nb = s_real // CONV_BLK
    ncb = HY_WIDTH // LANES
    sub = 16
    r16 = CONV_BLK // sub
    last16 = (s_real + HY_PAD) // sub - 1
    cur = lambda comp: pl.BlockSpec((B, CONV_BLK, LANES), lambda cb, jt: (0, jt + 1, comp * ncb + cb))
    prv = lambda comp: pl.BlockSpec((B, sub, LANES), lambda cb, jt: (0, r16 * (jt + 1) - 1, comp * ncb + cb))
    nxt = lambda comp: pl.BlockSpec(
        (B, sub, LANES), lambda cb, jt: (0, jnp.minimum(r16 * (jt + 2), last16), comp * ncb + cb))
    nat = pl.BlockSpec((B, CONV_BLK, LANES), lambda cb, jt: (0, jnp.maximum(jt - 1, 0), cb))
    scw_r = scw.reshape(HY_SHORT, 3, ncb, LANES).transpose(2, 1, 0, 3)
    scb_r = scb.reshape(3, ncb, 1, LANES).transpose(1, 0, 2, 3)
    skip_r = skip.reshape(ncb, 1, LANES)
    return pl.pallas_call(
        _hy_pre_kernel,
        grid=(ncb, nb + 1),
        in_specs=[cur(0), cur(1), cur(2), prv(0), prv(1), prv(2), nxt(0), nxt(1), nxt(2),
                  pl.BlockSpec((None, 3, HY_SHORT, LANES), lambda cb, jt: (cb, 0, 0, 0)),
                  pl.BlockSpec((None, 3, 1, LANES), lambda cb, jt: (cb, 0, 0, 0)),
                  pl.BlockSpec((None, 1, LANES), lambda cb, jt: (cb, 0, 0))],
        out_specs=[pl.BlockSpec((LANES, None, B, CONV_BLK), lambda cb, jt: (cb, jt, 0, 0)), nat, nat],
        out_shape=[jax.ShapeDtypeStruct((HY_WIDTH, nb + 1, B, CONV_BLK), BF16),
                   jax.ShapeDtypeStruct((B, s_real, HY_WIDTH), BF16),
                   jax.ShapeDtypeStruct((B, s_real, HY_WIDTH), BF16)],
        scratch_shapes=[pltpu.VMEM((B, LANES, CONV_BLK), F32)],
        compiler_params=_cparams(("parallel", "arbitrary"), 48),
        name="hyena_pre",
    )(hy_ext, hy_ext, hy_ext, hy_ext, hy_ext, hy_ext, hy_ext, hy_ext, hy_ext, scw_r, scb_r, skip_r)


def _hy_conv_kernel(kt_ref, z_ref, y_ref, g_ref, *, nb):
    cbk = z_ref.shape[0]
    B = z_ref.shape[2]
    s_real = nb * CONV_BLK
    nq = 2 * s_real // LANES + 1
    upper = (lax.broadcasted_iota(jnp.int32, (LANES, LANES), 1)
             >= lax.broadcasted_iota(jnp.int32, (LANES, LANES), 0))

    def channel(c, carry):
        kt_row = kt_ref[pl.ds(c, 1), :]
        for q in range(nq):
            off = 2 * s_real - LANES * q
            skew = lambda lo: pltpu.roll(jnp.broadcast_to(kt_row[:, lo:lo + LANES], (LANES, LANES)),
                                         0, 1, stride=1, stride_axis=0)
            blk_g = jnp.where(upper, skew(off + LANES), skew(off))
            g_ref[LANES * q:LANES * (q + 1), :] = blk_g.astype(BF16)
        y_ref[c] = jnp.zeros(y_ref.shape[1:], F32)
        for delta in range(-(nb - 1), nb + 1):
            base = CONV_BLK * (nb - delta)
            tile = jnp.concatenate([g_ref[base + LANES:base + LANES + CONV_BLK, :],
                                    g_ref[base:base + CONV_BLK, :]], axis=1)
            jb_lo = max(0, 1 - delta)
            n = min(nb, nb - delta) - jb_lo + 1
            o_lo = jb_lo + delta - 1
            lhs = z_ref[c, jb_lo:jb_lo + n].reshape(n * B, CONV_BLK)
            y_ref[c, o_lo:o_lo + n] += jnp.dot(lhs, tile, preferred_element_type=F32).reshape(n, B, CONV_BLK)
        return carry

    lax.fori_loop(0, cbk, channel, 0)


def _hy_conv(kt, zt, *, cbk=8):
    C, nb1, B, _ = zt.shape
    nb = nb1 - 1
    lk = kt.shape[1]
    return pl.pallas_call(
        functools.partial(_hy_conv_kernel, nb=nb),
        grid=(C // cbk,),
        in_specs=[pl.BlockSpec((cbk, lk), lambda i: (i, 0)),
                  pl.BlockSpec((cbk, nb1, B, CONV_BLK), lambda i: (i, 0, 0, 0))],
        out_specs=pl.BlockSpec((cbk, nb, B, CONV_BLK), lambda i: (i, 0, 0, 0)),
        out_shape=jax.ShapeDtypeStruct((C, nb, B, CONV_BLK), F32),
        scratch_shapes=[pltpu.VMEM((2 * nb * CONV_BLK + LANES, LANES), BF16)],
        compiler_params=_cparams(("parallel",), 48),
        name="hyena_conv",
    )(kt, zt)


def _hy_post_kernel(yt_ref, a_ref, bt_ref, o_ref, ys_ref):
    ys_ref[...] = pltpu.einshape("cbl->bcl", yt_ref[...])

    def body(b, carry):
        y = ys_ref[b].T
        o_ref[b] = (a_ref[b].astype(F32) * y + bt_ref[b].astype(F32)).astype(BF16)
        return carry

    lax.fori_loop(0, a_ref.shape[0], body, 0)


def _hy_post(yt, a, bt):
    C, nb, B, _ = yt.shape
    nat = pl.BlockSpec((B, CONV_BLK, LANES), lambda cb, ib: (0, ib, cb))
    return pl.pallas_call(
        _hy_post_kernel,
        grid=(C // LANES, nb),
        in_specs=[pl.BlockSpec((LANES, None, B, CONV_BLK), lambda cb, ib: (cb, ib, 0, 0)), nat, nat],
        out_specs=nat,
        out_shape=jax.ShapeDtypeStruct(a.shape, BF16),
        scratch_shapes=[pltpu.VMEM((B, LANES, CONV_BLK), F32)],
        compiler_params=_cparams(("parallel", "parallel"), 48),
        name="hyena_post",
    )(yt, a, bt)


def _attn_kernel(qt_ref, k_ref, vt_ref, km_ref, vtm_ref, bias_ref, bmeta_ref, lam_ref, sg_ref,
                 o_ref, m_ref, acc_ref, s_ref, bm_ref, *, blk):
    i = pl.program_id(2)
    nk = k_ref.shape[0] // blk
    assert nk % 2 == 0
    qt = qt_ref[...]
    rowi = lax.broadcasted_iota(jnp.int32, qt.shape, 0)
    zero = jnp.zeros_like(qt)
    qmaps = (jnp.where(rowi < ATT_HEAD_DIM, qt, zero), jnp.where(rowi >= ATT_HEAD_DIM, qt, zero))

    def with_ones(vt):
        return jnp.concatenate([vt, jnp.ones((ONES_ROWS, vt.shape[1]), BF16)], axis=0)

    kmeta, vtmeta, bias_m = km_ref[...], with_ones(vtm_ref[...]), bmeta_ref[jnp.minimum(i, 1)]
    for c in range(2):
        s = jnp.dot(kmeta, qmaps[c], preferred_element_type=F32) + bias_m
        m0 = jnp.max(s, axis=0, keepdims=True)
        m_ref[c] = m0
        acc_ref[c] = jnp.dot(vtmeta, jnp.exp2(s - m0).astype(BF16), preferred_element_type=F32)

    def scores(j, slot):
        kblk = k_ref[pl.ds(pl.multiple_of(j * blk, blk), blk), :]
        bias = bias_ref[jnp.clip(j - i, -2, 2) + 2]
        for c in range(2):
            s = jnp.dot(kblk, qmaps[c], preferred_element_type=F32) + bias
            s_ref[slot, c] = s
            bm_ref[slot, c] = jnp.max(s, axis=0, keepdims=True)

    def consume(j, slot):
        vtblk = with_ones(vt_ref[:, pl.ds(pl.multiple_of(j * blk, blk), blk)])
        for c in range(2):
            m_old = m_ref[c]
            m_new = jnp.maximum(m_old, bm_ref[slot, c])
            p = jnp.exp2(s_ref[slot, c] - m_new)
            acc_ref[c] = (jnp.exp2(m_old - m_new) * acc_ref[c]
                          + jnp.dot(vtblk, p.astype(BF16), preferred_element_type=F32))
            m_ref[c] = m_new

    scores(0, 0)

    def pair(jj, carry):
        j = 2 * jj
        scores(j + 1, 1)
        consume(j, 0)
        scores(j + 2, 0)
        consume(j + 1, 1)
        return carry

    lax.fori_loop(0, nk // 2 - 1, pair, 0)
    scores(nk - 1, 1)
    consume(nk - 2, 0)
    consume(nk - 1, 1)

    lq = lam_ref[...]
    lam = (jnp.exp(jnp.sum(lq[0:1] * lq[1:2], axis=1, keepdims=True))
           - jnp.exp(jnp.sum(lq[2:3] * lq[3:4], axis=1, keepdims=True)) + LAM_INIT)
    v_rows = slice(0, ATT_V_DIM)
    l_row = slice(ATT_V_DIM, ATT_V_DIM + 1)
    ot = acc_ref[0, v_rows] / acc_ref[0, l_row] - lam * (acc_ref[1, v_rows] / acc_ref[1, l_row])
    ot = ot * lax.rsqrt(jnp.mean(ot * ot, axis=0, keepdims=True) + RMS_EPS) * sg_ref[...] * (1.0 - LAM_INIT)
    o_ref[...] = ot.T.astype(BF16)


def _t5_bucket(rel):
    nb = N_BUCKETS // 2
    max_exact = nb // 2
    ret = jnp.where(rel > 0, nb, 0)
    n = jnp.abs(rel)
    nf = jnp.maximum(n, 1).astype(F32)
    large = max_exact + (jnp.log(nf / max_exact) / math.log(MAX_DISTANCE / max_exact)
                         * (nb - max_exact)).astype(jnp.int32)
    large = jnp.minimum(large, nb - 1)
    return ret + jnp.where(n < max_exact, n, large)


def _attn_bias_tables(rel_bias, blk):
    assert blk >= MAX_DISTANCE
    table = rel_bias.astype(F32) * LOG2E

    def bias_of(rel):
        bucket = _t5_bucket(rel)
        sel = bucket[None] == jnp.arange(N_BUCKETS, dtype=jnp.int32).reshape((N_BUCKETS,) + (1,) * rel.ndim)
        return jnp.stack([jnp.sum(jnp.where(sel, table[:, h].reshape((N_BUCKETS,) + (1,) * rel.ndim), 0.0), axis=0)
                          for h in range(ATT_HEADS)])

    r = jnp.arange(blk, dtype=jnp.int32)
    rel = jnp.stack([dj * blk + r[:, None] - r[None, :] for dj in (-2, -1, 0, 1, 2)])
    tiles = bias_of(rel)
    m = jnp.arange(N_META, dtype=jnp.int32)
    rel_m0 = m[:, None] - (N_META + r[None, :])
    rel_m1 = rel_m0 - blk
    bmeta = bias_of(jnp.stack([rel_m0, rel_m1]))
    return tiles, bmeta


def _attention(qt, k, vt, kmeta, vtmeta, tiles, bmeta, lam, subln_col, *, blk):
    B, S, _ = k.shape
    return pl.pallas_call(
        functools.partial(_attn_kernel, blk=blk),
        grid=(ATT_HEADS, B, S // blk),
        in_specs=[pl.BlockSpec((None, ATT_V_DIM, blk), lambda h, b, i: (b, h, i)),
                  pl.BlockSpec((None, S, ATT_V_DIM), lambda h, b, i: (b, 0, h)),
                  pl.BlockSpec((None, ATT_V_DIM, S), lambda h, b, i: (b, h, 0)),
                  pl.BlockSpec((N_META, ATT_V_DIM), lambda h, b, i: (0, h)),
                  pl.BlockSpec((ATT_V_DIM, N_META), lambda h, b, i: (h, 0)),
                  pl.BlockSpec((None, 5, blk, blk), lambda h, b, i: (h, 0, 0, 0)),
                  pl.BlockSpec((None, 2, N_META, blk), lambda h, b, i: (h, 0, 0, 0)),
                  pl.BlockSpec((4, ATT_HEAD_DIM), lambda h, b, i: (0, 0)),
                  pl.BlockSpec((ATT_V_DIM, 1), lambda h, b, i: (0, 0))],
        out_specs=pl.BlockSpec((None, blk, ATT_V_DIM), lambda h, b, i: (b, i, h)),
        scratch_shapes=[pltpu.VMEM((2, 1, blk), F32), pltpu.VMEM((2, ATT_V_DIM + ONES_ROWS, blk), F32),
                        pltpu.VMEM((2, 2, blk, blk), F32), pltpu.VMEM((2, 2, 1, blk), F32)],
        out_shape=jax.ShapeDtypeStruct((B, S, ATT_WIDTH), BF16),
        compiler_params=_cparams(("parallel", "parallel", "arbitrary"), 56),
        name="diff_attention",
    )(qt, k, vt, kmeta, vtmeta, tiles, bmeta, lam, subln_col)


def _merge_kernel(x_ref, ya_ref, yb_ref, sg_ref, pa_ref, pb_ref, wo_ref, g2_ref, wr2_ref, br_ref,
                  tri_ref, x1_ref, h2_ref, ti_ref, tg_ref, rk_ref, cnt_ref, lg_ref):
    ma = jnp.dot(ya_ref[...], pa_ref[...], preferred_element_type=F32)
    mb = jnp.dot(yb_ref[...], pb_ref[...], preferred_element_type=F32)
    m = sg_ref[:, :D_MODEL].astype(F32) * ma + sg_ref[:, D_MODEL:].astype(F32) * mb
    x1 = x_ref[...] + jnp.dot(m.astype(BF16), wo_ref[...], preferred_element_type=F32)
    x1_ref[...] = x1
    h2 = x1 * lax.rsqrt(jnp.mean(x1 * x1, axis=-1, keepdims=True) + RMS_EPS) * g2_ref[...]
    h2_hi = h2.astype(BF16)
    h2_ref[...] = _pack_rows(h2)
    h2_lo = (h2 - h2_hi.astype(F32)).astype(BF16)
    wr2 = wr2_ref[...]
    lg_ref[0] = jnp.dot(h2_hi, wr2, preferred_element_type=F32)
    lg_ref[1] = jnp.dot(h2_lo, wr2, preferred_element_type=F32)
    hi_t = lg_ref[0].T
    lo_t = lg_ref[1].T
    logits = hi_t[:N_EXPERTS] + hi_t[N_EXPERTS:] + lo_t[:N_EXPERTS] + br_ref[...]
    row = lax.broadcasted_iota(jnp.int32, logits.shape, 0)
    slot = lax.broadcasted_iota(jnp.int32, ti_ref.shape, 0)
    top_i = jnp.zeros(ti_ref.shape, jnp.int32)
    top_v = jnp.zeros(tg_ref.shape, F32)
    work = logits
    picks = []
    for kk in range(TOP_K):
        mx = jnp.max(work, axis=0, keepdims=True)
        idx = jnp.min(jnp.where(work == mx, row, N_EXPERTS), axis=0, keepdims=True)
        pick = row == idx
        picks.append(pick)
        top_i = jnp.where(slot == kk, idx, top_i)
        top_v = jnp.where(slot == kk, mx, top_v)
        work = jnp.where(pick, -jnp.inf, work)
    ex = jnp.exp(top_v - jnp.max(top_v, axis=0, keepdims=True))
    ti_ref[...] = top_i
    tg_ref[...] = ex / jnp.sum(ex, axis=0, keepdims=True)

    @pl.when(pl.program_id(0) == 0)
    def _():
        cnt_ref[...] = jnp.zeros_like(cnt_ref)

    chosen = jnp.where(work == -jnp.inf, 1.0, 0.0)
    before = cnt_ref[...] + jnp.dot(chosen.astype(BF16), tri_ref[...], preferred_element_type=F32)
    rank = jnp.zeros(rk_ref.shape, F32)
    for kk in range(TOP_K):
        rk = jnp.sum(jnp.where(picks[kk], before, 0.0), axis=0, keepdims=True)
        rank = jnp.where(slot == kk, rk, rank)
    rk_ref[...] = rank.astype(jnp.int32)
    cnt_ref[...] += jnp.sum(chosen, axis=1, keepdims=True)


def _merge(x, ya, yb, sg, pa, pb, wo, g2, wr_hi, wr_lo, br, *, tm):
    T, D = x.shape
    tok = lambda w: pl.BlockSpec((tm, w), lambda i: (i, 0))
    tok_t = pl.BlockSpec((TOP_K, tm), lambda i: (0, i))
    tri = jnp.triu(jnp.ones((tm, tm), F32), 1).astype(BF16)
    wr2 = jnp.concatenate([wr_hi, wr_lo], axis=1)
    return pl.pallas_call(
        _merge_kernel,
        grid=(T // tm,),
        in_specs=[tok(D), tok(HY_WIDTH), tok(ATT_WIDTH), tok(COL_G), _const_spec((HY_WIDTH, D)),
                  _const_spec((ATT_WIDTH, D)), _const_spec((D, D)), _const_spec((1, D)),
                  _const_spec((D, 2 * N_EXPERTS)), _const_spec((N_EXPERTS, 1)), _const_spec((tm, tm))],
        out_specs=[tok(D), tok(D // 2), tok_t, tok_t, tok_t, pl.BlockSpec((N_EXPERTS, 1), lambda i: (0, 0))],
        out_shape=[jax.ShapeDtypeStruct((T, D), F32), jax.ShapeDtypeStruct((T, D // 2), jnp.uint32),
                   jax.ShapeDtypeStruct((TOP_K, T), jnp.int32), jax.ShapeDtypeStruct((TOP_K, T), F32),
                   jax.ShapeDtypeStruct((TOP_K, T), jnp.int32), jax.ShapeDtypeStruct((N_EXPERTS, 1), F32)],
        scratch_shapes=[pltpu.VMEM((2, tm, 2 * N_EXPERTS), F32)],
        compiler_params=_cparams(("arbitrary",), 48),
        name="merge_router",
    )(x, ya, yb, sg, pa, pb, wo, g2, wr2, br.reshape(N_EXPERTS, 1), tri)


def _moe_kernel(iblk_ref, iexp_ref, start_ref, count_ref, nitem_ref, xs_ref, w1_ref, b1_ref, w2_ref, b2_ref, ys_ref,
                w1b_ref, w2b_ref):
    i = pl.program_id(0)
    tm = xs_ref.shape[0]

    @pl.when(jnp.logical_and(i < nitem_ref[0],
                             jnp.logical_or(i == 0, iexp_ref[jnp.maximum(i - 1, 0)] != iexp_ref[i])))
    def _():
        rows = 256
        for r in range(0, D_MODEL, rows):
            w1b_ref[r:r + rows, :] = w1_ref[r:r + rows, :].astype(BF16)
        for r in range(0, D_FF, rows):
            w2b_ref[r:r + rows, :] = w2_ref[r:r + rows, :].astype(BF16)

    @pl.when(i < nitem_ref[0])
    def _():
        blk = iblk_ref[i]
        e = iexp_ref[i]
        x = jnp.concatenate(_unpack_rows(xs_ref[...]), axis=1).astype(BF16)
        u = jnp.dot(x, w1b_ref[...], preferred_element_type=F32) + b1_ref[...]
        glu = jnp.minimum(u[:, :D_FF], SWIGLU_LIMIT)
        lin = jnp.clip(u[:, D_FF:], -SWIGLU_LIMIT, SWIGLU_LIMIT)
        a = glu * jax.nn.sigmoid(SWIGLU_ALPHA * glu) * (lin + 1.0)
        y = _pack_rows(jnp.dot(a.astype(BF16), w2b_ref[...], preferred_element_type=F32) + b2_ref[...])
        row = lax.broadcasted_iota(jnp.int32, (tm, 1), 0) + blk * tm
        mine = jnp.logical_and(row >= start_ref[e], row < start_ref[e] + count_ref[e])
        first_visit = jnp.logical_or(i == 0, iblk_ref[jnp.maximum(i - 1, 0)] != blk)

        @pl.when(first_visit)
        def _():
            ys_ref[...] = jnp.where(mine, y, jnp.zeros_like(y))

        @pl.when(jnp.logical_not(first_visit))
        def _():
            ys_ref[...] = jnp.where(mine, y, ys_ref[...])


def _moe_experts(item_blk, item_exp, start, count, n_items, xs, w1, b1, w2, b2, *, tm):
    A, D = xs.shape[0], xs.shape[1] * 2
    grid_spec = pltpu.PrefetchScalarGridSpec(
        num_scalar_prefetch=5,
        grid=(item_blk.shape[0],),
        in_specs=[pl.BlockSpec((tm, D // 2), lambda i, ib, ie, *_: (ib[i], 0)),
                  pl.BlockSpec((None, D, 2 * D_FF), lambda i, ib, ie, *_: (ie[i], 0, 0)),
                  pl.BlockSpec((None, 1, 2 * D_FF), lambda i, ib, ie, *_: (ie[i], 0, 0)),
                  pl.BlockSpec((None, D_FF, D), lambda i, ib, ie, *_: (ie[i], 0, 0)),
                  pl.BlockSpec((None, 1, D), lambda i, ib, ie, *_: (ie[i], 0, 0))],
        out_specs=pl.BlockSpec((tm, D // 2), lambda i, ib, ie, *_: (ib[i], 0)),
        scratch_shapes=[pltpu.VMEM((D, 2 * D_FF), BF16), pltpu.VMEM((D_FF, D), BF16)],
    )
    return pl.pallas_call(
        _moe_kernel,
        grid_spec=grid_spec,
        out_shape=jax.ShapeDtypeStruct((A, D // 2), jnp.uint32),
        compiler_params=_cparams(("arbitrary",), 56),
        name="moe_experts",
    )(item_blk, item_exp, start, count, n_items, xs, w1, b1, w2, b2)


def _combine_kernel(x1_ref, yg_ref, tg_ref, o_ref):
    half = D_MODEL // 2
    acc_lo = x1_ref[:, :half]
    acc_hi = x1_ref[:, half:]
    g = tg_ref[...].T
    for kk in range(TOP_K):
        lo, hi = _unpack_rows(yg_ref[kk])
        acc_lo = acc_lo + g[:, kk:kk + 1] * lo
        acc_hi = acc_hi + g[:, kk:kk + 1] * hi
    o_ref[:, :half] = acc_lo
    o_ref[:, half:] = acc_hi


def _combine(x1, yg, tg, *, tm):
    T, D = x1.shape
    return pl.pallas_call(
        _combine_kernel,
        grid=(T // tm,),
        in_specs=[pl.BlockSpec((tm, D), lambda i: (i, 0)), pl.BlockSpec((TOP_K, tm, D // 2), lambda i: (0, i, 0)),
                  pl.BlockSpec((TOP_K, tm), lambda i: (0, i))],
        out_specs=pl.BlockSpec((tm, D), lambda i: (i, 0)),
        out_shape=jax.ShapeDtypeStruct((T, D), F32),
        compiler_params=_cparams(("parallel",), 48),
        name="moe_combine",
    )(x1, yg, tg)


def _lookup(table, idx):
    n = table.shape[0]
    sel = idx[None] == jnp.arange(n, dtype=jnp.int32).reshape((n,) + (1,) * idx.ndim)
    return jnp.sum(jnp.where(sel, table.reshape((n,) + (1,) * idx.ndim), 0), axis=0)


def _route(top_i, rank, counts, *, tm):
    T = top_i.shape[1]
    A = T * TOP_K
    counts = counts.reshape(N_EXPERTS).astype(jnp.int32)
    start = jnp.cumsum(counts) - counts
    flat_e = top_i.reshape(A // LANES, LANES)
    slot_of = (_lookup(start, flat_e) + rank.reshape(A // LANES, LANES)).reshape(A)
    first_blk = start // tm
    n_it = jnp.where(counts > 0, (start + counts - 1) // tm - first_blk + 1, 0)
    it_end = jnp.cumsum(n_it)
    n_items = it_end[-1]
    n_max = A // tm + N_EXPERTS
    i = jnp.minimum(jnp.arange(n_max, dtype=jnp.int32), n_items - 1)
    item_exp = jnp.sum(it_end[None, :] <= i[:, None], axis=1, dtype=jnp.int32)
    item_blk = _lookup(first_blk, item_exp) + i - _lookup(it_end - n_it, item_exp)
    return slot_of, item_blk.astype(jnp.int32), item_exp, start, counts, n_items.reshape(1)


def _trunk(x, meta_proj, w, tabs, *, tm_prep, tm_tok, tm_moe, attn_blk):
    B, S, D = x.shape
    T = B * S
    hy_m, _, k_m, vt_m, _ = meta_proj
    assert tm_prep == HY_PAD
    front = jnp.concatenate([jnp.zeros((HY_PAD - N_META, COL_HY), BF16), hy_m[0]], axis=0)
    hy_ext, qt, k, vt, sg = _prep(x, front, w['g1'], w['w_in'], w['wqt'], w['wvt'], w['bd'], w['qg'], w['kg'],
                                  tm=tm_prep)
    kt = _hyena_filter(w['fw1'], w['fb1'], w['ffr1'], w['fw2'], w['fb2'], w['ffr2'], w['fw3'], s_real=S)
    zt, a, bt = _hy_pre(hy_ext, w['scw'], w['scb'], w['skip'], s_real=S)
    ya = _hy_post(_hy_conv(kt, zt), a, bt)
    tiles, bmeta = tabs
    yb = _attention(qt, k, vt, k_m[0], vt_m[0], tiles, bmeta, w['lam'], w['subln'], blk=attn_blk)
    x1, h2, top_i, top_g, rank, counts = _merge(
        x.reshape(T, D), ya.reshape(T, HY_WIDTH), yb.reshape(T, ATT_WIDTH), sg.reshape(T, COL_G),
        w['pa'], w['pb'], w['wo'], w['g2'], w['wr_hi'], w['wr_lo'], w['br'], tm=tm_tok)
    slot_of, item_blk, item_exp, start, cnt, n_items = _route(top_i, rank, counts, tm=tm_moe)
    xs = _sc_sorted_rows(h2, slot_of)
    ys = _moe_experts(item_blk, item_exp, start, cnt, n_items, xs, w['w1'], w['b1'], w['w2'], w['b2'], tm=tm_moe)
    yg = _sc_gather_rows(ys, slot_of).reshape(TOP_K, T, D // 2)
    return _combine(x1, yg, top_g, tm=tm_tok).reshape(B, S, D)


def kernel(x_prompt, x_sample, meta_tokens, rel_bias, norm1_g, w_in, short_conv_w, short_conv_b, filt_w1, filt_b1,
           filt_freq1, filt_w2, filt_b2, filt_freq2, filt_w3, hy_skip, proj_a, q_norm_g, k_norm_g, lambda_q1,
           lambda_k1, lambda_q2, lambda_k2, subln_g, proj_b, w_out, norm2_g, w_router, b_router, w_mlp1, b_mlp1,
           w_mlp2, b_mlp2):
    l = 0
    n_hm = COL_Q // ATT_HEAD_DIM
    w_in_bf = w_in[l].astype(BF16)
    c_q, c_v = COL_HY, COL_HY + COL_Q + COL_K
    wr_hi = w_router[l].astype(BF16)
    w = {
        'g1': norm1_g[l][None], 'w_in': w_in_bf,
        'wqt': w_in_bf[:, c_q:c_q + COL_Q].T, 'wvt': w_in_bf[:, c_v:c_v + COL_V].T,
        'bd': jnp.kron(jnp.eye(n_hm, dtype=F32), jnp.full((ATT_HEAD_DIM, ATT_HEAD_DIM), 1.0 / ATT_HEAD_DIM, F32)
                       ).astype(BF16),
        'qg': (jnp.tile(q_norm_g[l], n_hm) * (ATT_HEAD_DIM ** -0.5 * LOG2E))[:, None],
        'kg': jnp.tile(k_norm_g[l], n_hm)[None],
        'wr_hi': wr_hi, 'wr_lo': (w_router[l] - wr_hi.astype(F32)).astype(BF16),
        'fw1': filt_w1[l], 'fb1': filt_b1[l], 'ffr1': filt_freq1[l], 'fw2': filt_w2[l], 'fb2': filt_b2[l],
        'ffr2': filt_freq2[l], 'fw3': filt_w3[l],
        'scw': short_conv_w[l], 'scb': short_conv_b[l], 'skip': hy_skip[l],
        'lam': jnp.stack([lambda_q1[l], lambda_k1[l], lambda_q2[l], lambda_k2[l]]), 'subln': subln_g[l][:, None],
        'pa': proj_a[l].astype(BF16), 'pb': proj_b[l].astype(BF16), 'wo': w_out[l].astype(BF16),
        'g2': norm2_g[l][None], 'br': b_router[l][None],
        'w1': w_mlp1[l], 'b1': b_mlp1[l][:, None, :], 'w2': w_mlp2[l], 'b2': b_mlp2[l][:, None, :],
    }
    meta_proj = _prep(meta_tokens[None], None, w['g1'], w['w_in'], w['wqt'], w['wvt'], w['bd'], w['qg'], w['kg'],
                      tm=N_META)
    tabs = _attn_bias_tables(rel_bias, TILES['attn_blk'])
    y_prompt = _trunk(x_prompt, meta_proj, w, tabs, **TILES)
    y_sample = _trunk(x_sample, meta_proj, w, tabs, **TILES)
    return (y_prompt, y_sample)
```

```python
import functools
import math

import jax
import jax.numpy as jnp
from jax import lax
from jax.experimental import pallas as pl
from jax.experimental.pallas import tpu as pltpu
from jax.experimental.pallas import tpu_sc as plsc

F32 = jnp.float32
BF16 = jnp.bfloat16

D_MODEL = 1024
N_META = 16
RMS_EPS = 1e-6
HY_WIDTH = 512
HY_SHORT = 3
FILT_BANDS = 16
FILT_EMB = 1 + 2 * FILT_BANDS
FILT_HIDDEN = 64
DECAY_TARGET = 1e-2
FAST_DECAY_PCT = 0.3
SLOW_DECAY_PCT = 1.5
ATT_HEADS = 4
ATT_HEAD_DIM = 64
ATT_V_DIM = 128
ATT_WIDTH = 512
N_BUCKETS = 32
MAX_DISTANCE = 128
N_EXPERTS = 32
TOP_K = 4
D_FF = 1024
SWIGLU_LIMIT = 7.0
SWIGLU_ALPHA = 1.702
COL_HY = 3 * HY_WIDTH
COL_Q = 512
COL_K = 512
COL_V = 512
COL_G = 2 * D_MODEL
IN_COLS = COL_HY + COL_Q + COL_K + COL_V + COL_G
LAM_INIT = 0.8 - 0.6 * math.exp(-0.3 * 0)
LOG2E = 1.4426950408889634
ONES_ROWS = 16

V7X_VMEM_BYTES = 64 * 1024 * 1024
LANES = 128
CONV_BLK = 256
HY_PAD = 512
TILES = dict(tm_prep=512, tm_tok=512, tm_moe=512, attn_blk=512)


def _cparams(sem, vmem_mb):
    limit = vmem_mb * 1024 * 1024
    assert limit < V7X_VMEM_BYTES
    return pltpu.CompilerParams(dimension_semantics=sem, vmem_limit_bytes=limit)


def _const_spec(shape):
    nd = len(shape)
    return pl.BlockSpec(shape, lambda *_: (0,) * nd, pipeline_mode=pl.Buffered(1))


def _pack_rows(x):
    n = x.shape[1] // 2
    lo = pltpu.bitcast(x[:, :n].astype(BF16).astype(F32), jnp.uint32)
    hi = pltpu.bitcast(x[:, n:].astype(BF16).astype(F32), jnp.uint32)
    return (hi & jnp.uint32(0xFFFF0000)) | (lo >> 16)


def _unpack_rows(w):
    lo = pltpu.bitcast(w << 16, F32)
    hi = pltpu.bitcast(w & jnp.uint32(0xFFFF0000), F32)
    return lo, hi


SC_ROWS = 64


def _sc_workers():
    info = plsc.get_sparse_core_info()
    return info, info.num_cores * info.num_subcores


def _sc_stream_rows(table_hbm, out_hbm, rows_v, sems, base, n_chunks, load_index, index_ref):
    def rows_copy(j, buf):
        return pltpu.make_async_copy(table_hbm.at[index_ref(j, buf)], rows_v.at[buf], sems.at[buf])

    def fetch(j, buf):
        load_index(j, buf)
        rows_copy(j, buf).start()

    def drain(j, buf):
        rows_copy(j, buf).wait()
        pltpu.sync_copy(rows_v.at[buf], out_hbm.at[pl.ds(pl.multiple_of(base + j * SC_ROWS, SC_ROWS), SC_ROWS)])

    fetch(0, 0)

    @pl.loop(0, n_chunks // 2)
    def _(jj):
        j = 2 * jj
        fetch(j + 1, 1)
        drain(j, 0)

        @pl.when(j + 2 < n_chunks)
        def _():
            fetch(j + 2, 0)

        drain(j + 1, 1)


def _sc_sorted_rows(table, slot_of, *, scan=16384):
    info, n_workers = _sc_workers()
    lanes = info.num_lanes
    n_tok, width = table.shape
    n_picks = slot_of.shape[0]
    per_worker = n_picks // n_workers
    n_chunks = per_worker // SC_ROWS
    assert n_tok % scan == 0 and n_picks % scan == 0 and scan % lanes == 0
    assert n_chunks * SC_ROWS * n_workers == n_picks and n_chunks % 2 == 0
    mesh = plsc.VectorSubcoreMesh(core_axis_name="c", subcore_axis_name="s")

    @functools.partial(
        pl.kernel, mesh=mesh, out_type=jax.ShapeDtypeStruct((n_picks, width), table.dtype),
        scratch_types=[pltpu.VMEM((scan,), jnp.int32), pltpu.VMEM((per_worker,), jnp.int32),
                       pltpu.VMEM((2, SC_ROWS, width), table.dtype), pltpu.SemaphoreType.DMA((2,))],
        compiler_params=pltpu.CompilerParams(needs_layout_passes=False))
    def sorted_rows(table_hbm, slot_hbm, out_hbm, picks_v, tok_v, rows_v, sems):
        base = (lax.axis_index("s") * info.num_cores + lax.axis_index("c")) * per_worker
        lane = lax.iota(jnp.int32, lanes)

        @pl.loop(0, per_worker // lanes)
        def _(i):
            tok_v[pl.ds(i * lanes, lanes)] = jnp.zeros((lanes,), jnp.int32)

        @pl.loop(0, n_picks // scan)
        def _(ci):
            a0 = pl.multiple_of(ci * scan, scan)
            pltpu.sync_copy(slot_hbm.at[pl.ds(a0, scan)], picks_v)
            t0 = lax.rem(a0, n_tok)

            @pl.loop(0, scan // lanes)
            def _(i):
                row = picks_v[pl.ds(i * lanes, lanes)] - base
                mine = jnp.logical_and(row >= 0, row < per_worker)
                plsc.store_scatter(tok_v, [jnp.where(mine, row, 0)], t0 + i * lanes + lane, mask=mine)

        _sc_stream_rows(table_hbm, out_hbm, rows_v, sems, base, n_chunks,
                        load_index=lambda j, buf: None,
                        index_ref=lambda j, buf: tok_v.at[pl.ds(pl.multiple_of(j * SC_ROWS, SC_ROWS), SC_ROWS)])

    return sorted_rows(table, slot_of)


def _sc_gather_rows(table, idx):
    info, n_workers = _sc_workers()
    n_rows, width = idx.shape[0], table.shape[1]
    per_worker = n_rows // n_workers
    n_chunks = per_worker // SC_ROWS
    assert n_chunks * SC_ROWS * n_workers == n_rows and n_chunks % 2 == 0
    mesh = plsc.VectorSubcoreMesh(core_axis_name="c", subcore_axis_name="s")

    @functools.partial(
        pl.kernel, mesh=mesh, out_type=jax.ShapeDtypeStruct((n_rows, width), table.dtype),
        scratch_types=[pltpu.VMEM((2, SC_ROWS), jnp.int32), pltpu.VMEM((2, SC_ROWS, width), table.dtype),
                       pltpu.SemaphoreType.DMA((2,))])
    def gather(table_hbm, idx_hbm, out_hbm, idx_v, rows_v, sems):
        base = (lax.axis_index("s") * info.num_cores + lax.axis_index("c")) * per_worker

        def load_index(j, buf):
            off = pl.multiple_of(base + j * SC_ROWS, SC_ROWS)
            pltpu.sync_copy(idx_hbm.at[pl.ds(off, SC_ROWS)], idx_v.at[buf])

        _sc_stream_rows(table_hbm, out_hbm, rows_v, sems, base, n_chunks,
                        load_index=load_index, index_ref=lambda j, buf: idx_v.at[buf])

    return gather(table, idx)


def _prep_kernel(x_ref, front_ref, g1_ref, w_ref, wqt_ref, wvt_ref, bd_ref, qg_ref, kg_ref,
                 hy_ref, qt_ref, k_ref, vt_ref, sg_ref, *, lead):
    if lead:
        @pl.when(pl.program_id(1) == 0)
        def _():
            hy_ref[...] = front_ref[...]

        @pl.when(pl.program_id(1) > 0)
        def _():
            _prep_tile(x_ref, g1_ref, w_ref, wqt_ref, wvt_ref, bd_ref, qg_ref, kg_ref,
                       hy_ref, qt_ref, k_ref, vt_ref, sg_ref)
    else:
        _prep_tile(x_ref, g1_ref, w_ref, wqt_ref, wvt_ref, bd_ref, qg_ref, kg_ref,
                   hy_ref, qt_ref, k_ref, vt_ref, sg_ref)


def _prep_tile(x_ref, g1_ref, w_ref, wqt_ref, wvt_ref, bd_ref, qg_ref, kg_ref,
               hy_ref, qt_ref, k_ref, vt_ref, sg_ref):
    x = x_ref[...]
    ms = jnp.mean(x * x, axis=-1, keepdims=True)
    h = (x * lax.rsqrt(ms + RMS_EPS) * g1_ref[...]).astype(BF16)
    nt_dims = (((1,), (1,)), ((), ()))

    def proj(lo, hi):
        return jnp.dot(h, w_ref[:, lo:hi], preferred_element_type=F32)

    def proj_t(wt_ref):
        return lax.dot_general(wt_ref[...], h, nt_dims, preferred_element_type=F32)

    c0, c1, c2, c3 = COL_HY, COL_HY + COL_Q, COL_HY + COL_Q + COL_K, COL_HY + COL_Q + COL_K + COL_V
    hy_ref[...] = proj(0, c0).astype(BF16)
    qt = proj_t(wqt_ref)
    q3 = qt.reshape(COL_Q // ATT_HEAD_DIM, ATT_HEAD_DIM, qt.shape[1])
    q3 = q3 * lax.rsqrt(jnp.mean(q3 * q3, axis=1, keepdims=True) + RMS_EPS)
    qt_ref[...] = (q3.reshape(qt.shape) * qg_ref[...]).astype(BF16)
    kk = proj(c1, c2)
    msk = jnp.dot((kk * kk).astype(BF16), bd_ref[...], preferred_element_type=F32)
    k_ref[...] = (kk * lax.rsqrt(msk + RMS_EPS) * kg_ref[...]).astype(BF16)
    vt_ref[...] = proj_t(wvt_ref).astype(BF16)
    sg_ref[...] = jax.nn.sigmoid(proj(c3, IN_COLS)).astype(BF16)


def _prep(x, front, g1, w_in_bf, wqt, wvt, bd, qg_col, kg, *, tm):
    B, S, D = x.shape
    nt = S // tm
    lead = 0 if front is None else 1
    if front is None:
        front = jnp.zeros((tm, COL_HY), BF16)
    src = lambda i: jnp.maximum(i - lead, 0)
    tok = lambda w: pl.BlockSpec((None, tm, w), lambda b, i: (b, src(i), 0))
    tok_t = lambda w: pl.BlockSpec((None, w, tm), lambda b, i: (b, 0, src(i)))
    return pl.pallas_call(
        functools.partial(_prep_kernel, lead=lead),
        grid=(B, nt + lead),
        in_specs=[tok(D), _const_spec((tm, COL_HY)), _const_spec((1, D)), _const_spec((D, IN_COLS)),
                  _const_spec((COL_Q, D)), _const_spec((COL_V, D)), _const_spec((COL_Q, COL_Q)),
                  _const_spec((COL_Q, 1)), _const_spec((1, COL_K))],
        out_specs=[pl.BlockSpec((None, tm, COL_HY), lambda b, i: (b, i, 0)),
                   tok_t(COL_Q), tok(COL_K), tok_t(COL_V), tok(COL_G)],
        out_shape=[jax.ShapeDtypeStruct((B, S + lead * tm, COL_HY), BF16),
                   jax.ShapeDtypeStruct((B, COL_Q, S), BF16), jax.ShapeDtypeStruct((B, S, COL_K), BF16),
                   jax.ShapeDtypeStruct((B, COL_V, S), BF16), jax.ShapeDtypeStruct((B, S, COL_G), BF16)],
        compiler_params=_cparams(("parallel", "arbitrary"), 48),
        name="prep",
    )(x, front, g1, w_in_bf, wqt, wvt, bd, qg_col, kg)


def _filt_kernel(w1t_ref, w1c_ref, w1s_ref, b1_ref, fr1_ref, w2t_ref, b2_ref, fr2_ref, w3f_ref, w3b_ref,
                 bands_ref, deltas_ref, kt_ref, hid_ref, *, n_seq, s_real):
    lk = kt_ref.shape[1]
    hi = lax.Precision.HIGHEST
    u = lax.broadcasted_iota(jnp.int32, (1, lk), 1)
    d = u - s_real
    pos = jnp.abs(d).astype(F32)
    t = pos / float(max(n_seq - 1, 1))

    @pl.when(pl.program_id(0) == 0)
    def _():
        w = 2.0 * math.pi * pos / float(n_seq)
        ang = bands_ref[...] * w
        pre = (w1t_ref[...] * t
               + jnp.dot(w1c_ref[...], jnp.cos(ang), precision=hi, preferred_element_type=F32)
               - jnp.dot(w1s_ref[...], jnp.sin(ang), precision=hi, preferred_element_type=F32)
               + b1_ref[...])
        h1 = jnp.sin(fr1_ref[...] * pre)
        h2 = jnp.sin(fr2_ref[...] * (jnp.dot(w2t_ref[...], h1, precision=hi, preferred_element_type=F32)
                                     + b2_ref[...]))
        hid_ref[...] = h2

    h2 = hid_ref[...]
    hf = jnp.dot(w3f_ref[...], h2, precision=hi, preferred_element_type=F32)
    hb = jnp.dot(w3b_ref[...], h2, precision=hi, preferred_element_type=F32)
    decay = jnp.exp(-t * deltas_ref[...])
    valid = pos <= float(n_seq - 1)
    kun = jnp.where(valid, jnp.where(d >= 0, hf, hb) * decay, 0.0)
    nrm = lax.rsqrt(jnp.sum(kun * kun, axis=1, keepdims=True) + RMS_EPS)
    kt_ref[...] = kun * nrm


def _hyena_filter(fw1, fb1, ffr1, fw2, fb2, ffr2, fw3, *, s_real):
    n_seq = s_real + N_META
    lk = 2 * s_real + CONV_BLK
    cf = 64
    col = lambda a: a.reshape(-1, 1)
    w1t = fw1.T
    bands = jnp.linspace(1e-4, FILT_BANDS - 1, FILT_BANDS, dtype=F32).reshape(-1, 1)
    max_decay = math.log(DECAY_TARGET) / FAST_DECAY_PCT
    min_decay = math.log(DECAY_TARGET) / SLOW_DECAY_PCT
    deltas = jnp.abs(jnp.linspace(min_decay, max_decay, HY_WIDTH, dtype=F32)).reshape(-1, 1)
    w3t = fw3.T
    h = FILT_HIDDEN
    return pl.pallas_call(
        functools.partial(_filt_kernel, n_seq=n_seq, s_real=s_real),
        grid=(HY_WIDTH // cf,),
        in_specs=[_const_spec((h, 1)), _const_spec((h, FILT_BANDS)), _const_spec((h, FILT_BANDS)),
                  _const_spec((h, 1)), _const_spec((h, 1)), _const_spec((h, h)), _const_spec((h, 1)),
                  _const_spec((h, 1)),
                  pl.BlockSpec((cf, h), lambda i: (i, 0)), pl.BlockSpec((cf, h), lambda i: (i, 0)),
                  _const_spec((FILT_BANDS, 1)), pl.BlockSpec((cf, 1), lambda i: (i, 0))],
        out_specs=pl.BlockSpec((cf, lk), lambda i: (i, 0)),
        out_shape=jax.ShapeDtypeStruct((HY_WIDTH, lk), F32),
        scratch_shapes=[pltpu.VMEM((h, lk), F32)],
        compiler_params=_cparams(("arbitrary",), 48),
        name="hyena_filter",
    )(w1t[:, 0:1], w1t[:, 1:1 + FILT_BANDS], w1t[:, 1 + FILT_BANDS:], col(fb1), col(ffr1), fw2.T, col(fb2),
      col(ffr2), w3t[:HY_WIDTH], w3t[HY_WIDTH:], bands, deltas)


def _hy_pre_kernel(x0_ref, x1_ref, v_ref, p0_ref, p1_ref, pv_ref, n0_ref, n1_ref, nv_ref,
                   scw_ref, scb_ref, skip_ref, zt_ref, a_ref, bt_ref, zs_ref):
    jt = pl.program_id(1)
    last = pl.num_programs(1) - 1
    nb_batch = x0_ref.shape[0]
    row = lax.broadcasted_iota(jnp.int32, (CONV_BLK, LANES), 0)
    keep_next = (jt < last).astype(F32)
    z_keep = jnp.logical_or(jt > 0, row >= CONV_BLK - N_META)

    def short_conv(cur_ref, prev_ref, next_ref, comp, b):
        cur = cur_ref[b].astype(F32)
        prev = prev_ref[b, 15:16, :].astype(F32)
        nxt = next_ref[b, 0:1, :].astype(F32) * keep_next
        up = jnp.where(row == 0, prev, pltpu.roll(cur, 1, 0))
        dn = jnp.where(row == CONV_BLK - 1, nxt, pltpu.roll(cur, CONV_BLK - 1, 0))
        w = scw_ref[comp]
        return up * w[0:1] + cur * w[1:2] + dn * w[2:3] + scb_ref[comp]

    def body(b, carry):
        x0c = short_conv(x0_ref, p0_ref, n0_ref, 0, b)
        x1c = short_conv(x1_ref, p1_ref, n1_ref, 1, b)
        vc = short_conv(v_ref, pv_ref, nv_ref, 2, b)
        z = jnp.where(z_keep, x1c * vc, 0.0)
        zs_ref[b] = z.T
        a_ref[b] = x0c.astype(BF16)
        bt_ref[b] = (x0c * z * skip_ref[...]).astype(BF16)
        return carry

    lax.fori_loop(0, nb_batch, body, 0)
    zt_ref[...] = pltpu.einshape("bcl->cbl", zs_ref[...]).astype(BF16)


def _hy_pre(hy_ext, scw, scb, skip, *, s_real):
    B = hy_ext.shape[0]
    nb = s_real // CONV_BLK
    ncb = HY_WIDTH // LANES
    sub = 16
    r16 = CONV_BLK // sub
    last16 = (s_real + HY_PAD) // sub - 1
    cur = lambda comp: pl.BlockSpec((B, CONV_BLK, LANES), lambda cb, jt: (0, jt + 1, comp * ncb + cb))
    prv = lambda comp: pl.BlockSpec((B, sub, LANES), lambda cb, jt: (0, r16 * (jt + 1) - 1, comp * ncb + cb))
    nxt = lambda comp: pl.BlockSpec(
        (B, sub, LANES), lambda cb, jt: (0, jnp.minimum(r16 * (jt + 2), last16), comp * ncb + cb))
    nat = pl.BlockSpec((B, CONV_BLK, LANES), lambda cb, jt: (0, jnp.maximum(jt - 1, 0), cb))
    scw_r = scw.reshape(HY_SHORT, 3, ncb, LANES).transpose(2, 1, 0, 3)
    scb_r = scb.reshape(3, ncb, 1, LANES).transpose(1, 0, 2, 3)
    skip_r = skip.reshape(ncb, 1, LANES)
    return pl.pallas_call(
        _hy_pre_kernel,
        grid=(ncb, nb + 1),
        in_specs=[cur(0), cur(1), cur(2), prv(0), prv(1), prv(2), nxt(0), nxt(1), nxt(2),
                  pl.BlockSpec((None, 3, HY_SHORT, LANES), lambda cb, jt: (cb, 0, 0, 0)),
                  pl.BlockSpec((None, 3, 1, LANES), lambda cb, jt: (cb, 0, 0, 0)),
                  pl.BlockSpec((None, 1, LANES), lambda cb, jt: (cb, 0, 0))],
        out_specs=[pl.BlockSpec((LANES, None, B, CONV_BLK), lambda cb, jt: (cb, jt, 0, 0)), nat, nat],
        out_shape=[jax.ShapeDtypeStruct((HY_WIDTH, nb + 1, B, CONV_BLK), BF16),
                   jax.ShapeDtypeStruct((B, s_real, HY_WIDTH), BF16),
                   jax.ShapeDtypeStruct((B, s_real, HY_WIDTH), BF16)],
        scratch_shapes=[pltpu.VMEM((B, LANES, CONV_BLK), F32)],
        compiler_params=_cparams(("parallel", "arbitrary"), 48),
        name="hyena_pre",
    )(hy_ext, hy_ext, hy_ext, hy_ext, hy_ext, hy_ext, hy_ext, hy_ext, hy_ext, scw_r, scb_r, skip_r)


def _hy_conv_kernel(kt_ref, z_ref, y_ref, g_ref, *, nb):
    cbk = z_ref.shape[0]
    B = z_ref.shape[2]
    s_real = nb * CONV_BLK
    nq = 2 * s_real // LANES + 1
    upper = (lax.broadcasted_iota(jnp.int32, (LANES, LANES), 1)
             >= lax.broadcasted_iota(jnp.int32, (LANES, LANES), 0))

    def channel(c, carry):
        kt_row = kt_ref[pl.ds(c, 1), :]
        for q in range(nq):
            off = 2 * s_real - LANES * q
            skew = lambda lo: pltpu.roll(jnp.broadcast_to(kt_row[:, lo:lo + LANES], (LANES, LANES)),
                                         0, 1, stride=1, stride_axis=0)
            blk_g = jnp.where(upper, skew(off + LANES), skew(off))
            g_ref[LANES * q:LANES * (q + 1), :] = blk_g.astype(BF16)
        y_ref[c] = jnp.zeros(y_ref.shape[1:], F32)
        for delta in range(-(nb - 1), nb + 1):
            base = CONV_BLK * (nb - delta)
            tile = jnp.concatenate([g_ref[base + LANES:base + LANES + CONV_BLK, :],
                                    g_ref[base:base + CONV_BLK, :]], axis=1)
            jb_lo = max(0, 1 - delta)
            n = min(nb, nb - delta) - jb_lo + 1
            o_lo = jb_lo + delta - 1
            lhs = z_ref[c, jb_lo:jb_lo + n].reshape(n * B, CONV_BLK)
            y_ref[c, o_lo:o_lo + n] += jnp.dot(lhs, tile, preferred_element_type=F32).reshape(n, B, CONV_BLK)
        return carry

    lax.fori_loop(0, cbk, channel, 0)


def _hy_conv(kt, zt, *, cbk=8):
    C, nb1, B, _ = zt.shape
    nb = nb1 - 1
    lk = kt.shape[1]
    return pl.pallas_call(
        functools.partial(_hy_conv_kernel, nb=nb),
        grid=(C // cbk,),
        in_specs=[pl.BlockSpec((cbk, lk), lambda i: (i, 0)),
                  pl.BlockSpec((cbk, nb1, B, CONV_BLK), lambda i: (i, 0, 0, 0))],
        out_specs=pl.BlockSpec((cbk, nb, B, CONV_BLK), lambda i: (i, 0, 0, 0)),
        out_shape=jax.ShapeDtypeStruct((C, nb, B, CONV_BLK), F32),
        scratch_shapes=[pltpu.VMEM((2 * nb * CONV_BLK + LANES, LANES), BF16)],
        compiler_params=_cparams(("parallel",), 48),
        name="hyena_conv",
    )(kt, zt)


def _hy_post_kernel(yt_ref, a_ref, bt_ref, o_ref, ys_ref):
    ys_ref[...] = pltpu.einshape("cbl->bcl", yt_ref[...])

    def body(b, carry):
        y = ys_ref[b].T
        o_ref[b] = (a_ref[b].astype(F32) * y + bt_ref[b].astype(F32)).astype(BF16)
        return carry

    lax.fori_loop(0, a_ref.shape[0], body, 0)


def _hy_post(yt, a, bt):
    C, nb, B, _ = yt.shape
    nat = pl.BlockSpec((B, CONV_BLK, LANES), lambda cb, ib: (0, ib, cb))
    return pl.pallas_call(
        _hy_post_kernel,
        grid=(C // LANES, nb),
        in_specs=[pl.BlockSpec((LANES, None, B, CONV_BLK), lambda cb, ib: (cb, ib, 0, 0)), nat, nat],
        out_specs=nat,
        out_shape=jax.ShapeDtypeStruct(a.shape, BF16),
        scratch_shapes=[pltpu.VMEM((B, LANES, CONV_BLK), F32)],
        compiler_params=_cparams(("parallel", "parallel"), 48),
        name="hyena_post",
    )(yt, a, bt)


def _attn_kernel(qt_ref, k_ref, vt_ref, km_ref, vtm_ref, bias_ref, bmeta_ref, lam_ref, sg_ref,
                 o_ref, m_ref, acc_ref, s_ref, bm_ref, *, blk):
    i = pl.program_id(2)
    nk = k_ref.shape[0] // blk
    assert nk % 2 == 0
    qt = qt_ref[...]
    rowi = lax.broadcasted_iota(jnp.int32, qt.shape, 0)
    zero = jnp.zeros_like(qt)
    qmaps = (jnp.where(rowi < ATT_HEAD_DIM, qt, zero), jnp.where(rowi >= ATT_HEAD_DIM, qt, zero))

    def with_ones(vt):
        return jnp.concatenate([vt, jnp.ones((ONES_ROWS, vt.shape[1]), BF16)], axis=0)

    kmeta, vtmeta, bias_m = km_ref[...], with_ones(vtm_ref[...]), bmeta_ref[jnp.minimum(i, 1)]
    for c in range(2):
        s = jnp.dot(kmeta, qmaps[c], preferred_element_type=F32) + bias_m
        m0 = jnp.max(s, axis=0, keepdims=True)
        m_ref[c] = m0
        acc_ref[c] = jnp.dot(vtmeta, jnp.exp2(s - m0).astype(BF16), preferred_element_type=F32)

    def scores(j, slot):
        kblk = k_ref[pl.ds(pl.multiple_of(j * blk, blk), blk), :]
        bias = bias_ref[jnp.clip(j - i, -2, 2) + 2]
        for c in range(2):
            s = jnp.dot(kblk, qmaps[c], preferred_element_type=F32) + bias
            s_ref[slot, c] = s
            bm_ref[slot, c] = jnp.max(s, axis=0, keepdims=True)

    def consume(j, slot):
        vtblk = with_ones(vt_ref[:, pl.ds(pl.multiple_of(j * blk, blk), blk)])
        for c in range(2):
            m_old = m_ref[c]
            m_new = jnp.maximum(m_old, bm_ref[slot, c])
            p = jnp.exp2(s_ref[slot, c] - m_new)
            acc_ref[c] = (jnp.exp2(m_old - m_new) * acc_ref[c]
                          + jnp.dot(vtblk, p.astype(BF16), preferred_element_type=F32))
            m_ref[c] = m_new

    scores(0, 0)

    def pair(jj, carry):
        j = 2 * jj
        scores(j + 1, 1)
        consume(j, 0)
        scores(j + 2, 0)
        consume(j + 1, 1)
        return carry

    lax.fori_loop(0, nk // 2 - 1, pair, 0)
    scores(nk - 1, 1)
    consume(nk - 2, 0)
    consume(nk - 1, 1)

    lq = lam_ref[...]
    lam = (jnp.exp(jnp.sum(lq[0:1] * lq[1:2], axis=1, keepdims=True))
           - jnp.exp(jnp.sum(lq[2:3] * lq[3:4], axis=1, keepdims=True)) + LAM_INIT)
    v_rows = slice(0, ATT_V_DIM)
    l_row = slice(ATT_V_DIM, ATT_V_DIM + 1)
    ot = acc_ref[0, v_rows] / acc_ref[0, l_row] - lam * (acc_ref[1, v_rows] / acc_ref[1, l_row])
    ot = ot * lax.rsqrt(jnp.mean(ot * ot, axis=0, keepdims=True) + RMS_EPS) * sg_ref[...] * (1.0 - LAM_INIT)
    o_ref[...] = ot.T.astype(BF16)


def _t5_bucket(rel):
    nb = N_BUCKETS // 2
    max_exact = nb // 2
    ret = jnp.where(rel > 0, nb, 0)
    n = jnp.abs(rel)
    nf = jnp.maximum(n, 1).astype(F32)
    large = max_exact + (jnp.log(nf / max_exact) / math.log(MAX_DISTANCE / max_exact)
                         * (nb - max_exact)).astype(jnp.int32)
    large = jnp.minimum(large, nb - 1)
    return ret + jnp.where(n < max_exact, n, large)


def _attn_bias_tables(rel_bias, blk):
    assert blk >= MAX_DISTANCE
    table = rel_bias.astype(F32) * LOG2E

    def bias_of(rel):
        bucket = _t5_bucket(rel)
        sel = bucket[None] == jnp.arange(N_BUCKETS, dtype=jnp.int32).reshape((N_BUCKETS,) + (1,) * rel.ndim)
        return jnp.stack([jnp.sum(jnp.where(sel, table[:, h].reshape((N_BUCKETS,) + (1,) * rel.ndim), 0.0), axis=0)
                          for h in range(ATT_HEADS)])

    r = jnp.arange(blk, dtype=jnp.int32)
    rel = jnp.stack([dj * blk + r[:, None] - r[None, :] for dj in (-2, -1, 0, 1, 2)])
    tiles = bias_of(rel)
    m = jnp.arange(N_META, dtype=jnp.int32)
    rel_m0 = m[:, None] - (N_META + r[None, :])
    rel_m1 = rel_m0 - blk
    bmeta = bias_of(jnp.stack([rel_m0, rel_m1]))
    return tiles, bmeta


def _attention(qt, k, vt, kmeta, vtmeta, tiles, bmeta, lam, subln_col, *, blk):
    B, S, _ = k.shape
    return pl.pallas_call(
        functools.partial(_attn_kernel, blk=blk),
        grid=(ATT_HEADS, B, S // blk),
        in_specs=[pl.BlockSpec((None, ATT_V_DIM, blk), lambda h, b, i: (b, h, i)),
                  pl.BlockSpec((None, S, ATT_V_DIM), lambda h, b, i: (b, 0, h)),
                  pl.BlockSpec((None, ATT_V_DIM, S), lambda h, b, i: (b, h, 0)),
                  pl.BlockSpec((N_META, ATT_V_DIM), lambda h, b, i: (0, h)),
                  pl.BlockSpec((ATT_V_DIM, N_META), lambda h, b, i: (h, 0)),
                  pl.BlockSpec((None, 5, blk, blk), lambda h, b, i: (h, 0, 0, 0)),
                  pl.BlockSpec((None, 2, N_META, blk), lambda h, b, i: (h, 0, 0, 0)),
                  pl.BlockSpec((4, ATT_HEAD_DIM), lambda h, b, i: (0, 0)),
                  pl.BlockSpec((ATT_V_DIM, 1), lambda h, b, i: (0, 0))],
        out_specs=pl.BlockSpec((None, blk, ATT_V_DIM), lambda h, b, i: (b, i, h)),
        scratch_shapes=[pltpu.VMEM((2, 1, blk), F32), pltpu.VMEM((2, ATT_V_DIM + ONES_ROWS, blk), F32),
                        pltpu.VMEM((2, 2, blk, blk), F32), pltpu.VMEM((2, 2, 1, blk), F32)],
        out_shape=jax.ShapeDtypeStruct((B, S, ATT_WIDTH), BF16),
        compiler_params=_cparams(("parallel", "parallel", "arbitrary"), 56),
        name="diff_attention",
    )(qt, k, vt, kmeta, vtmeta, tiles, bmeta, lam, subln_col)


def _merge_kernel(x_ref, ya_ref, yb_ref, sg_ref, pa_ref, pb_ref, wo_ref, g2_ref, wr2_ref, br_ref,
                  tri_ref, x1_ref, h2_ref, ti_ref, tg_ref, rk_ref, cnt_ref, lg_ref):
    ma = jnp.dot(ya_ref[...], pa_ref[...], preferred_element_type=F32)
    mb = jnp.dot(yb_ref[...], pb_ref[...], preferred_element_type=F32)
    m = sg_ref[:, :D_MODEL].astype(F32) * ma + sg_ref[:, D_MODEL:].astype(F32) * mb
    x1 = x_ref[...] + jnp.dot(m.astype(BF16), wo_ref[...], preferred_element_type=F32)
    x1_ref[...] = x1
    h2 = x1 * lax.rsqrt(jnp.mean(x1 * x1, axis=-1, keepdims=True) + RMS_EPS) * g2_ref[...]
    h2_hi = h2.astype(BF16)
    h2_ref[...] = _pack_rows(h2)
    h2_lo = (h2 - h2_hi.astype(F32)).astype(BF16)
    wr2 = wr2_ref[...]
    lg_ref[0] = jnp.dot(h2_hi, wr2, preferred_element_type=F32)
    lg_ref[1] = jnp.dot(h2_lo, wr2, preferred_element_type=F32)
    hi_t = lg_ref[0].T
    lo_t = lg_ref[1].T
    logits = hi_t[:N_EXPERTS] + hi_t[N_EXPERTS:] + lo_t[:N_EXPERTS] + br_ref[...]
    row = lax.broadcasted_iota(jnp.int32, logits.shape, 0)
    slot = lax.broadcasted_iota(jnp.int32, ti_ref.shape, 0)
    top_i = jnp.zeros(ti_ref.shape, jnp.int32)
    top_v = jnp.zeros(tg_ref.shape, F32)
    work = logits
    picks = []
    for kk in range(TOP_K):
        mx = jnp.max(work, axis=0, keepdims=True)
        idx = jnp.min(jnp.where(work == mx, row, N_EXPERTS), axis=0, keepdims=True)
        pick = row == idx
        picks.append(pick)
        top_i = jnp.where(slot == kk, idx, top_i)
        top_v = jnp.where(slot == kk, mx, top_v)
        work = jnp.where(pick, -jnp.inf, work)
    ex = jnp.exp(top_v - jnp.max(top_v, axis=0, keepdims=True))
    ti_ref[...] = top_i
    tg_ref[...] = ex / jnp.sum(ex, axis=0, keepdims=True)

    @pl.when(pl.program_id(0) == 0)
    def _():
        cnt_ref[...] = jnp.zeros_like(cnt_ref)

    chosen = jnp.where(work == -jnp.inf, 1.0, 0.0)
    before = cnt_ref[...] + jnp.dot(chosen.astype(BF16), tri_ref[...], preferred_element_type=F32)
    rank = jnp.zeros(rk_ref.shape, F32)
    for kk in range(TOP_K):
        rk = jnp.sum(jnp.where(picks[kk], before, 0.0), axis=0, keepdims=True)
        rank = jnp.where(slot == kk, rk, rank)
    rk_ref[...] = rank.astype(jnp.int32)
    cnt_ref[...] += jnp.sum(chosen, axis=1, keepdims=True)


def _merge(x, ya, yb, sg, pa, pb, wo, g2, wr_hi, wr_lo, br, *, tm):
    T, D = x.shape
    tok = lambda w: pl.BlockSpec((tm, w), lambda i: (i, 0))
    tok_t = pl.BlockSpec((TOP_K, tm), lambda i: (0, i))
    tri = jnp.triu(jnp.ones((tm, tm), F32), 1).astype(BF16)
    wr2 = jnp.concatenate([wr_hi, wr_lo], axis=1)
    return pl.pallas_call(
        _merge_kernel,
        grid=(T // tm,),
        in_specs=[tok(D), tok(HY_WIDTH), tok(ATT_WIDTH), tok(COL_G), _const_spec((HY_WIDTH, D)),
                  _const_spec((ATT_WIDTH, D)), _const_spec((D, D)), _const_spec((1, D)),
                  _const_spec((D, 2 * N_EXPERTS)), _const_spec((N_EXPERTS, 1)), _const_spec((tm, tm))],
        out_specs=[tok(D), tok(D // 2), tok_t, tok_t, tok_t, pl.BlockSpec((N_EXPERTS, 1), lambda i: (0, 0))],
        out_shape=[jax.ShapeDtypeStruct((T, D), F32), jax.ShapeDtypeStruct((T, D // 2), jnp.uint32),
                   jax.ShapeDtypeStruct((TOP_K, T), jnp.int32), jax.ShapeDtypeStruct((TOP_K, T), F32),
                   jax.ShapeDtypeStruct((TOP_K, T), jnp.int32), jax.ShapeDtypeStruct((N_EXPERTS, 1), F32)],
        scratch_shapes=[pltpu.VMEM((2, tm, 2 * N_EXPERTS), F32)],
        compiler_params=_cparams(("arbitrary",), 48),
        name="merge_router",
    )(x, ya, yb, sg, pa, pb, wo, g2, wr2, br.reshape(N_EXPERTS, 1), tri)


def _moe_kernel(iblk_ref, iexp_ref, start_ref, count_ref, nitem_ref, xs_ref, w1_ref, b1_ref, w2_ref, b2_ref, ys_ref,
                w1b_ref, w2b_ref):
    i = pl.program_id(0)
    tm = xs_ref.shape[0]

    @pl.when(jnp.logical_and(i < nitem_ref[0],
                             jnp.logical_or(i == 0, iexp_ref[jnp.maximum(i - 1, 0)] != iexp_ref[i])))
    def _():
        rows = 256
        for r in range(0, D_MODEL, rows):
            w1b_ref[r:r + rows, :] = w1_ref[r:r + rows, :].astype(BF16)
        for r in range(0, D_FF, rows):
            w2b_ref[r:r + rows, :] = w2_ref[r:r + rows, :].astype(BF16)

    @pl.when(i < nitem_ref[0])
    def _():
        blk = iblk_ref[i]
        e = iexp_ref[i]
        x = jnp.concatenate(_unpack_rows(xs_ref[...]), axis=1).astype(BF16)
        u = jnp.dot(x, w1b_ref[...], preferred_element_type=F32) + b1_ref[...]
        glu = jnp.minimum(u[:, :D_FF], SWIGLU_LIMIT)
        lin = jnp.clip(u[:, D_FF:], -SWIGLU_LIMIT, SWIGLU_LIMIT)
        a = glu * jax.nn.sigmoid(SWIGLU_ALPHA * glu) * (lin + 1.0)
        y = _pack_rows(jnp.dot(a.astype(BF16), w2b_ref[...], preferred_element_type=F32) + b2_ref[...])
        row = lax.broadcasted_iota(jnp.int32, (tm, 1), 0) + blk * tm
        mine = jnp.logical_and(row >= start_ref[e], row < start_ref[e] + count_ref[e])
        first_visit = jnp.logical_or(i == 0, iblk_ref[jnp.maximum(i - 1, 0)] != blk)

        @pl.when(first_visit)
        def _():
            ys_ref[...] = jnp.where(mine, y, jnp.zeros_like(y))

        @pl.when(jnp.logical_not(first_visit))
        def _():
            ys_ref[...] = jnp.where(mine, y, ys_ref[...])


def _moe_experts(item_blk, item_exp, start, count, n_items, xs, w1, b1, w2, b2, *, tm):
    A, D = xs.shape[0], xs.shape[1] * 2
    grid_spec = pltpu.PrefetchScalarGridSpec(
        num_scalar_prefetch=5,
        grid=(item_blk.shape[0],),
        in_specs=[pl.BlockSpec((tm, D // 2), lambda i, ib, ie, *_: (ib[i], 0)),
                  pl.BlockSpec((None, D, 2 * D_FF), lambda i, ib, ie, *_: (ie[i], 0, 0)),
                  pl.BlockSpec((None, 1, 2 * D_FF), lambda i, ib, ie, *_: (ie[i], 0, 0)),
                  pl.BlockSpec((None, D_FF, D), lambda i, ib, ie, *_: (ie[i], 0, 0)),
                  pl.BlockSpec((None, 1, D), lambda i, ib, ie, *_: (ie[i], 0, 0))],
        out_specs=pl.BlockSpec((tm, D // 2), lambda i, ib, ie, *_: (ib[i], 0)),
        scratch_shapes=[pltpu.VMEM((D, 2 * D_FF), BF16), pltpu.VMEM((D_FF, D), BF16)],
    )
    return pl.pallas_call(
        _moe_kernel,
        grid_spec=grid_spec,
        out_shape=jax.ShapeDtypeStruct((A, D // 2), jnp.uint32),
        compiler_params=_cparams(("arbitrary",), 56),
        name="moe_experts",
    )(item_blk, item_exp, start, count, n_items, xs, w1, b1, w2, b2)


def _combine_kernel(x1_ref, yg_ref, tg_ref, o_ref):
    half = D_MODEL // 2
    acc_lo = x1_ref[:, :half]
    acc_hi = x1_ref[:, half:]
    g = tg_ref[...].T
    for kk in range(TOP_K):
        lo, hi = _unpack_rows(yg_ref[kk])
        acc_lo = acc_lo + g[:, kk:kk + 1] * lo
        acc_hi = acc_hi + g[:, kk:kk + 1] * hi
    o_ref[:, :half] = acc_lo
    o_ref[:, half:] = acc_hi


def _combine(x1, yg, tg, *, tm):
    T, D = x1.shape
    return pl.pallas_call(
        _combine_kernel,
        grid=(T // tm,),
        in_specs=[pl.BlockSpec((tm, D), lambda i: (i, 0)), pl.BlockSpec((TOP_K, tm, D // 2), lambda i: (0, i, 0)),
                  pl.BlockSpec((TOP_K, tm), lambda i: (0, i))],
        out_specs=pl.BlockSpec((tm, D), lambda i: (i, 0)),
        out_shape=jax.ShapeDtypeStruct((T, D), F32),
        compiler_params=_cparams(("parallel",), 48),
        name="moe_combine",
    )(x1, yg, tg)


def _lookup(table, idx):
    n = table.shape[0]
    sel = idx[None] == jnp.arange(n, dtype=jnp.int32).reshape((n,) + (1,) * idx.ndim)
    return jnp.sum(jnp.where(sel, table.reshape((n,) + (1,) * idx.ndim), 0), axis=0)


def _route(top_i, rank, counts, *, tm):
    T = top_i.shape[1]
    A = T * TOP_K
    counts = counts.reshape(N_EXPERTS).astype(jnp.int32)
    start = jnp.cumsum(counts) - counts
    flat_e = top_i.reshape(A // LANES, LANES)
    slot_of = (_lookup(start, flat_e) + rank.reshape(A // LANES, LANES)).reshape(A)
    first_blk = start // tm
    n_it = jnp.where(counts > 0, (start + counts - 1) // tm - first_blk + 1, 0)
    it_end = jnp.cumsum(n_it)
    n_items = it_end[-1]
    n_max = A // tm + N_EXPERTS
    i = jnp.minimum(jnp.arange(n_max, dtype=jnp.int32), n_items - 1)
    item_exp = jnp.sum(it_end[None, :] <= i[:, None], axis=1, dtype=jnp.int32)
    item_blk = _lookup(first_blk, item_exp) + i - _lookup(it_end - n_it, item_exp)
    return slot_of, item_blk.astype(jnp.int32), item_exp, start, counts, n_items.reshape(1)


def _trunk(x, meta_proj, w, tabs, *, tm_prep, tm_tok, tm_moe, attn_blk):
    B, S, D = x.shape
    T = B * S
    hy_m, _, k_m, vt_m, _ = meta_proj
    assert tm_prep == HY_PAD
    front = jnp.concatenate([jnp.zeros((HY_PAD - N_META, COL_HY), BF16), hy_m[0]], axis=0)
    hy_ext, qt, k, vt, sg = _prep(x, front, w['g1'], w['w_in'], w['wqt'], w['wvt'], w['bd'], w['qg'], w['kg'],
                                  tm=tm_prep)
    kt = _hyena_filter(w['fw1'], w['fb1'], w['ffr1'], w['fw2'], w['fb2'], w['ffr2'], w['fw3'], s_real=S)
    zt, a, bt = _hy_pre(hy_ext, w['scw'], w['scb'], w['skip'], s_real=S)
    ya = _hy_post(_hy_conv(kt, zt), a, bt)
    tiles, bmeta = tabs
    yb = _attention(qt, k, vt, k_m[0], vt_m[0], tiles, bmeta, w['lam'], w['subln'], blk=attn_blk)
    x1, h2, top_i, top_g, rank, counts = _merge(
        x.reshape(T, D), ya.reshape(T, HY_WIDTH), yb.reshape(T, ATT_WIDTH), sg.reshape(T, COL_G),
        w['pa'], w['pb'], w['wo'], w['g2'], w['wr_hi'], w['wr_lo'], w['br'], tm=tm_tok)
    slot_of, item_blk, item_exp, start, cnt, n_items = _route(top_i, rank, counts, tm=tm_moe)
    xs = _sc_sorted_rows(h2, slot_of)
    ys = _moe_experts(item_blk, item_exp, start, cnt, n_items, xs, w['w1'], w['b1'], w['w2'], w['b2'], tm=tm_moe)
    yg = _sc_gather_rows(ys, slot_of).reshape(TOP_K, T, D // 2)
    return _combine(x1, yg, top_g, tm=tm_tok).reshape(B, S, D)


def kernel(x_prompt, x_sample, meta_tokens, rel_bias, norm1_g, w_in, short_conv_w, short_conv_b, filt_w1, filt_b1,
           filt_freq1, filt_w2, filt_b2, filt_freq2, filt_w3, hy_skip, proj_a, q_norm_g, k_norm_g, lambda_q1,
           lambda_k1, lambda_q2, lambda_k2, subln_g, proj_b, w_out, norm2_g, w_router, b_router, w_mlp1, b_mlp1,
           w_mlp2, b_mlp2):
    l = 0
    n_hm = COL_Q // ATT_HEAD_DIM
    w_in_bf = w_in[l].astype(BF16)
    c_q, c_v = COL_HY, COL_HY + COL_Q + COL_K
    wr_hi = w_router[l].astype(BF16)
    w = {
        'g1': norm1_g[l][None], 'w_in': w_in_bf,
        'wqt': w_in_bf[:, c_q:c_q + COL_Q].T, 'wvt': w_in_bf[:, c_v:c_v + COL_V].T,
        'bd': jnp.kron(jnp.eye(n_hm, dtype=F32), jnp.full((ATT_HEAD_DIM, ATT_HEAD_DIM), 1.0 / ATT_HEAD_DIM, F32)
                       ).astype(BF16),
        'qg': (jnp.tile(q_norm_g[l], n_hm) * (ATT_HEAD_DIM ** -0.5 * LOG2E))[:, None],
        'kg': jnp.tile(k_norm_g[l], n_hm)[None],
        'wr_hi': wr_hi, 'wr_lo': (w_router[l] - wr_hi.astype(F32)).astype(BF16),
        'fw1': filt_w1[l], 'fb1': filt_b1[l], 'ffr1': filt_freq1[l], 'fw2': filt_w2[l], 'fb2': filt_b2[l],
        'ffr2': filt_freq2[l], 'fw3': filt_w3[l],
        'scw': short_conv_w[l], 'scb': short_conv_b[l], 'skip': hy_skip[l],
        'lam': jnp.stack([lambda_q1[l], lambda_k1[l], lambda_q2[l], lambda_k2[l]]), 'subln': subln_g[l][:, None],
        'pa': proj_a[l].astype(BF16), 'pb': proj_b[l].astype(BF16), 'wo': w_out[l].astype(BF16),
        'g2': norm2_g[l][None], 'br': b_router[l][None],
        'w1': w_mlp1[l], 'b1': b_mlp1[l][:, None, :], 'w2': w_mlp2[l], 'b2': b_mlp2[l][:, None, :],
    }
    meta_proj = _prep(meta_tokens[None], None, w['g1'], w['w_in'], w['wqt'], w['wvt'], w['bd'], w['qg'], w['kg'],
                      tm=N_META)
    tabs = _attn_bias_tables(rel_bias, TILES['attn_blk'])
    y_prompt = _trunk(x_prompt, meta_proj, w, tabs, **TILES)
    y_sample = _trunk(x_sample, meta_proj, w, tabs, **TILES)
    return (y_prompt, y_sample)
```

```python
import functools
import math

import jax
import jax.numpy as jnp
from jax import lax
from jax.experimental import pallas as pl
from jax.experimental.pallas import tpu as pltpu
from jax.experimental.pallas import tpu_sc as plsc

F32 = jnp.float32
BF16 = jnp.bfloat16

D_MODEL = 1024
N_META = 16
RMS_EPS = 1e-6
HY_WIDTH = 512
HY_SHORT = 3
FILT_BANDS = 16
FILT_EMB = 1 + 2 * FILT_BANDS
FILT_HIDDEN = 64
DECAY_TARGET = 1e-2
FAST_DECAY_PCT = 0.3
SLOW_DECAY_PCT = 1.5
ATT_HEADS = 4
ATT_HEAD_DIM = 64
ATT_V_DIM = 128
ATT_WIDTH = 512
N_BUCKETS = 32
MAX_DISTANCE = 128
N_EXPERTS = 32
TOP_K = 4
D_FF = 1024
SWIGLU_LIMIT = 7.0
SWIGLU_ALPHA = 1.702
COL_HY = 3 * HY_WIDTH
COL_Q = 512
COL_K = 512
COL_V = 512
COL_G = 2 * D_MODEL
IN_COLS = COL_HY + COL_Q + COL_K + COL_V + COL_G
LAM_INIT = 0.8 - 0.6 * math.exp(-0.3 * 0)
LOG2E = 1.4426950408889634
ONES_ROWS = 16

V7X_VMEM_BYTES = 64 * 1024 * 1024
LANES = 128
CONV_BLK = 256
HY_PAD = 512
TILES = dict(tm_prep=512, tm_tok=1024, tm_moe=512, attn_blk=512)


def _cparams(sem, vmem_mb):
    limit = vmem_mb * 1024 * 1024
    assert limit < V7X_VMEM_BYTES
    return pltpu.CompilerParams(dimension_semantics=sem, vmem_limit_bytes=limit)


def _const_spec(shape):
    nd = len(shape)
    return pl.BlockSpec(shape, lambda *_: (0,) * nd, pipeline_mode=pl.Buffered(1))


def _pack_rows(x):
    n = x.shape[1] // 2
    lo = pltpu.bitcast(x[:, :n].astype(BF16).astype(F32), jnp.uint32)
    hi = pltpu.bitcast(x[:, n:].astype(BF16).astype(F32), jnp.uint32)
    return (hi & jnp.uint32(0xFFFF0000)) | (lo >> 16)


def _unpack_rows(w):
    lo = pltpu.bitcast(w << 16, F32)
    hi = pltpu.bitcast(w & jnp.uint32(0xFFFF0000), F32)
    return lo, hi


SC_ROWS = 64


def _sc_workers():
    info = plsc.get_sparse_core_info()
    return info, info.num_cores * info.num_subcores


def _sc_stream_rows(table_hbm, out_hbm, rows_v, sems, base, n_chunks, load_index, index_ref):
    def rows_copy(j, buf):
        return pltpu.make_async_copy(table_hbm.at[index_ref(j, buf)], rows_v.at[buf], sems.at[buf])

    def fetch(j, buf):
        load_index(j, buf)
        rows_copy(j, buf).start()

    def drain(j, buf):
        rows_copy(j, buf).wait()
        pltpu.sync_copy(rows_v.at[buf], out_hbm.at[pl.ds(pl.multiple_of(base + j * SC_ROWS, SC_ROWS), SC_ROWS)])

    fetch(0, 0)

    @pl.loop(0, n_chunks // 2)
    def _(jj):
        j = 2 * jj
        fetch(j + 1, 1)
        drain(j, 0)

        @pl.when(j + 2 < n_chunks)
        def _():
            fetch(j + 2, 0)

        drain(j + 1, 1)


def _sc_sorted_rows(table, slot_of, *, scan=16384):
    info, n_workers = _sc_workers()
    lanes = info.num_lanes
    n_tok, width = table.shape
    n_picks = slot_of.shape[0]
    per_worker = n_picks // n_workers
    n_chunks = per_worker // SC_ROWS
    assert n_tok % scan == 0 and n_picks % scan == 0 and scan % lanes == 0
    assert n_chunks * SC_ROWS * n_workers == n_picks and n_chunks % 2 == 0
    mesh = plsc.VectorSubcoreMesh(core_axis_name="c", subcore_axis_name="s")

    @functools.partial(
        pl.kernel, mesh=mesh, out_type=jax.ShapeDtypeStruct((n_picks, width), table.dtype),
        scratch_types=[pltpu.VMEM((scan,), jnp.int32), pltpu.VMEM((per_worker,), jnp.int32),
                       pltpu.VMEM((2, SC_ROWS, width), table.dtype), pltpu.SemaphoreType.DMA((2,))],
        compiler_params=pltpu.CompilerParams(needs_layout_passes=False))
    def sorted_rows(table_hbm, slot_hbm, out_hbm, picks_v, tok_v, rows_v, sems):
        base = (lax.axis_index("s") * info.num_cores + lax.axis_index("c")) * per_worker
        lane = lax.iota(jnp.int32, lanes)

        @pl.loop(0, per_worker // lanes)
        def _(i):
            tok_v[pl.ds(i * lanes, lanes)] = jnp.zeros((lanes,), jnp.int32)

        @pl.loop(0, n_picks // scan)
        def _(ci):
            a0 = pl.multiple_of(ci * scan, scan)
            pltpu.sync_copy(slot_hbm.at[pl.ds(a0, scan)], picks_v)
            t0 = lax.rem(a0, n_tok)

            @pl.loop(0, scan // lanes)
            def _(i):
                row = picks_v[pl.ds(i * lanes, lanes)] - base
                mine = jnp.logical_and(row >= 0, row < per_worker)
                plsc.store_scatter(tok_v, [jnp.where(mine, row, 0)], t0 + i * lanes + lane, mask=mine)

        _sc_stream_rows(table_hbm, out_hbm, rows_v, sems, base, n_chunks,
                        load_index=lambda j, buf: None,
                        index_ref=lambda j, buf: tok_v.at[pl.ds(pl.multiple_of(j * SC_ROWS, SC_ROWS), SC_ROWS)])

    return sorted_rows(table, slot_of)


def _sc_gather_rows(table, idx):
    info, n_workers = _sc_workers()
    n_rows, width = idx.shape[0], table.shape[1]
    per_worker = n_rows // n_workers
    n_chunks = per_worker // SC_ROWS
    assert n_chunks * SC_ROWS * n_workers == n_rows and n_chunks % 2 == 0
    mesh = plsc.VectorSubcoreMesh(core_axis_name="c", subcore_axis_name="s")

    @functools.partial(
        pl.kernel, mesh=mesh, out_type=jax.ShapeDtypeStruct((n_rows, width), table.dtype),
        scratch_types=[pltpu.VMEM((2, SC_ROWS), jnp.int32), pltpu.VMEM((2, SC_ROWS, width), table.dtype),
                       pltpu.SemaphoreType.DMA((2,))])
    def gather(table_hbm, idx_hbm, out_hbm, idx_v, rows_v, sems):
        base = (lax.axis_index("s") * info.num_cores + lax.axis_index("c")) * per_worker

        def load_index(j, buf):
            off = pl.multiple_of(base + j * SC_ROWS, SC_ROWS)
            pltpu.sync_copy(idx_hbm.at[pl.ds(off, SC_ROWS)], idx_v.at[buf])

        _sc_stream_rows(table_hbm, out_hbm, rows_v, sems, base, n_chunks,
                        load_index=load_index, index_ref=lambda j, buf: idx_v.at[buf])

    return gather(table, idx)


def _prep_kernel(x_ref, front_ref, g1_ref, w_ref, wqt_ref, wvt_ref, bd_ref, qg_ref, kg_ref,
                 hy_ref, qt_ref, k_ref, vt_ref, sg_ref, *, lead):
    if lead:
        @pl.when(pl.program_id(1) == 0)
        def _():
            hy_ref[...] = front_ref[...]

        @pl.when(pl.program_id(1) > 0)
        def _():
            _prep_tile(x_ref, g1_ref, w_ref, wqt_ref, wvt_ref, bd_ref, qg_ref, kg_ref,
                       hy_ref, qt_ref, k_ref, vt_ref, sg_ref)
    else:
        _prep_tile(x_ref, g1_ref, w_ref, wqt_ref, wvt_ref, bd_ref, qg_ref, kg_ref,
                   hy_ref, qt_ref, k_ref, vt_ref, sg_ref)


def _prep_tile(x_ref, g1_ref, w_ref, wqt_ref, wvt_ref, bd_ref, qg_ref, kg_ref,
               hy_ref, qt_ref, k_ref, vt_ref, sg_ref):
    x = x_ref[...]
    ms = jnp.mean(x * x, axis=-1, keepdims=True)
    h = (x * lax.rsqrt(ms + RMS_EPS) * g1_ref[...]).astype(BF16)
    nt_dims = (((1,), (1,)), ((), ()))

    def proj(lo, hi):
        return jnp.dot(h, w_ref[:, lo:hi], preferred_element_type=F32)

    def proj_t(wt_ref):
        return lax.dot_general(wt_ref[...], h, nt_dims, preferred_element_type=F32)

    c0, c1, c2, c3 = COL_HY, COL_HY + COL_Q, COL_HY + COL_Q + COL_K, COL_HY + COL_Q + COL_K + COL_V
    hy_ref[...] = proj(0, c0).astype(BF16)
    qt = proj_t(wqt_ref)
    q3 = qt.reshape(COL_Q // ATT_HEAD_DIM, ATT_HEAD_DIM, qt.shape[1])
    q3 = q3 * lax.rsqrt(jnp.mean(q3 * q3, axis=1, keepdims=True) + RMS_EPS)
    qt_ref[...] = (q3.reshape(qt.shape) * qg_ref[...]).astype(BF16)
    kk = proj(c1, c2)
    msk = jnp.dot((kk * kk).astype(BF16), bd_ref[...], preferred_element_type=F32)
    k_ref[...] = (kk * lax.rsqrt(msk + RMS_EPS) * kg_ref[...]).astype(BF16)
    vt_ref[...] = proj_t(wvt_ref).astype(BF16)
    sg_ref[...] = jax.nn.sigmoid(proj(c3, IN_COLS)).astype(BF16)


def _prep(x, front, g1, w_in_bf, wqt, wvt, bd, qg_col, kg, *, tm):
    B, S, D = x.shape
    nt = S // tm
    lead = 0 if front is None else 1
    if front is None:
        front = jnp.zeros((tm, COL_HY), BF16)
    src = lambda i: jnp.maximum(i - lead, 0)
    tok = lambda w: pl.BlockSpec((None, tm, w), lambda b, i: (b, src(i), 0))
    tok_t = lambda w: pl.BlockSpec((None, w, tm), lambda b, i: (b, 0, src(i)))
    return pl.pallas_call(
        functools.partial(_prep_kernel, lead=lead),
        grid=(B, nt + lead),
        in_specs=[tok(D), _const_spec((tm, COL_HY)), _const_spec((1, D)), _const_spec((D, IN_COLS)),
                  _const_spec((COL_Q, D)), _const_spec((COL_V, D)), _const_spec((COL_Q, COL_Q)),
                  _const_spec((COL_Q, 1)), _const_spec((1, COL_K))],
        out_specs=[pl.BlockSpec((None, tm, COL_HY), lambda b, i: (b, i, 0)),
                   tok_t(COL_Q), tok(COL_K), tok_t(COL_V), tok(COL_G)],
        out_shape=[jax.ShapeDtypeStruct((B, S + lead * tm, COL_HY), BF16),
                   jax.ShapeDtypeStruct((B, COL_Q, S), BF16), jax.ShapeDtypeStruct((B, S, COL_K), BF16),
                   jax.ShapeDtypeStruct((B, COL_V, S), BF16), jax.ShapeDtypeStruct((B, S, COL_G), BF16)],
        compiler_params=_cparams(("parallel", "arbitrary"), 48),
        name="prep",
    )(x, front, g1, w_in_bf, wqt, wvt, bd, qg_col, kg)


def _filt_kernel(w1t_ref, w1c_ref, w1s_ref, b1_ref, fr1_ref, w2t_ref, b2_ref, fr2_ref, w3f_ref, w3b_ref,
                 bands_ref, deltas_ref, kt_ref, hid_ref, *, n_seq, s_real):
    lk = kt_ref.shape[1]
    hi = lax.Precision.HIGHEST
    u = lax.broadcasted_iota(jnp.int32, (1, lk), 1)
    d = u - s_real
    pos = jnp.abs(d).astype(F32)
    t = pos / float(max(n_seq - 1, 1))

    @pl.when(pl.program_id(0) == 0)
    def _():
        w = 2.0 * math.pi * pos / float(n_seq)
        ang = bands_ref[...] * w
        pre = (w1t_ref[...] * t
               + jnp.dot(w1c_ref[...], jnp.cos(ang), precision=hi, preferred_element_type=F32)
               - jnp.dot(w1s_ref[...], jnp.sin(ang), precision=hi, preferred_element_type=F32)
               + b1_ref[...])
        h1 = jnp.sin(fr1_ref[...] * pre)
        h2 = jnp.sin(fr2_ref[...] * (jnp.dot(w2t_ref[...], h1, precision=hi, preferred_element_type=F32)
                                     + b2_ref[...]))
        hid_ref[...] = h2

    h2 = hid_ref[...]
    hf = jnp.dot(w3f_ref[...], h2, precision=hi, preferred_element_type=F32)
    hb = jnp.dot(w3b_ref[...], h2, precision=hi, preferred_element_type=F32)
    decay = jnp.exp(-t * deltas_ref[...])
    valid = pos <= float(n_seq - 1)
    kun = jnp.where(valid, jnp.where(d >= 0, hf, hb) * decay, 0.0)
    nrm = lax.rsqrt(jnp.sum(kun * kun, axis=1, keepdims=True) + RMS_EPS)
    kt_ref[...] = kun * nrm


def _hyena_filter(fw1, fb1, ffr1, fw2, fb2, ffr2, fw3, *, s_real):
    n_seq = s_real + N_META
    lk = 2 * s_real + CONV_BLK
    cf = 64
    col = lambda a: a.reshape(-1, 1)
    w1t = fw1.T
    bands = jnp.linspace(1e-4, FILT_BANDS - 1, FILT_BANDS, dtype=F32).reshape(-1, 1)
    max_decay = math.log(DECAY_TARGET) / FAST_DECAY_PCT
    min_decay = math.log(DECAY_TARGET) / SLOW_DECAY_PCT
    deltas = jnp.abs(jnp.linspace(min_decay, max_decay, HY_WIDTH, dtype=F32)).reshape(-1, 1)
    w3t = fw3.T
    h = FILT_HIDDEN
    return pl.pallas_call(
        functools.partial(_filt_kernel, n_seq=n_seq, s_real=s_real),
        grid=(HY_WIDTH // cf,),
        in_specs=[_const_spec((h, 1)), _const_spec((h, FILT_BANDS)), _const_spec((h, FILT_BANDS)),
                  _const_spec((h, 1)), _const_spec((h, 1)), _const_spec((h, h)), _const_spec((h, 1)),
                  _const_spec((h, 1)),
                  pl.BlockSpec((cf, h), lambda i: (i, 0)), pl.BlockSpec((cf, h), lambda i: (i, 0)),
                  _const_spec((FILT_BANDS, 1)), pl.BlockSpec((cf, 1), lambda i: (i, 0))],
        out_specs=pl.BlockSpec((cf, lk), lambda i: (i, 0)),
        out_shape=jax.ShapeDtypeStruct((HY_WIDTH, lk), F32),
        scratch_shapes=[pltpu.VMEM((h, lk), F32)],
        compiler_params=_cparams(("arbitrary",), 48),
        name="hyena_filter",
    )(w1t[:, 0:1], w1t[:, 1:1 + FILT_BANDS], w1t[:, 1 + FILT_BANDS:], col(fb1), col(ffr1), fw2.T, col(fb2),
      col(ffr2), w3t[:HY_WIDTH], w3t[HY_WIDTH:], bands, deltas)


def _hy_pre_kernel(x0_ref, x1_ref, v_ref, p0_ref, p1_ref, pv_ref, n0_ref, n1_ref, nv_ref,
                   scw_ref, scb_ref, skip_ref, zt_ref, a_ref, bt_ref, zs_ref):
    jt = pl.program_id(1)
    last = pl.num_programs(1) - 1
    nb_batch = x0_ref.shape[0]
    row = lax.broadcasted_iota(jnp.int32, (CONV_BLK, LANES), 0)
    keep_next = (jt < last).astype(F32)
    z_keep = jnp.logical_or(jt > 0, row >= CONV_BLK - N_META)

    def short_conv(cur_ref, prev_ref, next_ref, comp, b):
        cur = cur_ref[b].astype(F32)
        prev = prev_ref[b, 15:16, :].astype(F32)
        nxt = next_ref[b, 0:1, :].astype(F32) * keep_next
        up = jnp.where(row == 0, prev, pltpu.roll(cur, 1, 0))
        dn = jnp.where(row == CONV_BLK - 1, nxt, pltpu.roll(cur, CONV_BLK - 1, 0))
        w = scw_ref[comp]
        return up * w[0:1] + cur * w[1:2] + dn * w[2:3] + scb_ref[comp]

    def body(b, carry):
        x0c = short_conv(x0_ref, p0_ref, n0_ref, 0, b)
        x1c = short_conv(x1_ref, p1_ref, n1_ref, 1, b)
        vc = short_conv(v_ref, pv_ref, nv_ref, 2, b)
        z = jnp.where(z_keep, x1c * vc, 0.0)
        zs_ref[b] = z.T
        a_ref[b] = x0c.astype(BF16)
        bt_ref[b] = (x0c * z * skip_ref[...]).astype(BF16)
        return carry

    lax.fori_loop(0, nb_batch, body, 0)
    zt_ref[...] = pltpu.einshape("bcl->cbl", zs_ref[...]).astype(BF16)


def _hy_pre(hy_ext, scw, scb, skip, *, s_real):
    B = hy_ext.shape[0]
    nb = s_real // CONV_BLK
    ncb = HY_WIDTH // LANES
    sub = 16
    r16 = CONV_BLK // sub
    last16 = (s_real + HY_PAD) // sub - 1
    cur = lambda comp: pl.BlockSpec((B, CONV_BLK, LANES), lambda cb, jt: (0, jt + 1, comp * ncb + cb))
    prv = lambda comp: pl.BlockSpec((B, sub, LANES), lambda cb, jt: (0, r16 * (jt + 1) - 1, comp * ncb + cb))
    nxt = lambda comp: pl.BlockSpec(
        (B, sub, LANES), lambda cb, jt: (0, jnp.minimum(r16 * (jt + 2), last16), comp * ncb + cb))
    nat = pl.BlockSpec((B, CONV_BLK, LANES), lambda cb, jt: (0, jnp.maximum(jt - 1, 0), cb))
    scw_r = scw.reshape(HY_SHORT, 3, ncb, LANES).transpose(2, 1, 0, 3)
    scb_r = scb.reshape(3, ncb, 1, LANES).transpose(1, 0, 2, 3)
    skip_r = skip.reshape(ncb, 1, LANES)
    return pl.pallas_call(
        _hy_pre_kernel,
        grid=(ncb, nb + 1),
        in_specs=[cur(0), cur(1), cur(2), prv(0), prv(1), prv(2), nxt(0), nxt(1), nxt(2),
                  pl.BlockSpec((None, 3, HY_SHORT, LANES), lambda cb, jt: (cb, 0, 0, 0)),
                  pl.BlockSpec((None, 3, 1, LANES), lambda cb, jt: (cb, 0, 0, 0)),
                  pl.BlockSpec((None, 1, LANES), lambda cb, jt: (cb, 0, 0))],
        out_specs=[pl.BlockSpec((LANES, None, B, CONV_BLK), lambda cb, jt: (cb, jt, 0, 0)), nat, nat],
        out_shape=[jax.ShapeDtypeStruct((HY_WIDTH, nb + 1, B, CONV_BLK), BF16),
                   jax.ShapeDtypeStruct((B, s_real, HY_WIDTH), BF16),
                   jax.ShapeDtypeStruct((B, s_real, HY_WIDTH), BF16)],
        scratch_shapes=[pltpu.VMEM((B, LANES, CONV_BLK), F32)],
        compiler_params=_cparams(("parallel", "arbitrary"), 48),
        name="hyena_pre",
    )(hy_ext, hy_ext, hy_ext, hy_ext, hy_ext, hy_ext, hy_ext, hy_ext, hy_ext, scw_r, scb_r, skip_r)


def _hy_conv_kernel(kt_ref, z_ref, y_ref, g_ref, *, nb):
    cbk = z_ref.shape[0]
    B = z_ref.shape[2]
    s_real = nb * CONV_BLK
    nq = 2 * s_real // LANES + 1
    upper = (lax.broadcasted_iota(jnp.int32, (LANES, LANES), 1)
             >= lax.broadcasted_iota(jnp.int32, (LANES, LANES), 0))

    def channel(c, carry):
        kt_row = kt_ref[pl.ds(c, 1), :]
        for q in range(nq):
            off = 2 * s_real - LANES * q
            skew = lambda lo: pltpu.roll(jnp.broadcast_to(kt_row[:, lo:lo + LANES], (LANES, LANES)),
                                         0, 1, stride=1, stride_axis=0)
            blk_g = jnp.where(upper, skew(off + LANES), skew(off))
            g_ref[LANES * q:LANES * (q + 1), :] = blk_g.astype(BF16)
        y_ref[c] = jnp.zeros(y_ref.shape[1:], F32)
        for delta in range(-(nb - 1), nb + 1):
            base = CONV_BLK * (nb - delta)
            tile = jnp.concatenate([g_ref[base + LANES:base + LANES + CONV_BLK, :],
                                    g_ref[base:base + CONV_BLK, :]], axis=1)
            jb_lo = max(0, 1 - delta)
            n = min(nb, nb - delta) - jb_lo + 1
            o_lo = jb_lo + delta - 1
            lhs = z_ref[c, jb_lo:jb_lo + n].reshape(n * B, CONV_BLK)
            y_ref[c, o_lo:o_lo + n] += jnp.dot(lhs, tile, preferred_element_type=F32).reshape(n, B, CONV_BLK)
        return carry

    lax.fori_loop(0, cbk, channel, 0)


def _hy_conv(kt, zt, *, cbk=8):
    C, nb1, B, _ = zt.shape
    nb = nb1 - 1
    lk = kt.shape[1]
    return pl.pallas_call(
        functools.partial(_hy_conv_kernel, nb=nb),
        grid=(C // cbk,),
        in_specs=[pl.BlockSpec((cbk, lk), lambda i: (i, 0)),
                  pl.BlockSpec((cbk, nb1, B, CONV_BLK), lambda i: (i, 0, 0, 0))],
        out_specs=pl.BlockSpec((cbk, nb, B, CONV_BLK), lambda i: (i, 0, 0, 0)),
        out_shape=jax.ShapeDtypeStruct((C, nb, B, CONV_BLK), F32),
        scratch_shapes=[pltpu.VMEM((2 * nb * CONV_BLK + LANES, LANES), BF16)],
        compiler_params=_cparams(("parallel",), 48),
        name="hyena_conv",
    )(kt, zt)


def _hy_post_kernel(yt_ref, a_ref, bt_ref, o_ref, ys_ref):
    ys_ref[...] = pltpu.einshape("cbl->bcl", yt_ref[...])

    def body(b, carry):
        y = ys_ref[b].T
        o_ref[b] = (a_ref[b].astype(F32) * y + bt_ref[b].astype(F32)).astype(BF16)
        return carry

    lax.fori_loop(0, a_ref.shape[0], body, 0)


def _hy_post(yt, a, bt):
    C, nb, B, _ = yt.shape
    nat = pl.BlockSpec((B, CONV_BLK, LANES), lambda cb, ib: (0, ib, cb))
    return pl.pallas_call(
        _hy_post_kernel,
        grid=(C // LANES, nb),
        in_specs=[pl.BlockSpec((LANES, None, B, CONV_BLK), lambda cb, ib: (cb, ib, 0, 0)), nat, nat],
        out_specs=nat,
        out_shape=jax.ShapeDtypeStruct(a.shape, BF16),
        scratch_shapes=[pltpu.VMEM((B, LANES, CONV_BLK), F32)],
        compiler_params=_cparams(("parallel", "parallel"), 48),
        name="hyena_post",
    )(yt, a, bt)


def _attn_kernel(qt_ref, k_ref, vt_ref, km_ref, vtm_ref, bias_ref, bmeta_ref, lam_ref, sg_ref,
                 o_ref, m_ref, acc_ref, s_ref, bm_ref, *, blk):
    i = pl.program_id(2)
    nk = k_ref.shape[0] // blk
    assert nk % 2 == 0
    qt = qt_ref[...]
    rowi = lax.broadcasted_iota(jnp.int32, qt.shape, 0)
    zero = jnp.zeros_like(qt)
    qmaps = (jnp.where(rowi < ATT_HEAD_DIM, qt, zero), jnp.where(rowi >= ATT_HEAD_DIM, qt, zero))

    def with_ones(vt):
        return jnp.concatenate([vt, jnp.ones((ONES_ROWS, vt.shape[1]), BF16)], axis=0)

    kmeta, vtmeta, bias_m = km_ref[...], with_ones(vtm_ref[...]), bmeta_ref[jnp.minimum(i, 1)]
    for c in range(2):
        s = jnp.dot(kmeta, qmaps[c], preferred_element_type=F32) + bias_m
        m0 = jnp.max(s, axis=0, keepdims=True)
        m_ref[c] = m0
        acc_ref[c] = jnp.dot(vtmeta, jnp.exp2(s - m0).astype(BF16), preferred_element_type=F32)

    def scores(j, slot):
        kblk = k_ref[pl.ds(pl.multiple_of(j * blk, blk), blk), :]
        bias = bias_ref[jnp.clip(j - i, -2, 2) + 2]
        for c in range(2):
            s = jnp.dot(kblk, qmaps[c], preferred_element_type=F32) + bias
            s_ref[slot, c] = s
            bm_ref[slot, c] = jnp.max(s, axis=0, keepdims=True)

    def consume(j, slot):
        vtblk = with_ones(vt_ref[:, pl.ds(pl.multiple_of(j * blk, blk), blk)])
        for c in range(2):
            m_old = m_ref[c]
            m_new = jnp.maximum(m_old, bm_ref[slot, c])
            p = jnp.exp2(s_ref[slot, c] - m_new)
            acc_ref[c] = (jnp.exp2(m_old - m_new) * acc_ref[c]
                          + jnp.dot(vtblk, p.astype(BF16), preferred_element_type=F32))
            m_ref[c] = m_new

    scores(0, 0)

    def pair(jj, carry):
        j = 2 * jj
        scores(j + 1, 1)
        consume(j, 0)
        scores(j + 2, 0)
        consume(j + 1, 1)
        return carry

    lax.fori_loop(0, nk // 2 - 1, pair, 0)
    scores(nk - 1, 1)
    consume(nk - 2, 0)
    consume(nk - 1, 1)

    lq = lam_ref[...]
    lam = (jnp.exp(jnp.sum(lq[0:1] * lq[1:2], axis=1, keepdims=True))
           - jnp.exp(jnp.sum(lq[2:3] * lq[3:4], axis=1, keepdims=True)) + LAM_INIT)
    v_rows = slice(0, ATT_V_DIM)
    l_row = slice(ATT_V_DIM, ATT_V_DIM + 1)
    ot = acc_ref[0, v_rows] / acc_ref[0, l_row] - lam * (acc_ref[1, v_rows] / acc_ref[1, l_row])
    ot = ot * lax.rsqrt(jnp.mean(ot * ot, axis=0, keepdims=True) + RMS_EPS) * sg_ref[...] * (1.0 - LAM_INIT)
    o_ref[...] = ot.T.astype(BF16)


def _t5_bucket(rel):
    nb = N_BUCKETS // 2
    max_exact = nb // 2
    ret = jnp.where(rel > 0, nb, 0)
    n = jnp.abs(rel)
    nf = jnp.maximum(n, 1).astype(F32)
    large = max_exact + (jnp.log(nf / max_exact) / math.log(MAX_DISTANCE / max_exact)
                         * (nb - max_exact)).astype(jnp.int32)
    large = jnp.minimum(large, nb - 1)
    return ret + jnp.where(n < max_exact, n, large)


def _attn_bias_tables(rel_bias, blk):
    assert blk >= MAX_DISTANCE
    table = rel_bias.astype(F32) * LOG2E

    def bias_of(rel):
        bucket = _t5_bucket(rel)
        sel = bucket[None] == jnp.arange(N_BUCKETS, dtype=jnp.int32).reshape((N_BUCKETS,) + (1,) * rel.ndim)
        return jnp.stack([jnp.sum(jnp.where(sel, table[:, h].reshape((N_BUCKETS,) + (1,) * rel.ndim), 0.0), axis=0)
                          for h in range(ATT_HEADS)])

    r = jnp.arange(blk, dtype=jnp.int32)
    rel = jnp.stack([dj * blk + r[:, None] - r[None, :] for dj in (-2, -1, 0, 1, 2)])
    tiles = bias_of(rel)
    m = jnp.arange(N_META, dtype=jnp.int32)
    rel_m0 = m[:, None] - (N_META + r[None, :])
    rel_m1 = rel_m0 - blk
    bmeta = bias_of(jnp.stack([rel_m0, rel_m1]))
    return tiles, bmeta


def _attention(qt, k, vt, kmeta, vtmeta, tiles, bmeta, lam, subln_col, *, blk):
    B, S, _ = k.shape
    return pl.pallas_call(
        functools.partial(_attn_kernel, blk=blk),
        grid=(ATT_HEADS, B, S // blk),
        in_specs=[pl.BlockSpec((None, ATT_V_DIM, blk), lambda h, b, i: (b, h, i)),
                  pl.BlockSpec((None, S, ATT_V_DIM), lambda h, b, i: (b, 0, h)),
                  pl.BlockSpec((None, ATT_V_DIM, S), lambda h, b, i: (b, h, 0)),
                  pl.BlockSpec((N_META, ATT_V_DIM), lambda h, b, i: (0, h)),
                  pl.BlockSpec((ATT_V_DIM, N_META), lambda h, b, i: (h, 0)),
                  pl.BlockSpec((None, 5, blk, blk), lambda h, b, i: (h, 0, 0, 0)),
                  pl.BlockSpec((None, 2, N_META, blk), lambda h, b, i: (h, 0, 0, 0)),
                  pl.BlockSpec((4, ATT_HEAD_DIM), lambda h, b, i: (0, 0)),
                  pl.BlockSpec((ATT_V_DIM, 1), lambda h, b, i: (0, 0))],
        out_specs=pl.BlockSpec((None, blk, ATT_V_DIM), lambda h, b, i: (b, i, h)),
        scratch_shapes=[pltpu.VMEM((2, 1, blk), F32), pltpu.VMEM((2, ATT_V_DIM + ONES_ROWS, blk), F32),
                        pltpu.VMEM((2, 2, blk, blk), F32), pltpu.VMEM((2, 2, 1, blk), F32)],
        out_shape=jax.ShapeDtypeStruct((B, S, ATT_WIDTH), BF16),
        compiler_params=_cparams(("parallel", "parallel", "arbitrary"), 56),
        name="diff_attention",
    )(qt, k, vt, kmeta, vtmeta, tiles, bmeta, lam, subln_col)


def _merge_kernel(x_ref, ya_ref, yb_ref, sg_ref, pa_ref, pb_ref, wo_ref, g2_ref, wr2_ref, br_ref,
                  tri_ref, x1_ref, h2_ref, ti_ref, tg_ref, rk_ref, cnt_ref, lg_ref):
    ma = jnp.dot(ya_ref[...], pa_ref[...], preferred_element_type=F32)
    mb = jnp.dot(yb_ref[...], pb_ref[...], preferred_element_type=F32)
    m = sg_ref[:, :D_MODEL].astype(F32) * ma + sg_ref[:, D_MODEL:].astype(F32) * mb
    x1 = x_ref[...] + jnp.dot(m.astype(BF16), wo_ref[...], preferred_element_type=F32)
    x1_ref[...] = x1
    h2 = x1 * lax.rsqrt(jnp.mean(x1 * x1, axis=-1, keepdims=True) + RMS_EPS) * g2_ref[...]
    h2_hi = h2.astype(BF16)
    h2_ref[...] = _pack_rows(h2)
    h2_lo = (h2 - h2_hi.astype(F32)).astype(BF16)
    wr2 = wr2_ref[...]
    lg_ref[0] = jnp.dot(h2_hi, wr2, preferred_element_type=F32)
    lg_ref[1] = jnp.dot(h2_lo, wr2, preferred_element_type=F32)
    hi_t = lg_ref[0].T
    lo_t = lg_ref[1].T
    logits = hi_t[:N_EXPERTS] + hi_t[N_EXPERTS:] + lo_t[:N_EXPERTS] + br_ref[...]
    row = lax.broadcasted_iota(jnp.int32, logits.shape, 0)
    slot = lax.broadcasted_iota(jnp.int32, ti_ref.shape, 0)
    top_i = jnp.zeros(ti_ref.shape, jnp.int32)
    top_v = jnp.zeros(tg_ref.shape, F32)
    work = logits
    picks = []
    for kk in range(TOP_K):
        mx = jnp.max(work, axis=0, keepdims=True)
        idx = jnp.min(jnp.where(work == mx, row, N_EXPERTS), axis=0, keepdims=True)
        pick = row == idx
        picks.append(pick)
        top_i = jnp.where(slot == kk, idx, top_i)
        top_v = jnp.where(slot == kk, mx, top_v)
        work = jnp.where(pick, -jnp.inf, work)
    ex = jnp.exp(top_v - jnp.max(top_v, axis=0, keepdims=True))
    ti_ref[...] = top_i
    tg_ref[...] = ex / jnp.sum(ex, axis=0, keepdims=True)

    @pl.when(pl.program_id(0) == 0)
    def _():
        cnt_ref[...] = jnp.zeros_like(cnt_ref)

    chosen = jnp.where(work == -jnp.inf, 1.0, 0.0)
    before = cnt_ref[...] + jnp.dot(chosen.astype(BF16), tri_ref[...], preferred_element_type=F32)
    rank = jnp.zeros(rk_ref.shape, F32)
    for kk in range(TOP_K):
        rk = jnp.sum(jnp.where(picks[kk], before, 0.0), axis=0, keepdims=True)
        rank = jnp.where(slot == kk, rk, rank)
    rk_ref[...] = rank.astype(jnp.int32)
    cnt_ref[...] += jnp.sum(chosen, axis=1, keepdims=True)


def _merge(x, ya, yb, sg, pa, pb, wo, g2, wr_hi, wr_lo, br, *, tm):
    T, D = x.shape
    tok = lambda w: pl.BlockSpec((tm, w), lambda i: (i, 0))
    tok_t = pl.BlockSpec((TOP_K, tm), lambda i: (0, i))
    tri = jnp.triu(jnp.ones((tm, tm), F32), 1).astype(BF16)
    wr2 = jnp.concatenate([wr_hi, wr_lo], axis=1)
    return pl.pallas_call(
        _merge_kernel,
        grid=(T // tm,),
        in_specs=[tok(D), tok(HY_WIDTH), tok(ATT_WIDTH), tok(COL_G), _const_spec((HY_WIDTH, D)),
                  _const_spec((ATT_WIDTH, D)), _const_spec((D, D)), _const_spec((1, D)),
                  _const_spec((D, 2 * N_EXPERTS)), _const_spec((N_EXPERTS, 1)), _const_spec((tm, tm))],
        out_specs=[tok(D), tok(D // 2), tok_t, tok_t, tok_t, pl.BlockSpec((N_EXPERTS, 1), lambda i: (0, 0))],
        out_shape=[jax.ShapeDtypeStruct((T, D), F32), jax.ShapeDtypeStruct((T, D // 2), jnp.uint32),
                   jax.ShapeDtypeStruct((TOP_K, T), jnp.int32), jax.ShapeDtypeStruct((TOP_K, T), F32),
                   jax.ShapeDtypeStruct((TOP_K, T), jnp.int32), jax.ShapeDtypeStruct((N_EXPERTS, 1), F32)],
        scratch_shapes=[pltpu.VMEM((2, tm, 2 * N_EXPERTS), F32)],
        compiler_params=_cparams(("arbitrary",), 48),
        name="merge_router",
    )(x, ya, yb, sg, pa, pb, wo, g2, wr2, br.reshape(N_EXPERTS, 1), tri)


def _moe_kernel(iblk_ref, iexp_ref, start_ref, count_ref, nitem_ref, xs_ref, w1_ref, b1_ref, w2_ref, b2_ref, ys_ref,
                w1b_ref, w2b_ref):
    i = pl.program_id(0)
    tm = xs_ref.shape[0]

    @pl.when(jnp.logical_and(i < nitem_ref[0],
                             jnp.logical_or(i == 0, iexp_ref[jnp.maximum(i - 1, 0)] != iexp_ref[i])))
    def _():
        rows = 256
        for r in range(0, D_MODEL, rows):
            w1b_ref[r:r + rows, :] = w1_ref[r:r + rows, :].astype(BF16)
        for r in range(0, D_FF, rows):
            w2b_ref[r:r + rows, :] = w2_ref[r:r + rows, :].astype(BF16)

    @pl.when(i < nitem_ref[0])
    def _():
        blk = iblk_ref[i]
        e = iexp_ref[i]
        x = jnp.concatenate(_unpack_rows(xs_ref[...]), axis=1).astype(BF16)
        u = jnp.dot(x, w1b_ref[...], preferred_element_type=F32) + b1_ref[...]
        glu = jnp.minimum(u[:, :D_FF], SWIGLU_LIMIT)
        lin = jnp.clip(u[:, D_FF:], -SWIGLU_LIMIT, SWIGLU_LIMIT)
        a = glu * jax.nn.sigmoid(SWIGLU_ALPHA * glu) * (lin + 1.0)
        y = _pack_rows(jnp.dot(a.astype(BF16), w2b_ref[...], preferred_element_type=F32) + b2_ref[...])
        row = lax.broadcasted_iota(jnp.int32, (tm, 1), 0) + blk * tm
        mine = jnp.logical_and(row >= start_ref[e], row < start_ref[e] + count_ref[e])
        first_visit = jnp.logical_or(i == 0, iblk_ref[jnp.maximum(i - 1, 0)] != blk)

        @pl.when(first_visit)
        def _():
            ys_ref[...] = jnp.where(mine, y, jnp.zeros_like(y))

        @pl.when(jnp.logical_not(first_visit))
        def _():
            ys_ref[...] = jnp.where(mine, y, ys_ref[...])


def _moe_experts(item_blk, item_exp, start, count, n_items, xs, w1, b1, w2, b2, *, tm):
    A, D = xs.shape[0], xs.shape[1] * 2
    grid_spec = pltpu.PrefetchScalarGridSpec(
        num_scalar_prefetch=5,
        grid=(item_blk.shape[0],),
        in_specs=[pl.BlockSpec((tm, D // 2), lambda i, ib, ie, *_: (ib[i], 0)),
                  pl.BlockSpec((None, D, 2 * D_FF), lambda i, ib, ie, *_: (ie[i], 0, 0)),
                  pl.BlockSpec((None, 1, 2 * D_FF), lambda i, ib, ie, *_: (ie[i], 0, 0)),
                  pl.BlockSpec((None, D_FF, D), lambda i, ib, ie, *_: (ie[i], 0, 0)),
                  pl.BlockSpec((None, 1, D), lambda i, ib, ie, *_: (ie[i], 0, 0))],
        out_specs=pl.BlockSpec((tm, D // 2), lambda i, ib, ie, *_: (ib[i], 0)),
        scratch_shapes=[pltpu.VMEM((D, 2 * D_FF), BF16), pltpu.VMEM((D_FF, D), BF16)],
    )
    return pl.pallas_call(
        _moe_kernel,
        grid_spec=grid_spec,
        out_shape=jax.ShapeDtypeStruct((A, D // 2), jnp.uint32),
        compiler_params=_cparams(("arbitrary",), 56),
        name="moe_experts",
    )(item_blk, item_exp, start, count, n_items, xs, w1, b1, w2, b2)


def _combine_kernel(x1_ref, yg_ref, tg_ref, o_ref):
    half = D_MODEL // 2
    acc_lo = x1_ref[:, :half]
    acc_hi = x1_ref[:, half:]
    g = tg_ref[...].T
    for kk in range(TOP_K):
        lo, hi = _unpack_rows(yg_ref[kk])
        acc_lo = acc_lo + g[:, kk:kk + 1] * lo
        acc_hi = acc_hi + g[:, kk:kk + 1] * hi
    o_ref[:, :half] = acc_lo
    o_ref[:, half:] = acc_hi


def _combine(x1, yg, tg, *, tm):
    T, D = x1.shape
    return pl.pallas_call(
        _combine_kernel,
        grid=(T // tm,),
        in_specs=[pl.BlockSpec((tm, D), lambda i: (i, 0)), pl.BlockSpec((TOP_K, tm, D // 2), lambda i: (0, i, 0)),
                  pl.BlockSpec((TOP_K, tm), lambda i: (0, i))],
        out_specs=pl.BlockSpec((tm, D), lambda i: (i, 0)),
        out_shape=jax.ShapeDtypeStruct((T, D), F32),
        compiler_params=_cparams(("parallel",), 48),
        name="moe_combine",
    )(x1, yg, tg)


def _lookup(table, idx):
    n = table.shape[0]
    sel = idx[None] == jnp.arange(n, dtype=jnp.int32).reshape((n,) + (1,) * idx.ndim)
    return jnp.sum(jnp.where(sel, table.reshape((n,) + (1,) * idx.ndim), 0), axis=0)


def _route(top_i, rank, counts, *, tm):
    T = top_i.shape[1]
    A = T * TOP_K
    counts = counts.reshape(N_EXPERTS).astype(jnp.int32)
    start = jnp.cumsum(counts) - counts
    flat_e = top_i.reshape(A // LANES, LANES)
    slot_of = (_lookup(start, flat_e) + rank.reshape(A // LANES, LANES)).reshape(A)
    first_blk = start // tm
    n_it = jnp.where(counts > 0, (start + counts - 1) // tm - first_blk + 1, 0)
    it_end = jnp.cumsum(n_it)
    n_items = it_end[-1]
    n_max = A // tm + N_EXPERTS
    i = jnp.minimum(jnp.arange(n_max, dtype=jnp.int32), n_items - 1)
    item_exp = jnp.sum(it_end[None, :] <= i[:, None], axis=1, dtype=jnp.int32)
    item_blk = _lookup(first_blk, item_exp) + i - _lookup(it_end - n_it, item_exp)
    return slot_of, item_blk.astype(jnp.int32), item_exp, start, counts, n_items.reshape(1)


def _trunk(x, meta_proj, w, tabs, *, tm_prep, tm_tok, tm_moe, attn_blk):
    B, S, D = x.shape
    T = B * S
    hy_m, _, k_m, vt_m, _ = meta_proj
    assert tm_prep == HY_PAD
    front = jnp.concatenate([jnp.zeros((HY_PAD - N_META, COL_HY), BF16), hy_m[0]], axis=0)
    hy_ext, qt, k, vt, sg = _prep(x, front, w['g1'], w['w_in'], w['wqt'], w['wvt'], w['bd'], w['qg'], w['kg'],
                                  tm=tm_prep)
    kt = _hyena_filter(w['fw1'], w['fb1'], w['ffr1'], w['fw2'], w['fb2'], w['ffr2'], w['fw3'], s_real=S)
    zt, a, bt = _hy_pre(hy_ext, w['scw'], w['scb'], w['skip'], s_real=S)
    ya = _hy_post(_hy_conv(kt, zt), a, bt)
    tiles, bmeta = tabs
    yb = _attention(qt, k, vt, k_m[0], vt_m[0], tiles, bmeta, w['lam'], w['subln'], blk=attn_blk)
    x1, h2, top_i, top_g, rank, counts = _merge(
        x.reshape(T, D), ya.reshape(T, HY_WIDTH), yb.reshape(T, ATT_WIDTH), sg.reshape(T, COL_G),
        w['pa'], w['pb'], w['wo'], w['g2'], w['wr_hi'], w['wr_lo'], w['br'], tm=tm_tok)
    slot_of, item_blk, item_exp, start, cnt, n_items = _route(top_i, rank, counts, tm=tm_moe)
    xs = _sc_sorted_rows(h2, slot_of)
    ys = _moe_experts(item_blk, item_exp, start, cnt, n_items, xs, w['w1'], w['b1'], w['w2'], w['b2'], tm=tm_moe)
    yg = _sc_gather_rows(ys, slot_of).reshape(TOP_K, T, D // 2)
    return _combine(x1, yg, top_g, tm=tm_tok).reshape(B, S, D)


def kernel(x_prompt, x_sample, meta_tokens, rel_bias, norm1_g, w_in, short_conv_w, short_conv_b, filt_w1, filt_b1,
           filt_freq1, filt_w2, filt_b2, filt_freq2, filt_w3, hy_skip, proj_a, q_norm_g, k_norm_g, lambda_q1,
           lambda_k1, lambda_q2, lambda_k2, subln_g, proj_b, w_out, norm2_g, w_router, b_router, w_mlp1, b_mlp1,
           w_mlp2, b_mlp2):
    l = 0
    n_hm = COL_Q // ATT_HEAD_DIM
    w_in_bf = w_in[l].astype(BF16)
    c_q, c_v = COL_HY, COL_HY + COL_Q + COL_K
    wr_hi = w_router[l].astype(BF16)
    w = {
        'g1': norm1_g[l][None], 'w_in': w_in_bf,
        'wqt': w_in_bf[:, c_q:c_q + COL_Q].T, 'wvt': w_in_bf[:, c_v:c_v + COL_V].T,
        'bd': jnp.kron(jnp.eye(n_hm, dtype=F32), jnp.full((ATT_HEAD_DIM, ATT_HEAD_DIM), 1.0 / ATT_HEAD_DIM, F32)
                       ).astype(BF16),
        'qg': (jnp.tile(q_norm_g[l], n_hm) * (ATT_HEAD_DIM ** -0.5 * LOG2E))[:, None],
        'kg': jnp.tile(k_norm_g[l], n_hm)[None],
        'wr_hi': wr_hi, 'wr_lo': (w_router[l] - wr_hi.astype(F32)).astype(BF16),
        'fw1': filt_w1[l], 'fb1': filt_b1[l], 'ffr1': filt_freq1[l], 'fw2': filt_w2[l], 'fb2': filt_b2[l],
        'ffr2': filt_freq2[l], 'fw3': filt_w3[l],
        'scw': short_conv_w[l], 'scb': short_conv_b[l], 'skip': hy_skip[l],
        'lam': jnp.stack([lambda_q1[l], lambda_k1[l], lambda_q2[l], lambda_k2[l]]), 'subln': subln_g[l][:, None],
        'pa': proj_a[l].astype(BF16), 'pb': proj_b[l].astype(BF16), 'wo': w_out[l].astype(BF16),
        'g2': norm2_g[l][None], 'br': b_router[l][None],
        'w1': w_mlp1[l], 'b1': b_mlp1[l][:, None, :], 'w2': w_mlp2[l], 'b2': b_mlp2[l][:, None, :],
    }
    meta_proj = _prep(meta_tokens[None], None, w['g1'], w['w_in'], w['wqt'], w['wvt'], w['bd'], w['qg'], w['kg'],
                      tm=N_META)
    tabs = _attn_bias_tables(rel_bias, TILES['attn_blk'])
    y_prompt = _trunk(x_prompt, meta_proj, w, tabs, **TILES)
    y_sample = _trunk(x_sample, meta_proj, w, tabs, **TILES)
    return (y_prompt, y_sample)
```
